```python
import jax, jax.numpy as jnp
from jax import lax
import numpy as np

D_MODEL = 1024
BATCH = 4
SEQ = 4096
DEPTH = 2
DEC_BATCH = 32
DEC_SEQ = 16
PAST_LEN = 2048

CHUNK = 64
EPS = 1e-6
N_EVEN = (DEPTH + 1) // 2
N_ODD = DEPTH // 2
A_HEADS = 4
A_DK = 128
A_DV = 128
A_KW = A_HEADS * A_DK
A_VW = A_HEADS * A_DV
B_HEADS = 4
B_DK = 128
B_DV = 128
B_KW = B_HEADS * B_DK
B_VW = B_HEADS * B_DV
ROPE_BASE = 10000.0
IN0_WIDTH = 2 * A_KW + 2 * A_VW + 2 * B_KW + 2 * B_VW
MIX_WIDTH = A_VW + B_VW
C_HEADS = 4
C_INNER = 2 * D_MODEL
C_DH = C_INNER // C_HEADS
C_CONV = 4
C_BLOCK = 4
C_NBLK = C_INNER // C_BLOCK
D_FF = -(-8 * D_MODEL // (3 * 256)) * 256

kernel_name = 'hybrid_hgrn2_retnet_mlstm_stream_step'


def _rmsnorm(x, g):
    xf = x.astype(jnp.float32)
    y = xf * lax.rsqrt(jnp.mean(xf * xf, axis=-1, keepdims=True) + EPS)
    return (y * g.astype(jnp.float32)).astype(x.dtype)


def _head_layernorm(x, g):
    xf = x.astype(jnp.float32)
    xc = xf - jnp.mean(xf, axis=-1, keepdims=True)
    var = jnp.mean(xc * xc, axis=-1, keepdims=True)
    return xc * lax.rsqrt(var + EPS) * g.astype(jnp.float32)


def _to_chunks(t, L):
    b, T, h, d = t.shape
    return t.reshape(b, T // L, L, h, d).transpose(1, 0, 3, 2, 4)


def _from_chunks(t):
    nc, b, h, L, d = t.shape
    return t.transpose(1, 0, 3, 2, 4).reshape(b, nc * L, h, d)


def _gates_to_chunks(t, L):
    b, T, h = t.shape
    return t.reshape(b, T // L, L, h).transpose(1, 0, 3, 2)


def _rotary(t, pos):
    half = t.shape[-1] // 2
    inv = ROPE_BASE ** (-jnp.arange(half, dtype=jnp.float32) / half)
    ang = pos[:, None] * inv[None, :]
    cos = jnp.cos(ang)[None, :, None, :]
    sin = jnp.sin(ang)[None, :, None, :]
    t1, t2 = t[..., :half], t[..., half:]
    return jnp.concatenate([t1 * cos - t2 * sin, t1 * sin + t2 * cos], axis=-1)


def _blockdiag(t, w):
    b, T, n = t.shape
    y = jnp.einsum('btni,nio->btno', t.reshape(b, T, C_NBLK, C_BLOCK), w.astype(jnp.float32))
    return y.reshape(b, T, n)


def _hgrn2_chunked(q, k, v, logf, S0):
    T = q.shape[1]
    L = min(CHUNK, T)
    causal = jnp.tril(jnp.ones((L, L), dtype=bool))

    def step(S, inp):
        qb, kb, vb, fb = inp
        b = jnp.cumsum(fb, axis=2)
        o_inter = jnp.einsum('bhjk,bhkv->bhjv', qb * jnp.exp(b), S)
        diff = b[:, :, :, None, :] - b[:, :, None, :, :]
        decay = jnp.exp(jnp.where(causal[:, :, None], diff, -jnp.inf))
        att = jnp.einsum('bhjc,bhjsc,bhsc->bhjs', qb, decay, kb)
        o = o_inter + jnp.einsum('bhjs,bhsv->bhjv', att, vb)
        bL = b[:, :, -1:, :]
        S_new = jnp.exp(bL[:, :, 0, :])[..., None] * S + jnp.einsum('bhsk,bhsv->bhkv', kb * jnp.exp(bL - b), vb)
        return S_new, o

    xs = (_to_chunks(q, L), _to_chunks(k, L), _to_chunks(v, L), _to_chunks(logf, L))
    S, o = lax.scan(step, S0, xs)
    return _from_chunks(o), S


def _retention_chunked(q, k, v, log_gamma, S0):
    T = q.shape[1]
    L = min(CHUNK, T)
    idx = jnp.arange(L, dtype=jnp.float32)
    rel = idx[:, None] - idx[None, :]
    D = jnp.where(rel >= 0, jnp.exp(log_gamma[:, None, None] * jnp.maximum(rel, 0.0)), 0.0)
    w_in = jnp.exp(log_gamma[:, None] * (idx + 1.0))
    w_tail = jnp.exp(log_gamma[:, None] * (L - 1.0 - idx))
    g_L = jnp.exp(log_gamma * L)

    def step(S, inp):
        qb, kb, vb = inp
        att = jnp.einsum('bhjd,bhsd->bhjs', qb, kb) * D[None]
        o = jnp.einsum('bhjs,bhsv->bhjv', att, vb) + w_in[None, :, :, None] * jnp.einsum('bhjd,bhdv->bhjv', qb, S)
        S_new = g_L[None, :, None, None] * S + jnp.einsum('bhsd,bhsv->bhdv', kb * w_tail[None, :, :, None], vb)
        return S_new, o

    xs = (_to_chunks(q, L), _to_chunks(k, L), _to_chunks(v, L))
    S, o = lax.scan(step, S0, xs)
    return _from_chunks(o), S


def _mlstm_chunked(q, k, v, ig, logf, C0, n0, m0):
    T = q.shape[1]
    L = min(CHUNK, T)
    causal = jnp.tril(jnp.ones((L, L), dtype=bool))

    def step(carry, inp):
        C, n, m = carry
        qb, kb, vb, ib, fb = inp
        b = jnp.cumsum(fb, axis=-1)
        lw = jnp.where(causal, b[..., :, None] - b[..., None, :] + ib[..., None, :], -jnp.inf)
        lp = b + m[..., None]
        mj = jnp.maximum(lp, jnp.max(lw, axis=-1))
        w = jnp.exp(lw - mj[..., None])
        wp = jnp.exp(lp - mj)
        s = jnp.einsum('bhjd,bhsd->bhjs', qb, kb) * w
        num = jnp.einsum('bhjs,bhsv->bhjv', s, vb) + wp[..., None] * jnp.einsum('bhvd,bhjd->bhjv', C, qb)
        den = jnp.sum(s, axis=-1) + wp * jnp.einsum('bhd,bhjd->bhj', n, qb)
        h = num / jnp.maximum(jnp.abs(den), jnp.exp(-mj))[..., None]
        m_new = mj[..., -1]
        bL = b[..., -1]
        wsL = jnp.exp(bL[..., None] - b + ib - m_new[..., None])
        wpL = jnp.exp(bL + m - m_new)
        C_new = wpL[..., None, None] * C + jnp.einsum('bhs,bhsv,bhsd->bhvd', wsL, vb, kb)
        n_new = wpL[..., None] * n + jnp.einsum('bhs,bhsd->bhd', wsL, kb)
        return (C_new, n_new, m_new), h

    xs = (_to_chunks(q, L), _to_chunks(k, L), _to_chunks(v, L), _gates_to_chunks(ig, L), _gates_to_chunks(logf, L))
    (C, n, m), h = lax.scan(step, (C0, n0, m0), xs)
    return _from_chunks(h), C, n, m


def _trunk(x, pos0, st_hgrn, st_ret, st_c, st_n, st_m, st_conv, p):
    f32 = jnp.float32
    B, T, _ = x.shape
    pos = (jnp.arange(T) + pos0).astype(f32)
    hd = lambda t, H: t.reshape(B, T, H, -1)
    new_hgrn, new_ret, new_c, new_n, new_m, new_conv = [], [], [], [], [], []
    splits = [int(s) for s in np.cumsum([A_KW, A_KW, A_VW, A_VW, B_KW, B_KW, B_VW])]
    for l in range(DEPTH):
        h = _rmsnorm(x, p['norm_mix'][l])
        if l % 2 == 0:
            a = l // 2
            z = (h @ p['w_in0'][a]).astype(f32)
            aq, af, ai, ag, bq, bk, bv, bg = jnp.split(z, splits, axis=-1)
            lb = jnp.cumsum(jax.nn.softmax(p['lb_logits'].astype(f32), axis=0), axis=0)[a]
            f = lb + (1.0 - lb) * jax.nn.sigmoid(af)
            oA, SA = _hgrn2_chunked(hd(aq, A_HEADS), hd(1.0 - f, A_HEADS), hd(ai, A_HEADS),
                                    hd(jnp.log(f), A_HEADS), st_hgrn[a].astype(f32))
            oA = _rmsnorm(oA * jax.nn.sigmoid(hd(ag, A_HEADS)), p['hgrn_norm'][a])
            qB = _rotary(hd(bq, B_HEADS), pos)
            kB = _rotary(hd(bk, B_HEADS), pos) * (B_DK ** -0.5)
            log_gamma = jnp.log1p(-jnp.exp2(-5.0 - jnp.arange(B_HEADS, dtype=f32)))
            oB, SB = _retention_chunked(qB, kB, hd(bv, B_HEADS), log_gamma, st_ret[a].astype(f32))
            oB = _rmsnorm(oB, p['ret_norm'][a]) * jax.nn.silu(hd(bg, B_HEADS))
            mix = jnp.concatenate([oA.reshape(B, T, A_VW), oB.reshape(B, T, B_VW)], axis=-1).astype(x.dtype)
            x = x + mix @ p['w_out0'][a]
            new_hgrn.append(SA)
            new_ret.append(SB)
        else:
            c = l // 2
            up = (h @ p['w_up1'][c]).astype(f32)
            xm, zg = jnp.split(up, 2, axis=-1)
            xpad = jnp.concatenate([st_conv[c].astype(f32), xm], axis=1)
            cw = p['conv_w'][c].astype(f32)
            xc = p['conv_b'][c].astype(f32) + xpad[:, 0:T] * cw[0]
            for w in range(1, C_CONV):
                xc = xc + xpad[:, w:w + T] * cw[w]
            xc = jax.nn.silu(xc)
            q = _blockdiag(xc, p['w_q1'][c])
            k = _blockdiag(xc, p['w_k1'][c]) * (C_DH ** -0.5)
            v = _blockdiag(xm, p['w_v1'][c])
            gin = jnp.concatenate([q, k, v], axis=-1)
            ig = gin @ p['w_ig'][c].astype(f32) + p['b_ig'][c].astype(f32)
            logf = jax.nn.log_sigmoid(gin @ p['w_fg'][c].astype(f32) + p['b_fg'][c].astype(f32))
            hC, Cn, nn_, mn = _mlstm_chunked(hd(q, C_HEADS), hd(k, C_HEADS), hd(v, C_HEADS), ig, logf,
                                            st_c[c].astype(f32), st_n[c].astype(f32), st_m[c].astype(f32))
            hC = _head_layernorm(hC, p['mlstm_norm'][c]).reshape(B, T, C_INNER) + p['mlstm_skip'][c].astype(f32) * xc
            x = x + (hC * jax.nn.silu(zg)).astype(x.dtype) @ p['w_down1'][c]
            new_c.append(Cn)
            new_n.append(nn_)
            new_m.append(mn)
            new_conv.append(xpad[:, T:])
        h = _rmsnorm(x, p['norm_ffn'][l])
        x = x + (jax.nn.silu(h @ p['w_ffn_gate'][l]) * (h @ p['w_ffn_up'][l])) @ p['w_ffn_down'][l]
    y = _rmsnorm(x, p['norm_final'])
    st = lambda lst: jnp.stack(lst, axis=0).astype(x.dtype)
    return y, st(new_hgrn), st(new_ret), st(new_c), st(new_n), st(new_m), st(new_conv)


def setup_inputs(seed: int = 0) -> dict:
    key = jax.random.key(seed)
    ks = iter(jax.random.split(key, 48))
    f32 = jnp.float32
    nrm = lambda shape, scale: scale * jax.random.normal(next(ks), shape, f32)
    gain = lambda shape: 1.0 + 0.05 * jax.random.normal(next(ks), shape, f32)
    return {
        'x_prompt': nrm((BATCH, SEQ, D_MODEL), 1.0),
        'x_sample': nrm((DEC_BATCH, DEC_SEQ, D_MODEL), 1.0),
        'state_hgrn': nrm((N_EVEN, DEC_BATCH, A_HEADS, A_DK, A_DV), 0.5),
        'state_ret': nrm((N_EVEN, DEC_BATCH, B_HEADS, B_DK, B_DV), 0.5),
        'state_mlstm_c': nrm((N_ODD, DEC_BATCH, C_HEADS, C_DH, C_DH), 0.1),
        'state_mlstm_n': nrm((N_ODD, DEC_BATCH, C_HEADS, C_DH), 0.1),
        'state_mlstm_m': nrm((N_ODD, DEC_BATCH, C_HEADS), 1.0),
        'state_conv': nrm((N_ODD, DEC_BATCH, C_CONV - 1, C_INNER), 1.0),
        'norm_mix': gain((DEPTH, D_MODEL)),
        'norm_ffn': gain((DEPTH, D_MODEL)),
        'norm_final': gain((D_MODEL,)),
        'w_in0': nrm((N_EVEN, D_MODEL, IN0_WIDTH), D_MODEL ** -0.5),
        'lb_logits': nrm((N_EVEN + 1, A_KW), 0.5),
        'hgrn_norm': gain((N_EVEN, A_DV)),
        'ret_norm': gain((N_EVEN, B_DV)),
        'w_out0': nrm((N_EVEN, MIX_WIDTH, D_MODEL), MIX_WIDTH ** -0.5),
        'w_up1': nrm((N_ODD, D_MODEL, 2 * C_INNER), D_MODEL ** -0.5),
        'conv_w': nrm((N_ODD, C_CONV, C_INNER), C_CONV ** -0.5),
        'conv_b': nrm((N_ODD, C_INNER), 0.02),
        'w_q1': nrm((N_ODD, C_NBLK, C_BLOCK, C_BLOCK), C_BLOCK ** -0.5),
        'w_k1': nrm((N_ODD, C_NBLK, C_BLOCK, C_BLOCK), C_BLOCK ** -0.5),
        'w_v1': nrm((N_ODD, C_NBLK, C_BLOCK, C_BLOCK), C_BLOCK ** -0.5),
        'w_ig': nrm((N_ODD, 3 * C_INNER, C_HEADS), (3 * C_INNER) ** -0.5),
        'b_ig': nrm((N_ODD, C_HEADS), 0.1),
        'w_fg': nrm((N_ODD, 3 * C_INNER, C_HEADS), (3 * C_INNER) ** -0.5),
        'b_fg': jnp.linspace(3.0, 6.0, C_HEADS, dtype=f32)[None, :] + nrm((N_ODD, C_HEADS), 0.1),
        'mlstm_norm': gain((N_ODD, C_DH)),
        'mlstm_skip': gain((N_ODD, C_INNER)),
        'w_down1': nrm((N_ODD, C_INNER, D_MODEL), C_INNER ** -0.5),
        'w_ffn_gate': nrm((DEPTH, D_MODEL, D_FF), D_MODEL ** -0.5),
        'w_ffn_up': nrm((DEPTH, D_MODEL, D_FF), D_MODEL ** -0.5),
        'w_ffn_down': nrm((DEPTH, D_FF, D_MODEL), D_FF ** -0.5),
    }


def reference(x_prompt, x_sample, state_hgrn, state_ret, state_mlstm_c, state_mlstm_n, state_mlstm_m, state_conv,
              norm_mix, norm_ffn, norm_final, w_in0, lb_logits, hgrn_norm, ret_norm, w_out0,
              w_up1, conv_w, conv_b, w_q1, w_k1, w_v1, w_ig, b_ig, w_fg, b_fg, mlstm_norm, mlstm_skip, w_down1,
              w_ffn_gate, w_ffn_up, w_ffn_down):
    p = dict(norm_mix=norm_mix, norm_ffn=norm_ffn, norm_final=norm_final, w_in0=w_in0, lb_logits=lb_logits,
             hgrn_norm=hgrn_norm, ret_norm=ret_norm, w_out0=w_out0, w_up1=w_up1, conv_w=conv_w, conv_b=conv_b,
             w_q1=w_q1, w_k1=w_k1, w_v1=w_v1, w_ig=w_ig, b_ig=b_ig, w_fg=w_fg, b_fg=b_fg,
             mlstm_norm=mlstm_norm, mlstm_skip=mlstm_skip, w_down1=w_down1,
             w_ffn_gate=w_ffn_gate, w_ffn_up=w_ffn_up, w_ffn_down=w_ffn_down)
    zeros = lambda shape: jnp.zeros(shape, x_prompt.dtype)
    y_prompt, hg_p, rt_p, mc_p, mn_p, mm_p, cv_p = _trunk(
        x_prompt, 0,
        zeros((N_EVEN, BATCH, A_HEADS, A_DK, A_DV)), zeros((N_EVEN, BATCH, B_HEADS, B_DK, B_DV)),
        zeros((N_ODD, BATCH, C_HEADS, C_DH, C_DH)), zeros((N_ODD, BATCH, C_HEADS, C_DH)),
        zeros((N_ODD, BATCH, C_HEADS)), zeros((N_ODD, BATCH, C_CONV - 1, C_INNER)), p)
    y_sample, hg_s, rt_s, mc_s, mn_s, mm_s, cv_s = _trunk(
        x_sample, PAST_LEN, state_hgrn, state_ret, state_mlstm_c, state_mlstm_n, state_mlstm_m, state_conv, p)
    return (y_prompt, y_sample, hg_p, hg_s, rt_p, rt_s, mc_p, mc_s, mn_p, mn_s, mm_p, mm_s, cv_p, cv_s)
```

```python
import functools
import math

import numpy as np
import jax
import jax.numpy as jnp
from jax import lax
from jax.experimental import pallas as pl
from jax.experimental.pallas import tpu as pltpu

F32 = jnp.float32
BF16 = jnp.bfloat16

D_MODEL = 1024
CHUNK = 64
EPS = 1e-6
N_HEADS = 4
HEAD_DK = 128
KW = N_HEADS * HEAD_DK
IN0_WIDTH = 8 * KW
ROPE_BASE = 10000.0
C_INNER = 2 * D_MODEL
C_DH = C_INNER // N_HEADS
C_CONV = 4
C_BLOCK = 4
BD_GROUP = 256
D_FF = -(-8 * D_MODEL // (3 * 256)) * 256
GATE_LANES = 128
CONV_HDR = 8
VMEM_LIMIT_BYTES = 56 * 1024 * 1024
ROW_TILE = 256


def _dot(a, b):
    return jnp.dot(a.astype(BF16), b.astype(BF16), preferred_element_type=F32)


def _dot_nt(a, b):
    return lax.dot_general(a.astype(BF16), b.astype(BF16), (((1,), (1,)), ((), ())),
                           preferred_element_type=F32)


def _dot_tn(a, b):
    return lax.dot_general(a.astype(BF16), b.astype(BF16), (((0,), (0,)), ((), ())),
                           preferred_element_type=F32)


def _exact_dot(sel, x):
    hi = x.astype(BF16)
    lo = (x - hi.astype(F32)).astype(BF16)
    return (jnp.dot(sel, hi, preferred_element_type=F32) + jnp.dot(sel, lo, preferred_element_type=F32))


def _rms(x, g):
    return x * lax.rsqrt(jnp.mean(x * x, axis=-1, keepdims=True) + EPS) * g


def _sigmoid(x):
    return 1.0 / (1.0 + jnp.exp(-x))


def _silu(x):
    return x * _sigmoid(x)


def _log_sigmoid(x):
    return jnp.minimum(x, 0.0) - jnp.log(1.0 + jnp.exp(-jnp.abs(x)))


def _norm_matmul_body(x_ref, g_ref, w_ref, o_ref):
    h = _rms(x_ref[...], g_ref[...])
    o_ref[...] = jnp.dot(h.astype(BF16), w_ref[...], preferred_element_type=F32)


def _norm_matmul(x, g, w):
    m, d = x.shape
    n = w.shape[1]
    return pl.pallas_call(
        _norm_matmul_body,
        grid=(m // ROW_TILE,),
        in_specs=[pl.BlockSpec((ROW_TILE, d), lambda i: (i, 0)),
                  pl.BlockSpec((1, d), lambda i: (0, 0)),
                  pl.BlockSpec((d, n), lambda i: (0, 0), pipeline_mode=pl.Buffered(1))],
        out_specs=pl.BlockSpec((ROW_TILE, n), lambda i: (i, 0)),
        out_shape=jax.ShapeDtypeStruct((m, n), F32),
        compiler_params=pltpu.CompilerParams(dimension_semantics=("arbitrary",),
                                             vmem_limit_bytes=VMEM_LIMIT_BYTES),
        name="norm_matmul",
    )(x, g.reshape(1, d), w)


def _proj_ffn_body(x_ref, a_ref, wo_ref, g_ref, wg_ref, wu_ref, wd_ref, gf_ref, o_ref, *, final_norm):
    x1 = x_ref[...] + jnp.dot(a_ref[...], wo_ref[...], preferred_element_type=F32)
    h = _rms(x1, g_ref[...]).astype(BF16)
    gate = jnp.dot(h, wg_ref[...], preferred_element_type=F32)
    up = jnp.dot(h, wu_ref[...], preferred_element_type=F32)
    t = (_silu(gate) * up).astype(BF16)
    x2 = x1 + jnp.dot(t, wd_ref[...], preferred_element_type=F32)
    if final_norm:
        x2 = _rms(x2, gf_ref[...])
    o_ref[...] = x2


def _proj_ffn(x, a, wo, g, wg, wu, wd, gf, final_norm):
    m, d = x.shape
    ka = a.shape[1]
    ff = wg.shape[1]
    const = lambda shape: pl.BlockSpec(shape, lambda i: (0, 0), pipeline_mode=pl.Buffered(1))
    return pl.pallas_call(
        functools.partial(_proj_ffn_body, final_norm=final_norm),
        grid=(m // ROW_TILE,),
        in_specs=[pl.BlockSpec((ROW_TILE, d), lambda i: (i, 0)),
                  pl.BlockSpec((ROW_TILE, ka), lambda i: (i, 0)),
                  const((ka, d)), const((1, d)), const((d, ff)), const((d, ff)), const((ff, d)),
                  const((1, d))],
        out_specs=pl.BlockSpec((ROW_TILE, d), lambda i: (i, 0)),
        out_shape=jax.ShapeDtypeStruct((m, d), F32),
        compiler_params=pltpu.CompilerParams(dimension_semantics=("arbitrary",),
                                             vmem_limit_bytes=VMEM_LIMIT_BYTES),
        name="proj_ffn",
    )(x, a, wo, g.reshape(1, d), wg, wu, wd, gf.reshape(1, d))


def _hgrn_consts(L):
    r = np.arange(L)[:, None]
    t = np.arange(L)[None, :]
    sels = [t <= r]
    masks = [r == t]
    h = L // 2
    while h >= 1:
        base = (r // (2 * h)) * (2 * h)
        upper = (r % (2 * h)) >= h
        sels.append(np.where(upper, (t >= base + h) & (t <= r), (t > r) & (t <= base + h - 1)))
        masks.append((r // (2 * h) == t // (2 * h)) & upper & ((t % (2 * h)) < h))
        h //= 2
    sel = np.concatenate(sels, axis=0).astype(np.float32)
    return jnp.asarray(sel, BF16), jnp.asarray(np.stack(masks).astype(np.float32))


def _retention_consts(L):
    lg = np.log1p(-np.exp2(-5.0 - np.arange(N_HEADS, dtype=np.float32))).astype(np.float32)
    idx = np.arange(L, dtype=np.float32)
    rel = idx[:, None] - idx[None, :]
    dmat = np.where(rel >= 0, np.exp(lg[:, None, None] * np.maximum(rel, 0.0)), 0.0)
    w_in = np.exp(lg[:, None] * (idx + 1.0))
    w_tail = np.exp(lg[:, None] * (L - 1.0 - idx))
    bcast = lambda v: jnp.asarray(np.broadcast_to(v[:, :, None], (N_HEADS, L, HEAD_DK)).astype(np.float32))
    g_chunk = [float(v) for v in np.exp(lg * L).astype(np.float32)]
    return jnp.asarray(dmat.astype(np.float32)), bcast(w_in), bcast(w_tail), g_chunk


def _rope_tables(pos0, T):
    half = HEAD_DK // 2
    inv = ROPE_BASE ** (-jnp.arange(half, dtype=F32) / half)
    ang = (jnp.arange(T) + pos0).astype(F32)[:, None] * inv[None, :]
    cos, sin = jnp.cos(ang), jnp.sin(ang)
    return jnp.concatenate([cos, cos], axis=-1), jnp.concatenate([-sin, sin], axis=-1)


def _mixer0_body(z_ref, cos_ref, sin_ref, lbl_ref, ga_ref, gb_ref, sel_ref, mask_ref, dmat_ref,
                 win_ref, wtail_ref, sa0_ref, sb0_ref, mix_ref, sa_ref, sb_ref, sat_scr, sbt_scr,
                 *, L, layer_slot, g_chunk):
    c = pl.program_id(1)
    last = pl.num_programs(1) - 1
    n_lev = mask_ref.shape[0] - 1

    @pl.when(c == 0)
    def _():
        for hh in range(N_HEADS):
            sat_scr[hh] = sa0_ref[0, hh].T
            sbt_scr[hh] = sb0_ref[0, hh].T

    lbl = lbl_ref[...]
    e = jnp.exp(lbl - jnp.max(lbl, axis=0, keepdims=True))
    lb = jnp.sum(e[:layer_slot + 1], axis=0, keepdims=True) / jnp.sum(e, axis=0, keepdims=True)

    f = lb + (1.0 - lb) * _sigmoid(z_ref[:, KW:2 * KW])
    logf = jnp.log(f)
    expo = _exact_dot(sel_ref[...], logf)
    cosf = cos_ref[...]
    sinf = sin_ref[...]

    for hh in range(N_HEADS):
        hs = slice(hh * HEAD_DK, (hh + 1) * HEAD_DK)
        col = lambda j: slice(j * KW + hh * HEAD_DK, j * KW + (hh + 1) * HEAD_DK)

        q = z_ref[:, col(0)]
        k = 1.0 - f[:, hs]
        v = z_ref[:, col(2)]
        b = expo[0:L, hs]
        b_last = b[L - 1:L, :]
        st = sat_scr[hh]
        o = _dot_nt(q * jnp.exp(b), st)
        att = mask_ref[0] * _dot_nt(q, k)
        for lev in range(n_lev):
            sc = jnp.exp(expo[(lev + 1) * L:(lev + 2) * L, hs])
            att = att + mask_ref[lev + 1] * _dot_nt(q * sc, k * sc)
        o = o + _dot(att, v)
        st_new = st * jnp.exp(b_last) + _dot_tn(v, k * jnp.exp(b_last - b))
        sat_scr[hh] = st_new
        o = _rms(o * _sigmoid(z_ref[:, col(3)]), ga_ref[...])
        mix_ref[:, hs] = o.astype(mix_ref.dtype)

        rq = z_ref[:, col(4)]
        rk = z_ref[:, col(5)]
        rv = z_ref[:, col(6)]
        rq = rq * cosf + pltpu.roll(rq, HEAD_DK // 2, 1) * sinf
        rk = (rk * cosf + pltpu.roll(rk, HEAD_DK // 2, 1) * sinf) * (HEAD_DK ** -0.5)
        rt = sbt_scr[hh]
        ro = _dot(_dot_nt(rq, rk) * dmat_ref[hh], rv) + win_ref[hh] * _dot_nt(rq, rt)
        rt_new = g_chunk[hh] * rt + _dot_tn(rv, rk * wtail_ref[hh])
        sbt_scr[hh] = rt_new
        ro = _rms(ro, gb_ref[...]) * _silu(z_ref[:, col(7)])
        mix_ref[:, KW + hh * HEAD_DK:KW + (hh + 1) * HEAD_DK] = ro.astype(mix_ref.dtype)

        @pl.when(c == last)
        def _():
            sa_ref[0, hh] = st_new.T
            sb_ref[0, hh] = rt_new.T


def _mixer0(z, row0, n_streams, T, pos0, sa0, sb0, lb_logits, ga, gb, layer_slot):
    L = min(CHUNK, T)
    nc = T // L
    blk0 = row0 // L
    sel, masks = _hgrn_consts(L)
    dmat, w_in, w_tail, g_chunk = _retention_consts(L)
    cosf, sinf = _rope_tables(pos0, T)
    full = lambda a: pl.BlockSpec(a.shape, lambda s, c: (0,) * a.ndim)
    state_spec = pl.BlockSpec((1, N_HEADS, HEAD_DK, HEAD_DK), lambda s, c: (s, 0, 0, 0))
    state_shape = jax.ShapeDtypeStruct((n_streams, N_HEADS, HEAD_DK, HEAD_DK), F32)
    ga2, gb2 = ga.reshape(1, HEAD_DK), gb.reshape(1, HEAD_DK)
    return pl.pallas_call(
        functools.partial(_mixer0_body, L=L, layer_slot=layer_slot, g_chunk=g_chunk),
        grid=(n_streams, nc),
        in_specs=[pl.BlockSpec((L, IN0_WIDTH), lambda s, c: (blk0 + s * nc + c, 0)),
                  pl.BlockSpec((L, HEAD_DK), lambda s, c: (c, 0)),
                  pl.BlockSpec((L, HEAD_DK), lambda s, c: (c, 0)),
                  full(lb_logits), full(ga2), full(gb2), full(sel), full(masks), full(dmat),
                  full(w_in), full(w_tail), state_spec, state_spec],
        out_specs=[pl.BlockSpec((L, 2 * KW), lambda s, c: (s * nc + c, 0)), state_spec, state_spec],
        out_shape=[jax.ShapeDtypeStruct((n_streams * T, 2 * KW), BF16), state_shape, state_shape],
        scratch_shapes=[pltpu.VMEM((N_HEADS, HEAD_DK, HEAD_DK), F32),
                        pltpu.VMEM((N_HEADS, HEAD_DK, HEAD_DK), F32)],
        compiler_params=pltpu.CompilerParams(dimension_semantics=("arbitrary", "arbitrary"),
                                             vmem_limit_bytes=VMEM_LIMIT_BYTES),
        name="mixer0_T%d" % T,
    )(z, cosf, sinf, lb_logits, ga2, gb2, sel, masks, dmat, w_in, w_tail, sa0, sb0)


def _mixer1_body(up_ref, cw_ref, cb_ref, wq_ref, wk_ref, wv_ref, wgate_ref, bgate_ref, tri_ref,
                 gn_ref, skip_ref, c0_ref, n0_ref, m0_ref, conv0_ref,
                 hg_ref, c_ref, n_ref, m_ref, conv_ref, xbuf, *, L):
    c = pl.program_id(1)
    last = pl.num_programs(1) - 1
    n_taps = C_CONV - 1

    @pl.when(c == 0)
    def _():
        c_ref[...] = c0_ref[...]
        n_ref[...] = n0_ref[...]
        m_ref[...] = m0_ref[...]
        xbuf[CONV_HDR - n_taps:CONV_HDR, :] = conv0_ref[0]

    xm = up_ref[:, :C_INNER]
    zg = up_ref[:, C_INNER:]

    xbuf[CONV_HDR:CONV_HDR + L, :] = xm
    xc = cb_ref[...] + xm * cw_ref[n_taps:n_taps + 1, :]
    for w in range(n_taps):
        xc = xc + xbuf[CONV_HDR - n_taps + w:CONV_HDR - n_taps + w + L, :] * cw_ref[w:w + 1, :]
    tail = xbuf[CONV_HDR + L - n_taps:CONV_HDR + L, :]
    xbuf[CONV_HDR - n_taps:CONV_HDR, :] = tail
    xc = _silu(xc)

    xcb = xc.astype(BF16)
    xmb = xm.astype(BF16)
    qs, ks, vs = [], [], []
    for g in range(C_INNER // BD_GROUP):
        gs = slice(g * BD_GROUP, (g + 1) * BD_GROUP)
        qs.append(jnp.dot(xcb[:, gs], wq_ref[g], preferred_element_type=F32))
        ks.append(jnp.dot(xcb[:, gs], wk_ref[g], preferred_element_type=F32))
        vs.append(jnp.dot(xmb[:, gs], wv_ref[g], preferred_element_type=F32))
    q = jnp.concatenate(qs, axis=1)
    k = jnp.concatenate(ks, axis=1) * (C_DH ** -0.5)
    v = jnp.concatenate(vs, axis=1)

    gates = (_dot(q, wgate_ref[0:C_INNER, :]) + _dot(k, wgate_ref[C_INNER:2 * C_INNER, :])
             + _dot(v, wgate_ref[2 * C_INNER:3 * C_INNER, :]) + bgate_ref[...])
    bcum = _exact_dot(tri_ref[...], _log_sigmoid(gates))
    lane = lax.broadcasted_iota(jnp.int32, gates.shape, 1)
    rows = jnp.where(lane < N_HEADS, gates, bcum).T
    ti = lax.broadcasted_iota(jnp.int32, (L, L), 0)
    si = lax.broadcasted_iota(jnp.int32, (L, L), 1)
    causal = si <= ti

    for hh in range(N_HEADS):
        hs = slice(hh * C_DH, (hh + 1) * C_DH)
        b_col = bcum[:, N_HEADS + hh:N_HEADS + hh + 1]
        i_col = gates[:, hh:hh + 1]
        b_row = rows[N_HEADS + hh:N_HEADS + hh + 1, :]
        i_row = rows[hh:hh + 1, :]
        m_prev = m_ref[0, :, hh:hh + 1]
        cm = c_ref[0, hh]
        nv = n_ref[0, hh:hh + 1, :]
        qh, kh, vh = q[:, hs], k[:, hs], v[:, hs]

        lw = jnp.where(causal, b_col - b_row + i_row, -jnp.inf)
        lp = b_col + m_prev
        mj = jnp.maximum(lp, jnp.max(lw, axis=-1, keepdims=True))
        wgt = jnp.exp(lw - mj)
        wp = jnp.exp(lp - mj)
        s = _dot_nt(qh, kh) * wgt
        num = _dot(s, vh) + wp * _dot_nt(qh, cm)
        den = jnp.sum(s, axis=-1, keepdims=True) + wp * jnp.sum(qh * nv, axis=-1, keepdims=True)
        hcell = num / jnp.maximum(jnp.abs(den), jnp.exp(-mj))

        m_new = mj[L - 1:L, :]
        b_last = b_col[L - 1:L, :]
        ws = jnp.exp(b_last - b_col + i_col - m_new)
        wpl = jnp.exp(b_last + m_prev - m_new)
        c_ref[0, hh] = wpl * cm + _dot_tn(vh * ws, kh)
        n_ref[0, hh:hh + 1, :] = wpl * nv + jnp.sum(kh * ws, axis=0, keepdims=True)
        m_ref[0, :, hh:hh + 1] = m_new

        hc = hcell - jnp.mean(hcell, axis=-1, keepdims=True)
        hn = hc * lax.rsqrt(jnp.mean(hc * hc, axis=-1, keepdims=True) + EPS) * gn_ref[...]
        out = (hn + skip_ref[:, hs] * xc[:, hs]) * _silu(zg[:, hs])
        hg_ref[:, hs] = out.astype(hg_ref.dtype)

    @pl.when(c == last)
    def _():
        conv_ref[0] = tail


def _mixer1(up, row0, n_streams, T, c0, n0, m0, conv0, cw, cb, wq, wk, wv, wgate, bgate, gn, skip):
    L = min(CHUNK, T)
    nc = T // L
    blk0 = row0 // L
    tri = jnp.asarray(np.tril(np.ones((L, L), np.float32)), BF16)
    full = lambda a: pl.BlockSpec(a.shape, lambda s, c: (0,) * a.ndim)
    per_stream = lambda a: pl.BlockSpec((1,) + a.shape[1:], lambda s, c: (s,) + (0,) * (a.ndim - 1))
    gn2, skip2, cb2 = gn.reshape(1, C_DH), skip.reshape(1, C_INNER), cb.reshape(1, C_INNER)
    m0 = m0.reshape(n_streams, 1, N_HEADS)
    consts = (cw, cb2, wq, wk, wv, wgate, bgate, tri, gn2, skip2)
    states = (c0, n0, m0, conv0)
    hg, c1, n1, m1, conv1 = pl.pallas_call(
        functools.partial(_mixer1_body, L=L),
        grid=(n_streams, nc),
        in_specs=[pl.BlockSpec((L, 2 * C_INNER), lambda s, c: (blk0 + s * nc + c, 0))]
                 + [full(a) for a in consts] + [per_stream(a) for a in states],
        out_specs=[pl.BlockSpec((L, C_INNER), lambda s, c: (s * nc + c, 0))]
                  + [per_stream(a) for a in states],
        out_shape=[jax.ShapeDtypeStruct((n_streams * T, C_INNER), BF16)]
                  + [jax.ShapeDtypeStruct(a.shape, F32) for a in states],
        scratch_shapes=[pltpu.VMEM((CONV_HDR + L, C_INNER), F32)],
        compiler_params=pltpu.CompilerParams(dimension_semantics=("arbitrary", "arbitrary"),
                                             vmem_limit_bytes=VMEM_LIMIT_BYTES),
        name="mixer1_T%d" % T,
    )(up, *consts, *states)
    return hg, c1, n1, m1.reshape(n_streams, N_HEADS), conv1


def _dense_blockdiag(w):
    per = BD_GROUP // C_BLOCK
    w4 = w.reshape(-1, per, C_BLOCK, C_BLOCK)
    eye = jnp.eye(per, dtype=w.dtype)
    return jnp.einsum('gnio,nm->gnimo', w4, eye).reshape(-1, BD_GROUP, BD_GROUP).astype(BF16)


def kernel(x_prompt, x_sample, state_hgrn, state_ret, state_mlstm_c, state_mlstm_n, state_mlstm_m, state_conv,
           norm_mix, norm_ffn, norm_final, w_in0, lb_logits, hgrn_norm, ret_norm, w_out0,
           w_up1, conv_w, conv_b, w_q1, w_k1, w_v1, w_ig, b_ig, w_fg, b_fg, mlstm_norm, mlstm_skip, w_down1,
           w_ffn_gate, w_ffn_up, w_ffn_down):
    bp, tp, d = x_prompt.shape
    bs, ts, _ = x_sample.shape
    mp, ms = bp * tp, bs * ts
    past_len = 2048
    bf = lambda w: w.astype(BF16)
    zeros = lambda *shape: jnp.zeros(shape, F32)

    x = jnp.concatenate([x_prompt.reshape(mp, d), x_sample.reshape(ms, d)], axis=0)

    z = _norm_matmul(x, norm_mix[0], bf(w_in0[0]))
    mix_p, hg_p, rt_p = _mixer0(z, 0, bp, tp, 0, zeros(bp, N_HEADS, HEAD_DK, HEAD_DK),
                                zeros(bp, N_HEADS, HEAD_DK, HEAD_DK), lb_logits, hgrn_norm[0], ret_norm[0], 0)
    mix_s, hg_s, rt_s = _mixer0(z, mp, bs, ts, past_len, state_hgrn[0], state_ret[0],
                                lb_logits, hgrn_norm[0], ret_norm[0], 0)
    mix = jnp.concatenate([mix_p, mix_s], axis=0)
    x = _proj_ffn(x, mix, bf(w_out0[0]), norm_ffn[0], bf(w_ffn_gate[0]), bf(w_ffn_up[0]), bf(w_ffn_down[0]),
                  norm_final, False)

    up = _norm_matmul(x, norm_mix[1], bf(w_up1[0]))
    wgate = jnp.pad(jnp.concatenate([w_ig[0], w_fg[0]], axis=1), ((0, 0), (0, GATE_LANES - 2 * N_HEADS))).astype(BF16)
    bgate = jnp.pad(jnp.concatenate([b_ig[0], b_fg[0]]), (0, GATE_LANES - 2 * N_HEADS)).reshape(1, GATE_LANES)
    m1_consts = (conv_w[0], conv_b[0], _dense_blockdiag(w_q1[0]), _dense_blockdiag(w_k1[0]),
                 _dense_blockdiag(w_v1[0]), wgate, bgate, mlstm_norm[0], mlstm_skip[0])
    hg1_p, mc_p, mn_p, mm_p, cv_p = _mixer1(up, 0, bp, tp, zeros(bp, N_HEADS, C_DH, C_DH), zeros(bp, N_HEADS, C_DH),
                                            zeros(bp, N_HEADS), zeros(bp, C_CONV - 1, C_INNER), *m1_consts)
    hg1_s, mc_s, mn_s, mm_s, cv_s = _mixer1(up, mp, bs, ts, state_mlstm_c[0], state_mlstm_n[0], state_mlstm_m[0],
                                            state_conv[0], *m1_consts)
    hg1 = jnp.concatenate([hg1_p, hg1_s], axis=0)
    y = _proj_ffn(x, hg1, bf(w_down1[0]), norm_ffn[1], bf(w_ffn_gate[1]), bf(w_ffn_up[1]), bf(w_ffn_down[1]),
                  norm_final, True)

    lead = lambda a: a[None]
    return (y[:mp].reshape(bp, tp, d), y[mp:].reshape(bs, ts, d),
            lead(hg_p), lead(hg_s), lead(rt_p), lead(rt_s),
            lead(mc_p), lead(mc_s), lead(mn_p), lead(mn_s), lead(mm_p), lead(mm_s), lead(cv_p), lead(cv_s))
```

```python
import functools
import math

import numpy as np
import jax
import jax.numpy as jnp
from jax import lax
from jax.experimental import pallas as pl
from jax.experimental.pallas import tpu as pltpu

F32 = jnp.float32
BF16 = jnp.bfloat16

D_MODEL = 1024
CHUNK0 = 256
CHUNK1 = 256
SEL_BLOCK = 64
EPS = 1e-6
N_HEADS = 4
HEAD_DK = 128
KW = N_HEADS * HEAD_DK
IN0_WIDTH = 8 * KW
ROPE_BASE = 10000.0
C_INNER = 2 * D_MODEL
C_DH = C_INNER // N_HEADS
C_CONV = 4
C_BLOCK = 4
BD_GROUP = 256
D_FF = -(-8 * D_MODEL // (3 * 256)) * 256
GATE_LANES = 128
CONV_HDR = 8
VMEM_LIMIT_BYTES = 56 * 1024 * 1024
ROW_TILE = 256


def _dot(a, b):
    return jnp.dot(a.astype(BF16), b.astype(BF16), preferred_element_type=F32)


def _dot_nt(a, b):
    return lax.dot_general(a.astype(BF16), b.astype(BF16), (((1,), (1,)), ((), ())),
                           preferred_element_type=F32)


def _dot_tn(a, b):
    return lax.dot_general(a.astype(BF16), b.astype(BF16), (((0,), (0,)), ((), ())),
                           preferred_element_type=F32)


def _exact_dot(sel, x):
    hi = x.astype(BF16)
    lo = (x - hi.astype(F32)).astype(BF16)
    return (jnp.dot(sel, hi, preferred_element_type=F32) + jnp.dot(sel, lo, preferred_element_type=F32))


def _rms(x, g):
    return x * lax.rsqrt(jnp.mean(x * x, axis=-1, keepdims=True) + EPS) * g


def _sigmoid(x):
    return 1.0 / (1.0 + jnp.exp(-x))


def _silu(x):
    return x * _sigmoid(x)


def _log_sigmoid(x):
    return jnp.minimum(x, 0.0) - jnp.log(1.0 + jnp.exp(-jnp.abs(x)))


def _norm_matmul_body(x_ref, g_ref, w_ref, o_ref):
    h = _rms(x_ref[...], g_ref[...])
    o_ref[...] = jnp.dot(h.astype(BF16), w_ref[...], preferred_element_type=F32)


def _norm_matmul(x, g, w):
    m, d = x.shape
    n = w.shape[1]
    return pl.pallas_call(
        _norm_matmul_body,
        grid=(m // ROW_TILE,),
        in_specs=[pl.BlockSpec((ROW_TILE, d), lambda i: (i, 0)),
                  pl.BlockSpec((1, d), lambda i: (0, 0)),
                  pl.BlockSpec((d, n), lambda i: (0, 0), pipeline_mode=pl.Buffered(1))],
        out_specs=pl.BlockSpec((ROW_TILE, n), lambda i: (i, 0)),
        out_shape=jax.ShapeDtypeStruct((m, n), F32),
        compiler_params=pltpu.CompilerParams(dimension_semantics=("arbitrary",),
                                             vmem_limit_bytes=VMEM_LIMIT_BYTES),
        name="norm_matmul",
    )(x, g.reshape(1, d), w)


def _proj_ffn_body(x_ref, a_ref, wo_ref, g_ref, wg_ref, wu_ref, wd_ref, gf_ref, o_ref, *, final_norm):
    x1 = x_ref[...] + jnp.dot(a_ref[...], wo_ref[...], preferred_element_type=F32)
    h = _rms(x1, g_ref[...]).astype(BF16)
    gate = jnp.dot(h, wg_ref[...], preferred_element_type=F32)
    up = jnp.dot(h, wu_ref[...], preferred_element_type=F32)
    t = (_silu(gate) * up).astype(BF16)
    x2 = x1 + jnp.dot(t, wd_ref[...], preferred_element_type=F32)
    if final_norm:
        x2 = _rms(x2, gf_ref[...])
    o_ref[...] = x2


def _proj_ffn(x, a, wo, g, wg, wu, wd, gf, final_norm):
    m, d = x.shape
    ka = a.shape[1]
    ff = wg.shape[1]
    const = lambda shape: pl.BlockSpec(shape, lambda i: (0, 0), pipeline_mode=pl.Buffered(1))
    return pl.pallas_call(
        functools.partial(_proj_ffn_body, final_norm=final_norm),
        grid=(m // ROW_TILE,),
        in_specs=[pl.BlockSpec((ROW_TILE, d), lambda i: (i, 0)),
                  pl.BlockSpec((ROW_TILE, ka), lambda i: (i, 0)),
                  const((ka, d)), const((1, d)), const((d, ff)), const((d, ff)), const((ff, d)),
                  const((1, d))],
        out_specs=pl.BlockSpec((ROW_TILE, d), lambda i: (i, 0)),
        out_shape=jax.ShapeDtypeStruct((m, d), F32),
        compiler_params=pltpu.CompilerParams(dimension_semantics=("arbitrary",),
                                             vmem_limit_bytes=VMEM_LIMIT_BYTES),
        name="proj_ffn",
    )(x, a, wo, g.reshape(1, d), wg, wu, wd, gf.reshape(1, d))


def _hgrn_consts(L):
    r = np.arange(L)[:, None]
    t = np.arange(L)[None, :]
    sels = [t <= r]
    masks = [r == t]
    h = L // 2
    while h >= 1:
        base = (r // (2 * h)) * (2 * h)
        upper = (r % (2 * h)) >= h
        sels.append(np.where(upper, (t >= base + h) & (t <= r), (t > r) & (t <= base + h - 1)))
        masks.append((r // (2 * h) == t // (2 * h)) & upper & ((t % (2 * h)) < h))
        h //= 2
    return np.concatenate(sels, axis=0).astype(np.float32), np.stack(masks).astype(np.float32)


def _decay_exponents(sel_ref, logf, L):
    blk = min(SEL_BLOCK, L)
    nb = L // blk
    n_loc = int(math.log2(blk))
    res = [_exact_dot(sel_ref[...], logf[i * blk:(i + 1) * blk]) for i in range(nb)]
    cum = [r[0:blk] for r in res]
    pre = [jnp.zeros_like(cum[0][0:1])]
    for i in range(nb):
        pre.append(pre[i] + cum[i][blk - 1:blk])
    b = jnp.concatenate([cum[i] + pre[i] for i in range(nb)], axis=0) if nb > 1 else cum[0]
    levels = []
    nbh = nb // 2
    while nbh >= 1:
        parts = []
        for i in range(nb):
            ref = pre[(i // (2 * nbh)) * 2 * nbh + nbh]
            if (i % (2 * nbh)) >= nbh:
                parts.append(cum[i] + (pre[i] - ref))
            else:
                parts.append((ref - pre[i]) - cum[i])
        levels.append(jnp.concatenate(parts, axis=0))
        nbh //= 2
    for lev in range(n_loc):
        parts = [r[(lev + 1) * blk:(lev + 2) * blk] for r in res]
        levels.append(jnp.concatenate(parts, axis=0) if nb > 1 else parts[0])
    return b, pre[nb], levels


def _retention_consts(L):
    lg = np.log1p(-np.exp2(-5.0 - np.arange(N_HEADS, dtype=np.float32))).astype(np.float32)
    idx = np.arange(L, dtype=np.float32)
    rel = idx[:, None] - idx[None, :]
    dmat = np.where(rel >= 0, np.exp(lg[:, None, None] * np.maximum(rel, 0.0)), 0.0)
    w_in = np.exp(lg[:, None] * (idx + 1.0))
    w_tail = np.exp(lg[:, None] * (L - 1.0 - idx))
    bcast = lambda v: jnp.asarray(np.broadcast_to(v[:, :, None], (N_HEADS, L, HEAD_DK)).astype(np.float32))
    g_chunk = [float(v) for v in np.exp(lg * L).astype(np.float32)]
    return jnp.asarray(dmat.astype(np.float32)), bcast(w_in), bcast(w_tail), g_chunk


def _rope_tables(pos0, T):
    half = HEAD_DK // 2
    inv = ROPE_BASE ** (-jnp.arange(half, dtype=F32) / half)
    ang = (jnp.arange(T) + pos0).astype(F32)[:, None] * inv[None, :]
    cos, sin = jnp.cos(ang), jnp.sin(ang)
    return jnp.concatenate([cos, cos], axis=-1), jnp.concatenate([-sin, sin], axis=-1)


def _mixer0_body(z_ref, cos_ref, sin_ref, lbl_ref, ga_ref, gb_ref, sel_ref, mask_ref, dmat_ref,
                 win_ref, wtail_ref, sa0_ref, sb0_ref, mix_ref, sa_ref, sb_ref, sat_scr, sbt_scr,
                 *, L, layer_slot, g_chunk):
    c = pl.program_id(1)
    last = pl.num_programs(1) - 1
    n_lev = mask_ref.shape[0] - 1

    @pl.when(c == 0)
    def _():
        for hh in range(N_HEADS):
            sat_scr[hh] = sa0_ref[0, hh].T
            sbt_scr[hh] = sb0_ref[0, hh].T

    lbl = lbl_ref[...]
    e = jnp.exp(lbl - jnp.max(lbl, axis=0, keepdims=True))
    lb = jnp.sum(e[:layer_slot + 1], axis=0, keepdims=True) / jnp.sum(e, axis=0, keepdims=True)

    f = lb + (1.0 - lb) * _sigmoid(z_ref[:, KW:2 * KW])
    bcum, bfin, expo = _decay_exponents(sel_ref, jnp.log(f), L)
    cosf = cos_ref[...]
    sinf = sin_ref[...]

    for hh in range(N_HEADS):
        hs = slice(hh * HEAD_DK, (hh + 1) * HEAD_DK)
        col = lambda j: slice(j * KW + hh * HEAD_DK, j * KW + (hh + 1) * HEAD_DK)

        q = z_ref[:, col(0)]
        k = 1.0 - f[:, hs]
        v = z_ref[:, col(2)]
        b = bcum[:, hs]
        b_last = bfin[:, hs]
        st = sat_scr[hh]
        o = _dot_nt(q * jnp.exp(b), st)
        att = mask_ref[0] * _dot_nt(q, k)
        for lev in range(n_lev):
            sc = jnp.exp(expo[lev][:, hs])
            att = att + mask_ref[lev + 1] * _dot_nt(q * sc, k * sc)
        o = o + _dot(att, v)
        st_new = st * jnp.exp(b_last) + _dot_tn(v, k * jnp.exp(b_last - b))
        sat_scr[hh] = st_new
        o = _rms(o * _sigmoid(z_ref[:, col(3)]), ga_ref[...])
        mix_ref[:, hs] = o.astype(mix_ref.dtype)

        rq = z_ref[:, col(4)]
        rk = z_ref[:, col(5)]
        rv = z_ref[:, col(6)]
        rq = rq * cosf + pltpu.roll(rq, HEAD_DK // 2, 1) * sinf
        rk = (rk * cosf + pltpu.roll(rk, HEAD_DK // 2, 1) * sinf) * (HEAD_DK ** -0.5)
        rt = sbt_scr[hh]
        ro = _dot(_dot_nt(rq, rk) * dmat_ref[hh], rv) + win_ref[hh] * _dot_nt(rq, rt)
        rt_new = g_chunk[hh] * rt + _dot_tn(rv, rk * wtail_ref[hh])
        sbt_scr[hh] = rt_new
        ro = _rms(ro, gb_ref[...]) * _silu(z_ref[:, col(7)])
        mix_ref[:, KW + hh * HEAD_DK:KW + (hh + 1) * HEAD_DK] = ro.astype(mix_ref.dtype)

        @pl.when(c == last)
        def _():
            sa_ref[0, hh] = st_new.T
            sb_ref[0, hh] = rt_new.T


def _mixer0(z, row0, n_streams, T, pos0, sa0, sb0, lb_logits, ga, gb, layer_slot):
    L = min(CHUNK0, T)
    nc = T // L
    blk0 = row0 // L
    sel = jnp.asarray(_hgrn_consts(min(SEL_BLOCK, L))[0], BF16)
    masks = jnp.asarray(_hgrn_consts(L)[1])
    dmat, w_in, w_tail, g_chunk = _retention_consts(L)
    cosf, sinf = _rope_tables(pos0, T)
    full = lambda a: pl.BlockSpec(a.shape, lambda s, c: (0,) * a.ndim, pipeline_mode=pl.Buffered(1))
    state_spec = pl.BlockSpec((1, N_HEADS, HEAD_DK, HEAD_DK), lambda s, c: (s, 0, 0, 0))
    state_shape = jax.ShapeDtypeStruct((n_streams, N_HEADS, HEAD_DK, HEAD_DK), F32)
    ga2, gb2 = ga.reshape(1, HEAD_DK), gb.reshape(1, HEAD_DK)
    return pl.pallas_call(
        functools.partial(_mixer0_body, L=L, layer_slot=layer_slot, g_chunk=g_chunk),
        grid=(n_streams, nc),
        in_specs=[pl.BlockSpec((L, IN0_WIDTH), lambda s, c: (blk0 + s * nc + c, 0)),
                  pl.BlockSpec((L, HEAD_DK), lambda s, c: (c, 0)),
                  pl.BlockSpec((L, HEAD_DK), lambda s, c: (c, 0)),
                  full(lb_logits), full(ga2), full(gb2), full(sel), full(masks), full(dmat),
                  full(w_in), full(w_tail), state_spec, state_spec],
        out_specs=[pl.BlockSpec((L, 2 * KW), lambda s, c: (s * nc + c, 0)), state_spec, state_spec],
        out_shape=[jax.ShapeDtypeStruct((n_streams * T, 2 * KW), BF16), state_shape, state_shape],
        scratch_shapes=[pltpu.VMEM((N_HEADS, HEAD_DK, HEAD_DK), F32),
                        pltpu.VMEM((N_HEADS, HEAD_DK, HEAD_DK), F32)],
        compiler_params=pltpu.CompilerParams(dimension_semantics=("arbitrary", "arbitrary"),
                                             vmem_limit_bytes=VMEM_LIMIT_BYTES),
        name="mixer0_T%d" % T,
    )(z, cosf, sinf, lb_logits, ga2, gb2, sel, masks, dmat, w_in, w_tail, sa0, sb0)


def _mixer1_body(up_ref, cw_ref, cb_ref, wq_ref, wk_ref, wv_ref, wgate_ref, bgate_ref, tri_ref,
                 gn_ref, skip_ref, c0_ref, n0_ref, m0_ref, conv0_ref,
                 hg_ref, c_ref, n_ref, m_ref, conv_ref, xbuf, *, L):
    c = pl.program_id(1)
    last = pl.num_programs(1) - 1
    n_taps = C_CONV - 1

    @pl.when(c == 0)
    def _():
        c_ref[...] = c0_ref[...]
        n_ref[...] = n0_ref[...]
        m_ref[...] = m0_ref[...]
        xbuf[CONV_HDR - n_taps:CONV_HDR, :] = conv0_ref[0]

    xm = up_ref[:, :C_INNER]
    zg = up_ref[:, C_INNER:]

    xbuf[CONV_HDR:CONV_HDR + L, :] = xm
    xc = cb_ref[...] + xm * cw_ref[n_taps:n_taps + 1, :]
    for w in range(n_taps):
        xc = xc + xbuf[CONV_HDR - n_taps + w:CONV_HDR - n_taps + w + L, :] * cw_ref[w:w + 1, :]
    tail = xbuf[CONV_HDR + L - n_taps:CONV_HDR + L, :]
    xbuf[CONV_HDR - n_taps:CONV_HDR, :] = tail
    xc = _silu(xc)

    xcb = xc.astype(BF16)
    xmb = xm.astype(BF16)
    qs, ks, vs = [], [], []
    for g in range(C_INNER // BD_GROUP):
        gs = slice(g * BD_GROUP, (g + 1) * BD_GROUP)
        qs.append(jnp.dot(xcb[:, gs], wq_ref[g], preferred_element_type=F32))
        ks.append(jnp.dot(xcb[:, gs], wk_ref[g], preferred_element_type=F32))
        vs.append(jnp.dot(xmb[:, gs], wv_ref[g], preferred_element_type=F32))
    q = jnp.concatenate(qs, axis=1)
    k = jnp.concatenate(ks, axis=1) * (C_DH ** -0.5)
    v = jnp.concatenate(vs, axis=1)

    gates = (_dot(q, wgate_ref[0:C_INNER, :]) + _dot(k, wgate_ref[C_INNER:2 * C_INNER, :])
             + _dot(v, wgate_ref[2 * C_INNER:3 * C_INNER, :]) + bgate_ref[...])
    bcum = _exact_dot(tri_ref[...], _log_sigmoid(gates))
    lane = lax.broadcasted_iota(jnp.int32, gates.shape, 1)
    rows = jnp.where(lane < N_HEADS, gates, bcum).T
    ti = lax.broadcasted_iota(jnp.int32, (L, L), 0)
    si = lax.broadcasted_iota(jnp.int32, (L, L), 1)
    causal = si <= ti

    for hh in range(N_HEADS):
        hs = slice(hh * C_DH, (hh + 1) * C_DH)
        b_col = bcum[:, N_HEADS + hh:N_HEADS + hh + 1]
        i_col = gates[:, hh:hh + 1]
        b_row = rows[N_HEADS + hh:N_HEADS + hh + 1, :]
        i_row = rows[hh:hh + 1, :]
        m_prev = m_ref[0, :, hh:hh + 1]
        cm = c_ref[0, hh]
        nv = n_ref[0, hh:hh + 1, :]
        qh, kh, vh = q[:, hs], k[:, hs], v[:, hs]

        lw = jnp.where(causal, b_col - b_row + i_row, -jnp.inf)
        lp = b_col + m_prev
        mj = jnp.maximum(lp, jnp.max(lw, axis=-1, keepdims=True))
        wgt = jnp.exp(lw - mj)
        wp = jnp.exp(lp - mj)
        s = _dot_nt(qh, kh) * wgt
        num = _dot(s, vh) + wp * _dot_nt(qh, cm)
        den = jnp.sum(s, axis=-1, keepdims=True) + wp * jnp.sum(qh * nv, axis=-1, keepdims=True)
        hcell = num / jnp.maximum(jnp.abs(den), jnp.exp(-mj))

        m_new = mj[L - 1:L, :]
        b_last = b_col[L - 1:L, :]
        ws = jnp.exp(b_last - b_col + i_col - m_new)
        wpl = jnp.exp(b_last + m_prev - m_new)
        c_ref[0, hh] = wpl * cm + _dot_tn(vh * ws, kh)
        n_ref[0, hh:hh + 1, :] = wpl * nv + jnp.sum(kh * ws, axis=0, keepdims=True)
        m_ref[0, :, hh:hh + 1] = m_new

        hc = hcell - jnp.mean(hcell, axis=-1, keepdims=True)
        hn = hc * lax.rsqrt(jnp.mean(hc * hc, axis=-1, keepdims=True) + EPS) * gn_ref[...]
        out = (hn + skip_ref[:, hs] * xc[:, hs]) * _silu(zg[:, hs])
        hg_ref[:, hs] = out.astype(hg_ref.dtype)

    @pl.when(c == last)
    def _():
        conv_ref[0] = tail


def _mixer1(up, row0, n_streams, T, c0, n0, m0, conv0, cw, cb, wq, wk, wv, wgate, bgate, gn, skip):
    L = min(CHUNK1, T)
    nc = T // L
    blk0 = row0 // L
    tri = jnp.asarray(np.tril(np.ones((L, L), np.float32)), BF16)
    full = lambda a: pl.BlockSpec(a.shape, lambda s, c: (0,) * a.ndim, pipeline_mode=pl.Buffered(1))
    per_stream = lambda a: pl.BlockSpec((1,) + a.shape[1:], lambda s, c: (s,) + (0,) * (a.ndim - 1))
    gn2, skip2, cb2 = gn.reshape(1, C_DH), skip.reshape(1, C_INNER), cb.reshape(1, C_INNER)
    m0 = m0.reshape(n_streams, 1, N_HEADS)
    consts = (cw, cb2, wq, wk, wv, wgate, bgate, tri, gn2, skip2)
    states = (c0, n0, m0, conv0)
    hg, c1, n1, m1, conv1 = pl.pallas_call(
        functools.partial(_mixer1_body, L=L),
        grid=(n_streams, nc),
        in_specs=[pl.BlockSpec((L, 2 * C_INNER), lambda s, c: (blk0 + s * nc + c, 0))]
                 + [full(a) for a in consts] + [per_stream(a) for a in states],
        out_specs=[pl.BlockSpec((L, C_INNER), lambda s, c: (s * nc + c, 0))]
                  + [per_stream(a) for a in states],
        out_shape=[jax.ShapeDtypeStruct((n_streams * T, C_INNER), BF16)]
                  + [jax.ShapeDtypeStruct(a.shape, F32) for a in states],
        scratch_shapes=[pltpu.VMEM((CONV_HDR + L, C_INNER), F32)],
        compiler_params=pltpu.CompilerParams(dimension_semantics=("arbitrary", "arbitrary"),
                                             vmem_limit_bytes=VMEM_LIMIT_BYTES),
        name="mixer1_T%d" % T,
    )(up, *consts, *states)
    return hg, c1, n1, m1.reshape(n_streams, N_HEADS), conv1


def _dense_blockdiag(w):
    per = BD_GROUP // C_BLOCK
    w4 = w.reshape(-1, per, C_BLOCK, C_BLOCK)
    eye = jnp.eye(per, dtype=w.dtype)
    return jnp.einsum('gnio,nm->gnimo', w4, eye).reshape(-1, BD_GROUP, BD_GROUP).astype(BF16)


def kernel(x_prompt, x_sample, state_hgrn, state_ret, state_mlstm_c, state_mlstm_n, state_mlstm_m, state_conv,
           norm_mix, norm_ffn, norm_final, w_in0, lb_logits, hgrn_norm, ret_norm, w_out0,
           w_up1, conv_w, conv_b, w_q1, w_k1, w_v1, w_ig, b_ig, w_fg, b_fg, mlstm_norm, mlstm_skip, w_down1,
           w_ffn_gate, w_ffn_up, w_ffn_down):
    bp, tp, d = x_prompt.shape
    bs, ts, _ = x_sample.shape
    mp, ms = bp * tp, bs * ts
    past_len = 2048
    bf = lambda w: w.astype(BF16)
    zeros = lambda *shape: jnp.zeros(shape, F32)

    x = jnp.concatenate([x_prompt.reshape(mp, d), x_sample.reshape(ms, d)], axis=0)

    z = _norm_matmul(x, norm_mix[0], bf(w_in0[0]))
    mix_p, hg_p, rt_p = _mixer0(z, 0, bp, tp, 0, zeros(bp, N_HEADS, HEAD_DK, HEAD_DK),
                                zeros(bp, N_HEADS, HEAD_DK, HEAD_DK), lb_logits, hgrn_norm[0], ret_norm[0], 0)
    mix_s, hg_s, rt_s = _mixer0(z, mp, bs, ts, past_len, state_hgrn[0], state_ret[0],
                                lb_logits, hgrn_norm[0], ret_norm[0], 0)
    mix = jnp.concatenate([mix_p, mix_s], axis=0)
    x = _proj_ffn(x, mix, bf(w_out0[0]), norm_ffn[0], bf(w_ffn_gate[0]), bf(w_ffn_up[0]), bf(w_ffn_down[0]),
                  norm_final, False)

    up = _norm_matmul(x, norm_mix[1], bf(w_up1[0]))
    wgate = jnp.pad(jnp.concatenate([w_ig[0], w_fg[0]], axis=1), ((0, 0), (0, GATE_LANES - 2 * N_HEADS))).astype(BF16)
    bgate = jnp.pad(jnp.concatenate([b_ig[0], b_fg[0]]), (0, GATE_LANES - 2 * N_HEADS)).reshape(1, GATE_LANES)
    m1_consts = (conv_w[0], conv_b[0], _dense_blockdiag(w_q1[0]), _dense_blockdiag(w_k1[0]),
                 _dense_blockdiag(w_v1[0]), wgate, bgate, mlstm_norm[0], mlstm_skip[0])
    hg1_p, mc_p, mn_p, mm_p, cv_p = _mixer1(up, 0, bp, tp, zeros(bp, N_HEADS, C_DH, C_DH), zeros(bp, N_HEADS, C_DH),
                                            zeros(bp, N_HEADS), zeros(bp, C_CONV - 1, C_INNER), *m1_consts)
    hg1_s, mc_s, mn_s, mm_s, cv_s = _mixer1(up, mp, bs, ts, state_mlstm_c[0], state_mlstm_n[0], state_mlstm_m[0],
                                            state_conv[0], *m1_consts)
    hg1 = jnp.concatenate([hg1_p, hg1_s], axis=0)
    y = _proj_ffn(x, hg1, bf(w_down1[0]), norm_ffn[1], bf(w_ffn_gate[1]), bf(w_ffn_up[1]), bf(w_ffn_down[1]),
                  norm_final, True)

    lead = lambda a: a[None]
    return (y[:mp].reshape(bp, tp, d), y[mp:].reshape(bs, ts, d),
            lead(hg_p), lead(hg_s), lead(rt_p), lead(rt_s),
            lead(mc_p), lead(mc_s), lead(mn_p), lead(mn_s), lead(mm_p), lead(mm_s), lead(cv_p), lead(cv_s))
```

```python
import functools
import math

import numpy as np
import jax
import jax.numpy as jnp
from jax import lax
from jax.experimental import pallas as pl
from jax.experimental.pallas import tpu as pltpu

F32 = jnp.float32
BF16 = jnp.bfloat16

D_MODEL = 1024
CHUNK0 = 256
CHUNK1 = 256
SEL_BLOCK = 64
EPS = 1e-6
N_HEADS = 4
HEAD_DK = 128
KW = N_HEADS * HEAD_DK
IN0_WIDTH = 8 * KW
ROPE_BASE = 10000.0
C_INNER = 2 * D_MODEL
C_DH = C_INNER // N_HEADS
C_CONV = 4
C_BLOCK = 4
BD_GROUP = 256
D_FF = -(-8 * D_MODEL // (3 * 256)) * 256
GATE_LANES = 128
CONV_HDR = 8
VMEM_LIMIT_BYTES = 56 * 1024 * 1024
ROW_TILE = 256


def _dot(a, b):
    return jnp.dot(a.astype(BF16), b.astype(BF16), preferred_element_type=F32)


def _dot_nt(a, b):
    return lax.dot_general(a.astype(BF16), b.astype(BF16), (((1,), (1,)), ((), ())),
                           preferred_element_type=F32)


def _dot_tn(a, b):
    return lax.dot_general(a.astype(BF16), b.astype(BF16), (((0,), (0,)), ((), ())),
                           preferred_element_type=F32)


def _exact_dot(sel, x):
    hi = x.astype(BF16)
    lo = (x - hi.astype(F32)).astype(BF16)
    return (jnp.dot(sel, hi, preferred_element_type=F32) + jnp.dot(sel, lo, preferred_element_type=F32))


def _rms(x, g):
    return x * lax.rsqrt(jnp.mean(x * x, axis=-1, keepdims=True) + EPS) * g


def _sigmoid(x):
    return 1.0 / (1.0 + jnp.exp(-x))


def _silu(x):
    return x * _sigmoid(x)


def _log_sigmoid(x):
    return jnp.minimum(x, 0.0) - jnp.log(1.0 + jnp.exp(-jnp.abs(x)))


def _norm_matmul_body(x_ref, g_ref, w_ref, o_ref):
    h = _rms(x_ref[...], g_ref[...])
    o_ref[...] = jnp.dot(h.astype(BF16), w_ref[...], preferred_element_type=F32)


def _norm_matmul(x, g, w):
    m, d = x.shape
    n = w.shape[1]
    return pl.pallas_call(
        _norm_matmul_body,
        grid=(m // ROW_TILE,),
        in_specs=[pl.BlockSpec((ROW_TILE, d), lambda i: (i, 0)),
                  pl.BlockSpec((1, d), lambda i: (0, 0)),
                  pl.BlockSpec((d, n), lambda i: (0, 0), pipeline_mode=pl.Buffered(1))],
        out_specs=pl.BlockSpec((ROW_TILE, n), lambda i: (i, 0)),
        out_shape=jax.ShapeDtypeStruct((m, n), F32),
        compiler_params=pltpu.CompilerParams(dimension_semantics=("arbitrary",),
                                             vmem_limit_bytes=VMEM_LIMIT_BYTES),
        name="norm_matmul",
    )(x, g.reshape(1, d), w)


def _proj_ffn_body(x_ref, a_ref, wo_ref, g_ref, wg_ref, wu_ref, wd_ref, gf_ref, o_ref, *, final_norm):
    x1 = x_ref[...] + jnp.dot(a_ref[...], wo_ref[...], preferred_element_type=F32)
    h = _rms(x1, g_ref[...]).astype(BF16)
    gate = jnp.dot(h, wg_ref[...], preferred_element_type=F32)
    up = jnp.dot(h, wu_ref[...], preferred_element_type=F32)
    t = (_silu(gate) * up).astype(BF16)
    x2 = x1 + jnp.dot(t, wd_ref[...], preferred_element_type=F32)
    if final_norm:
        x2 = _rms(x2, gf_ref[...])
    o_ref[...] = x2


def _proj_ffn(x, a, wo, g, wg, wu, wd, gf, final_norm):
    m, d = x.shape
    ka = a.shape[1]
    ff = wg.shape[1]
    const = lambda shape: pl.BlockSpec(shape, lambda i: (0, 0), pipeline_mode=pl.Buffered(1))
    return pl.pallas_call(
        functools.partial(_proj_ffn_body, final_norm=final_norm),
        grid=(m // ROW_TILE,),
        in_specs=[pl.BlockSpec((ROW_TILE, d), lambda i: (i, 0)),
                  pl.BlockSpec((ROW_TILE, ka), lambda i: (i, 0)),
                  const((ka, d)), const((1, d)), const((d, ff)), const((d, ff)), const((ff, d)),
                  const((1, d))],
        out_specs=pl.BlockSpec((ROW_TILE, d), lambda i: (i, 0)),
        out_shape=jax.ShapeDtypeStruct((m, d), F32),
        compiler_params=pltpu.CompilerParams(dimension_semantics=("arbitrary",),
                                             vmem_limit_bytes=VMEM_LIMIT_BYTES),
        name="proj_ffn",
    )(x, a, wo, g.reshape(1, d), wg, wu, wd, gf.reshape(1, d))


def _hgrn_consts(L):
    r = np.arange(L)[:, None]
    t = np.arange(L)[None, :]
    sels = [t <= r]
    masks = [r == t]
    h = L // 2
    while h >= 1:
        base = (r // (2 * h)) * (2 * h)
        upper = (r % (2 * h)) >= h
        sels.append(np.where(upper, (t >= base + h) & (t <= r), (t > r) & (t <= base + h - 1)))
        masks.append((r // (2 * h) == t // (2 * h)) & upper & ((t % (2 * h)) < h))
        h //= 2
    return np.concatenate(sels, axis=0).astype(np.float32), np.stack(masks).astype(np.float32)


def _decay_exponents(sel_ref, logf, L):
    blk = min(SEL_BLOCK, L)
    nb = L // blk
    n_loc = int(math.log2(blk))
    res = [_exact_dot(sel_ref[...], logf[i * blk:(i + 1) * blk]) for i in range(nb)]
    cum = [r[0:blk] for r in res]
    pre = [jnp.zeros_like(cum[0][0:1])]
    for i in range(nb):
        pre.append(pre[i] + cum[i][blk - 1:blk])
    b = jnp.concatenate([cum[i] + pre[i] for i in range(nb)], axis=0) if nb > 1 else cum[0]
    levels = []
    nbh = nb // 2
    while nbh >= 1:
        parts = []
        for i in range(nb):
            ref = pre[(i // (2 * nbh)) * 2 * nbh + nbh]
            if (i % (2 * nbh)) >= nbh:
                parts.append(cum[i] + (pre[i] - ref))
            else:
                parts.append((ref - pre[i]) - cum[i])
        levels.append(jnp.concatenate(parts, axis=0))
        nbh //= 2
    for lev in range(n_loc):
        parts = [r[(lev + 1) * blk:(lev + 2) * blk] for r in res]
        levels.append(jnp.concatenate(parts, axis=0) if nb > 1 else parts[0])
    return b, pre[nb], levels


def _retention_consts(L):
    lg = np.log1p(-np.exp2(-5.0 - np.arange(N_HEADS, dtype=np.float32))).astype(np.float32)
    idx = np.arange(L, dtype=np.float32)
    rel = idx[:, None] - idx[None, :]
    dmat = np.where(rel >= 0, np.exp(lg[:, None, None] * np.maximum(rel, 0.0)), 0.0)
    w_in = np.exp(lg[:, None] * (idx + 1.0))
    w_tail = np.exp(lg[:, None] * (L - 1.0 - idx))
    bcast = lambda v: jnp.asarray(np.broadcast_to(v[:, :, None], (N_HEADS, L, HEAD_DK)).astype(np.float32))
    g_chunk = [float(v) for v in np.exp(lg * L).astype(np.float32)]
    return jnp.asarray(dmat.astype(np.float32)), bcast(w_in), bcast(w_tail), g_chunk


def _rope_tables(pos0, T):
    half = HEAD_DK // 2
    inv = ROPE_BASE ** (-jnp.arange(half, dtype=F32) / half)
    ang = (jnp.arange(T) + pos0).astype(F32)[:, None] * inv[None, :]
    cos, sin = jnp.cos(ang), jnp.sin(ang)
    return jnp.concatenate([cos, cos], axis=-1), jnp.concatenate([-sin, sin], axis=-1)


def _mixer0_body(*refs, L, layer_slot, g_chunk, fused):
    n_src = 3 if fused else 1
    src = refs[:n_src]
    (cos_ref, sin_ref, lbl_ref, ga_ref, gb_ref, sel_ref, mask_ref, dmat_ref, win_ref, wtail_ref,
     sa0_ref, sb0_ref, mix_ref, sa_ref, sb_ref, sat_scr, sbt_scr) = refs[n_src:]
    c = pl.program_id(1)
    last = pl.num_programs(1) - 1
    n_lev = mask_ref.shape[0] - 1
    head_w = IN0_WIDTH // N_HEADS

    @pl.when(c == 0)
    def _():
        for hh in range(N_HEADS):
            sat_scr[hh] = sa0_ref[0, hh].T
            sbt_scr[hh] = sb0_ref[0, hh].T

    lbl = lbl_ref[...]
    e = jnp.exp(lbl - jnp.max(lbl, axis=0, keepdims=True))
    lb_all = jnp.sum(e[:layer_slot + 1], axis=0, keepdims=True) / jnp.sum(e, axis=0, keepdims=True)

    if fused:
        x_ref, g_ref, w_ref = src
        hb = _rms(x_ref[...], g_ref[...]).astype(BF16)
    cosf = cos_ref[...]
    sinf = sin_ref[...]

    def project(hh):
        hcols = slice(hh * head_w, (hh + 1) * head_w)
        return jnp.dot(hb, w_ref[:, hcols], preferred_element_type=F32) if fused else src[0][:, hcols]

    z_next = project(0)
    for hh in range(N_HEADS):
        hs = slice(hh * HEAD_DK, (hh + 1) * HEAD_DK)
        zh = z_next
        if hh + 1 < N_HEADS:
            z_next = project(hh + 1)
        part = lambda j: zh[:, j * HEAD_DK:(j + 1) * HEAD_DK]

        lb = lb_all[:, hs]
        f = lb + (1.0 - lb) * _sigmoid(part(1))
        b, b_last, expo = _decay_exponents(sel_ref, jnp.log(f), L)
        q = part(0)
        k = 1.0 - f
        v = part(2)
        st = sat_scr[hh]
        o = _dot_nt(q * jnp.exp(b), st)
        att = mask_ref[0] * _dot_nt(q, k)
        for lev in range(n_lev):
            sc = jnp.exp(expo[lev])
            att = att + mask_ref[lev + 1] * _dot_nt(q * sc, k * sc)
        o = o + _dot(att, v)
        st_new = st * jnp.exp(b_last) + _dot_tn(v, k * jnp.exp(b_last - b))
        sat_scr[hh] = st_new
        o = _rms(o * _sigmoid(part(3)), ga_ref[...])
        mix_ref[:, hs] = o.astype(mix_ref.dtype)

        rq = part(4)
        rk = part(5)
        rv = part(6)
        rq = rq * cosf + pltpu.roll(rq, HEAD_DK // 2, 1) * sinf
        rk = (rk * cosf + pltpu.roll(rk, HEAD_DK // 2, 1) * sinf) * (HEAD_DK ** -0.5)
        rt = sbt_scr[hh]
        ro = _dot(_dot_nt(rq, rk) * dmat_ref[hh], rv) + win_ref[hh] * _dot_nt(rq, rt)
        rt_new = g_chunk[hh] * rt + _dot_tn(rv, rk * wtail_ref[hh])
        sbt_scr[hh] = rt_new
        ro = _rms(ro, gb_ref[...]) * _silu(part(7))
        mix_ref[:, KW + hh * HEAD_DK:KW + (hh + 1) * HEAD_DK] = ro.astype(mix_ref.dtype)

        @pl.when(c == last)
        def _():
            sa_ref[0, hh] = st_new.T
            sb_ref[0, hh] = rt_new.T


def _head_major(w):
    d = w.shape[0]
    return w.reshape(d, 8, N_HEADS, HEAD_DK).transpose(0, 2, 1, 3).reshape(d, IN0_WIDTH)


def _mixer0(src, n_streams, T, pos0, sa0, sb0, lb_logits, ga, gb, layer_slot):
    L = min(CHUNK0, T)
    nc = T // L
    fused = len(src) == 3
    sel = jnp.asarray(_hgrn_consts(min(SEL_BLOCK, L))[0], BF16)
    masks = jnp.asarray(_hgrn_consts(L)[1])
    dmat, w_in, w_tail, g_chunk = _retention_consts(L)
    cosf, sinf = _rope_tables(pos0, T)
    full = lambda a: pl.BlockSpec(a.shape, lambda s, c: (0,) * a.ndim, pipeline_mode=pl.Buffered(1))
    rows = lambda a: pl.BlockSpec((L, a.shape[1]), lambda s, c: (s * nc + c, 0))
    state_spec = pl.BlockSpec((1, N_HEADS, HEAD_DK, HEAD_DK), lambda s, c: (s, 0, 0, 0))
    state_shape = jax.ShapeDtypeStruct((n_streams, N_HEADS, HEAD_DK, HEAD_DK), F32)
    ga2, gb2 = ga.reshape(1, HEAD_DK), gb.reshape(1, HEAD_DK)
    if fused:
        src = (src[0], src[1].reshape(1, -1), src[2])
        src_specs = [rows(src[0]), full(src[1]), full(src[2])]
    else:
        src_specs = [rows(src[0])]
    consts = (lb_logits, ga2, gb2, sel, masks, dmat, w_in, w_tail)
    return pl.pallas_call(
        functools.partial(_mixer0_body, L=L, layer_slot=layer_slot, g_chunk=g_chunk, fused=fused),
        grid=(n_streams, nc),
        in_specs=src_specs
                 + [pl.BlockSpec((L, HEAD_DK), lambda s, c: (c, 0)), pl.BlockSpec((L, HEAD_DK), lambda s, c: (c, 0))]
                 + [full(a) for a in consts] + [state_spec, state_spec],
        out_specs=[pl.BlockSpec((L, 2 * KW), lambda s, c: (s * nc + c, 0)), state_spec, state_spec],
        out_shape=[jax.ShapeDtypeStruct((n_streams * T, 2 * KW), BF16), state_shape, state_shape],
        scratch_shapes=[pltpu.VMEM((N_HEADS, HEAD_DK, HEAD_DK), F32),
                        pltpu.VMEM((N_HEADS, HEAD_DK, HEAD_DK), F32)],
        compiler_params=pltpu.CompilerParams(dimension_semantics=("arbitrary", "arbitrary"),
                                             vmem_limit_bytes=VMEM_LIMIT_BYTES),
        name="mixer0_T%d" % T,
    )(*src, cosf, sinf, *consts, sa0, sb0)


def _mixer1_body(*refs, L, fused):
    n_src = 3 if fused else 1
    src = refs[:n_src]
    (cw_ref, cb_ref, wq_ref, wk_ref, wv_ref, wgate_ref, bgate_ref, tri_ref, gn_ref, skip_ref,
     c0_ref, n0_ref, m0_ref, conv0_ref, hg_ref, c_ref, n_ref, m_ref, conv_ref, xbuf) = refs[n_src:]
    c = pl.program_id(1)
    last = pl.num_programs(1) - 1
    n_taps = C_CONV - 1

    @pl.when(c == 0)
    def _():
        c_ref[...] = c0_ref[...]
        n_ref[...] = n0_ref[...]
        m_ref[...] = m0_ref[...]
        xbuf[CONV_HDR - n_taps:CONV_HDR, :] = conv0_ref[0]

    if fused:
        x_ref, g_ref, w_ref = src
        hb = _rms(x_ref[...], g_ref[...]).astype(BF16)
        xm = jnp.dot(hb, w_ref[:, :C_INNER], preferred_element_type=F32)
        zg = jnp.dot(hb, w_ref[:, C_INNER:], preferred_element_type=F32)
    else:
        xm = src[0][:, :C_INNER]
        zg = src[0][:, C_INNER:]

    xbuf[CONV_HDR:CONV_HDR + L, :] = xm
    xc = cb_ref[...] + xm * cw_ref[n_taps:n_taps + 1, :]
    for w in range(n_taps):
        xc = xc + xbuf[CONV_HDR - n_taps + w:CONV_HDR - n_taps + w + L, :] * cw_ref[w:w + 1, :]
    tail = xbuf[CONV_HDR + L - n_taps:CONV_HDR + L, :]
    xbuf[CONV_HDR - n_taps:CONV_HDR, :] = tail
    xc = _silu(xc)

    xcb = xc.astype(BF16)
    xmb = xm.astype(BF16)
    qs, ks, vs = [], [], []
    for g in range(C_INNER // BD_GROUP):
        gs = slice(g * BD_GROUP, (g + 1) * BD_GROUP)
        qs.append(jnp.dot(xcb[:, gs], wq_ref[g], preferred_element_type=F32))
        ks.append(jnp.dot(xcb[:, gs], wk_ref[g], preferred_element_type=F32))
        vs.append(jnp.dot(xmb[:, gs], wv_ref[g], preferred_element_type=F32))
    q = jnp.concatenate(qs, axis=1)
    k = jnp.concatenate(ks, axis=1) * (C_DH ** -0.5)
    v = jnp.concatenate(vs, axis=1)

    gates = (_dot(q, wgate_ref[0:C_INNER, :]) + _dot(k, wgate_ref[C_INNER:2 * C_INNER, :])
             + _dot(v, wgate_ref[2 * C_INNER:3 * C_INNER, :]) + bgate_ref[...])
    bcum = _exact_dot(tri_ref[...], _log_sigmoid(gates))
    lane = lax.broadcasted_iota(jnp.int32, gates.shape, 1)
    rows = jnp.where(lane < N_HEADS, gates, bcum).T
    ti = lax.broadcasted_iota(jnp.int32, (L, L), 0)
    si = lax.broadcasted_iota(jnp.int32, (L, L), 1)
    causal = si <= ti

    for hh in range(N_HEADS):
        hs = slice(hh * C_DH, (hh + 1) * C_DH)
        b_col = bcum[:, N_HEADS + hh:N_HEADS + hh + 1]
        i_col = gates[:, hh:hh + 1]
        b_row = rows[N_HEADS + hh:N_HEADS + hh + 1, :]
        i_row = rows[hh:hh + 1, :]
        m_prev = m_ref[0, :, hh:hh + 1]
        cm = c_ref[0, hh]
        nv = n_ref[0, hh:hh + 1, :]
        qh, kh, vh = q[:, hs], k[:, hs], v[:, hs]

        lw = jnp.where(causal, b_col - b_row + i_row, -jnp.inf)
        lp = b_col + m_prev
        mj = jnp.maximum(lp, jnp.max(lw, axis=-1, keepdims=True))
        wgt = jnp.exp(lw - mj)
        wp = jnp.exp(lp - mj)
        s = _dot_nt(qh, kh) * wgt
        num = _dot(s, vh) + wp * _dot_nt(qh, cm)
        den = jnp.sum(s, axis=-1, keepdims=True) + wp * jnp.sum(qh * nv, axis=-1, keepdims=True)
        hcell = num / jnp.maximum(jnp.abs(den), jnp.exp(-mj))

        m_new = mj[L - 1:L, :]
        b_last = b_col[L - 1:L, :]
        ws = jnp.exp(b_last - b_col + i_col - m_new)
        wpl = jnp.exp(b_last + m_prev - m_new)
        c_ref[0, hh] = wpl * cm + _dot_tn(vh * ws, kh)
        n_ref[0, hh:hh + 1, :] = wpl * nv + jnp.sum(kh * ws, axis=0, keepdims=True)
        m_ref[0, :, hh:hh + 1] = m_new

        hc = hcell - jnp.mean(hcell, axis=-1, keepdims=True)
        hn = hc * lax.rsqrt(jnp.mean(hc * hc, axis=-1, keepdims=True) + EPS) * gn_ref[...]
        out = (hn + skip_ref[:, hs] * xc[:, hs]) * _silu(zg[:, hs])
        hg_ref[:, hs] = out.astype(hg_ref.dtype)

    @pl.when(c == last)
    def _():
        conv_ref[0] = tail


def _mixer1(src, n_streams, T, c0, n0, m0, conv0, cw, cb, wq, wk, wv, wgate, bgate, gn, skip):
    L = min(CHUNK1, T)
    nc = T // L
    fused = len(src) == 3
    tri = jnp.asarray(np.tril(np.ones((L, L), np.float32)), BF16)
    full = lambda a: pl.BlockSpec(a.shape, lambda s, c: (0,) * a.ndim, pipeline_mode=pl.Buffered(1))
    rows = lambda a: pl.BlockSpec((L, a.shape[1]), lambda s, c: (s * nc + c, 0))
    per_stream = lambda a: pl.BlockSpec((1,) + a.shape[1:], lambda s, c: (s,) + (0,) * (a.ndim - 1))
    gn2, skip2, cb2 = gn.reshape(1, C_DH), skip.reshape(1, C_INNER), cb.reshape(1, C_INNER)
    m0 = m0.reshape(n_streams, 1, N_HEADS)
    if fused:
        src = (src[0], src[1].reshape(1, -1), src[2])
        src_specs = [rows(src[0]), full(src[1]), full(src[2])]
    else:
        src_specs = [rows(src[0])]
    consts = (cw, cb2, wq, wk, wv, wgate, bgate, tri, gn2, skip2)
    states = (c0, n0, m0, conv0)
    hg, c1, n1, m1, conv1 = pl.pallas_call(
        functools.partial(_mixer1_body, L=L, fused=fused),
        grid=(n_streams, nc),
        in_specs=src_specs + [full(a) for a in consts] + [per_stream(a) for a in states],
        out_specs=[pl.BlockSpec((L, C_INNER), lambda s, c: (s * nc + c, 0))]
                  + [per_stream(a) for a in states],
        out_shape=[jax.ShapeDtypeStruct((n_streams * T, C_INNER), BF16)]
                  + [jax.ShapeDtypeStruct(a.shape, F32) for a in states],
        scratch_shapes=[pltpu.VMEM((CONV_HDR + L, C_INNER), F32)],
        compiler_params=pltpu.CompilerParams(dimension_semantics=("arbitrary", "arbitrary"),
                                             vmem_limit_bytes=VMEM_LIMIT_BYTES),
        name="mixer1_T%d" % T,
    )(*src, *consts, *states)
    return hg, c1, n1, m1.reshape(n_streams, N_HEADS), conv1


def _dense_blockdiag(w):
    per = BD_GROUP // C_BLOCK
    w4 = w.reshape(-1, per, C_BLOCK, C_BLOCK)
    eye = jnp.eye(per, dtype=w.dtype)
    return jnp.einsum('gnio,nm->gnimo', w4, eye).reshape(-1, BD_GROUP, BD_GROUP).astype(BF16)


def kernel(x_prompt, x_sample, state_hgrn, state_ret, state_mlstm_c, state_mlstm_n, state_mlstm_m, state_conv,
           norm_mix, norm_ffn, norm_final, w_in0, lb_logits, hgrn_norm, ret_norm, w_out0,
           w_up1, conv_w, conv_b, w_q1, w_k1, w_v1, w_ig, b_ig, w_fg, b_fg, mlstm_norm, mlstm_skip, w_down1,
           w_ffn_gate, w_ffn_up, w_ffn_down):
    bp, tp, d = x_prompt.shape
    bs, ts, _ = x_sample.shape
    mp, ms = bp * tp, bs * ts
    past_len = 2048
    bf = lambda w: w.astype(BF16)
    zeros = lambda *shape: jnp.zeros(shape, F32)

    xp = x_prompt.reshape(mp, d)
    xs = x_sample.reshape(ms, d)

    w_in = bf(_head_major(w_in0[0]))
    ffn0 = (bf(w_out0[0]), norm_ffn[0], bf(w_ffn_gate[0]), bf(w_ffn_up[0]), bf(w_ffn_down[0]), norm_final, False)
    mix_p, hg_p, rt_p = _mixer0((xp, norm_mix[0], w_in), bp, tp, 0, zeros(bp, N_HEADS, HEAD_DK, HEAD_DK),
                                zeros(bp, N_HEADS, HEAD_DK, HEAD_DK), lb_logits, hgrn_norm[0], ret_norm[0], 0)
    mix_s, hg_s, rt_s = _mixer0((_norm_matmul(xs, norm_mix[0], w_in),), bs, ts, past_len, state_hgrn[0],
                                state_ret[0], lb_logits, hgrn_norm[0], ret_norm[0], 0)
    xp = _proj_ffn(xp, mix_p, *ffn0)
    xs = _proj_ffn(xs, mix_s, *ffn0)

    w_up = bf(w_up1[0])
    ffn1 = (bf(w_down1[0]), norm_ffn[1], bf(w_ffn_gate[1]), bf(w_ffn_up[1]), bf(w_ffn_down[1]), norm_final, True)
    wgate = jnp.pad(jnp.concatenate([w_ig[0], w_fg[0]], axis=1), ((0, 0), (0, GATE_LANES - 2 * N_HEADS))).astype(BF16)
    bgate = jnp.pad(jnp.concatenate([b_ig[0], b_fg[0]]), (0, GATE_LANES - 2 * N_HEADS)).reshape(1, GATE_LANES)
    m1_consts = (conv_w[0], conv_b[0], _dense_blockdiag(w_q1[0]), _dense_blockdiag(w_k1[0]),
                 _dense_blockdiag(w_v1[0]), wgate, bgate, mlstm_norm[0], mlstm_skip[0])
    hg1_p, mc_p, mn_p, mm_p, cv_p = _mixer1((xp, norm_mix[1], w_up), bp, tp, zeros(bp, N_HEADS, C_DH, C_DH),
                                            zeros(bp, N_HEADS, C_DH), zeros(bp, N_HEADS),
                                            zeros(bp, C_CONV - 1, C_INNER), *m1_consts)
    hg1_s, mc_s, mn_s, mm_s, cv_s = _mixer1((_norm_matmul(xs, norm_mix[1], w_up),), bs, ts, state_mlstm_c[0],
                                            state_mlstm_n[0], state_mlstm_m[0], state_conv[0], *m1_consts)
    yp = _proj_ffn(xp, hg1_p, *ffn1)
    ys = _proj_ffn(xs, hg1_s, *ffn1)

    lead = lambda a: a[None]
    return (yp.reshape(bp, tp, d), ys.reshape(bs, ts, d),
            lead(hg_p), lead(hg_s), lead(rt_p), lead(rt_s),
            lead(mc_p), lead(mc_s), lead(mn_p), lead(mn_s), lead(mm_p), lead(mm_s), lead(cv_p), lead(cv_s))
```

```python
import functools
import math

import numpy as np
import jax
import jax.numpy as jnp
from jax import lax
from jax.experimental import pallas as pl
from jax.experimental.pallas import tpu as pltpu

F32 = jnp.float32
BF16 = jnp.bfloat16

D_MODEL = 1024
CHUNK0 = 256
CHUNK1 = 256
SEL_BLOCK = 64
HGRN_SUB = 128
EPS = 1e-6
N_HEADS = 4
HEAD_DK = 128
KW = N_HEADS * HEAD_DK
IN0_WIDTH = 8 * KW
ROPE_BASE = 10000.0
C_INNER = 2 * D_MODEL
C_DH = C_INNER // N_HEADS
C_CONV = 4
C_BLOCK = 4
BD_GROUP = 256
D_FF = -(-8 * D_MODEL // (3 * 256)) * 256
GATE_LANES = 128
CONV_HDR = 8
VMEM_LIMIT_BYTES = 56 * 1024 * 1024
ROW_TILE = 256


def _dot(a, b):
    return jnp.dot(a.astype(BF16), b.astype(BF16), preferred_element_type=F32)


def _dot_nt(a, b):
    return lax.dot_general(a.astype(BF16), b.astype(BF16), (((1,), (1,)), ((), ())),
                           preferred_element_type=F32)


def _dot_tn(a, b):
    return lax.dot_general(a.astype(BF16), b.astype(BF16), (((0,), (0,)), ((), ())),
                           preferred_element_type=F32)


def _exact_dot(sel, x):
    hi = x.astype(BF16)
    lo = (x - hi.astype(F32)).astype(BF16)
    return (jnp.dot(sel, hi, preferred_element_type=F32) + jnp.dot(sel, lo, preferred_element_type=F32))


def _rms(x, g):
    return x * lax.rsqrt(jnp.mean(x * x, axis=-1, keepdims=True) + EPS) * g


def _sigmoid(x):
    return 1.0 / (1.0 + jnp.exp(-x))


def _silu(x):
    return x * _sigmoid(x)


def _log_sigmoid(x):
    return jnp.minimum(x, 0.0) - jnp.log(1.0 + jnp.exp(-jnp.abs(x)))


def _norm_matmul_body(x_ref, g_ref, w_ref, o_ref):
    h = _rms(x_ref[...], g_ref[...])
    o_ref[...] = jnp.dot(h.astype(BF16), w_ref[...], preferred_element_type=F32)


def _norm_matmul(x, g, w):
    m, d = x.shape
    n = w.shape[1]
    return pl.pallas_call(
        _norm_matmul_body,
        grid=(m // ROW_TILE,),
        in_specs=[pl.BlockSpec((ROW_TILE, d), lambda i: (i, 0)),
                  pl.BlockSpec((1, d), lambda i: (0, 0)),
                  pl.BlockSpec((d, n), lambda i: (0, 0), pipeline_mode=pl.Buffered(1))],
        out_specs=pl.BlockSpec((ROW_TILE, n), lambda i: (i, 0)),
        out_shape=jax.ShapeDtypeStruct((m, n), F32),
        compiler_params=pltpu.CompilerParams(dimension_semantics=("arbitrary",),
                                             vmem_limit_bytes=VMEM_LIMIT_BYTES),
        name="norm_matmul",
    )(x, g.reshape(1, d), w)


def _proj_ffn_body(x_ref, a_ref, wo_ref, g_ref, wg_ref, wu_ref, wd_ref, gf_ref, o_ref, *, final_norm):
    x1 = x_ref[...] + jnp.dot(a_ref[...], wo_ref[...], preferred_element_type=F32)
    h = _rms(x1, g_ref[...]).astype(BF16)
    gate = jnp.dot(h, wg_ref[...], preferred_element_type=F32)
    up = jnp.dot(h, wu_ref[...], preferred_element_type=F32)
    t = (_silu(gate) * up).astype(BF16)
    x2 = x1 + jnp.dot(t, wd_ref[...], preferred_element_type=F32)
    if final_norm:
        x2 = _rms(x2, gf_ref[...])
    o_ref[...] = x2


def _proj_ffn(x, a, wo, g, wg, wu, wd, gf, final_norm):
    m, d = x.shape
    ka = a.shape[1]
    ff = wg.shape[1]
    const = lambda shape: pl.BlockSpec(shape, lambda i: (0, 0), pipeline_mode=pl.Buffered(1))
    return pl.pallas_call(
        functools.partial(_proj_ffn_body, final_norm=final_norm),
        grid=(m // ROW_TILE,),
        in_specs=[pl.BlockSpec((ROW_TILE, d), lambda i: (i, 0)),
                  pl.BlockSpec((ROW_TILE, ka), lambda i: (i, 0)),
                  const((ka, d)), const((1, d)), const((d, ff)), const((d, ff)), const((ff, d)),
                  const((1, d))],
        out_specs=pl.BlockSpec((ROW_TILE, d), lambda i: (i, 0)),
        out_shape=jax.ShapeDtypeStruct((m, d), F32),
        compiler_params=pltpu.CompilerParams(dimension_semantics=("arbitrary",),
                                             vmem_limit_bytes=VMEM_LIMIT_BYTES),
        name="proj_ffn",
    )(x, a, wo, g.reshape(1, d), wg, wu, wd, gf.reshape(1, d))


def _hgrn_consts(L):
    r = np.arange(L)[:, None]
    t = np.arange(L)[None, :]
    sels = [t <= r]
    masks = [r == t]
    h = L // 2
    while h >= 1:
        base = (r // (2 * h)) * (2 * h)
        upper = (r % (2 * h)) >= h
        sels.append(np.where(upper, (t >= base + h) & (t <= r), (t > r) & (t <= base + h - 1)))
        masks.append((r // (2 * h) == t // (2 * h)) & upper & ((t % (2 * h)) < h))
        h //= 2
    return np.concatenate(sels, axis=0).astype(np.float32), np.stack(masks).astype(np.float32)


def _decay_exponents(sel_ref, logf, L):
    blk = min(SEL_BLOCK, L)
    nb = L // blk
    n_loc = int(math.log2(blk))
    res = [_exact_dot(sel_ref[...], logf[i * blk:(i + 1) * blk]) for i in range(nb)]
    cum = [r[0:blk] for r in res]
    pre = [jnp.zeros_like(cum[0][0:1])]
    for i in range(nb):
        pre.append(pre[i] + cum[i][blk - 1:blk])
    b = jnp.concatenate([cum[i] + pre[i] for i in range(nb)], axis=0) if nb > 1 else cum[0]
    levels = []
    nbh = nb // 2
    while nbh >= 1:
        parts = []
        for i in range(nb):
            ref = pre[(i // (2 * nbh)) * 2 * nbh + nbh]
            if (i % (2 * nbh)) >= nbh:
                parts.append(cum[i] + (pre[i] - ref))
            else:
                parts.append((ref - pre[i]) - cum[i])
        levels.append(jnp.concatenate(parts, axis=0))
        nbh //= 2
    for lev in range(n_loc):
        parts = [r[(lev + 1) * blk:(lev + 2) * blk] for r in res]
        levels.append(jnp.concatenate(parts, axis=0) if nb > 1 else parts[0])
    return b, pre[nb], levels


def _retention_consts(L):
    lg = np.log1p(-np.exp2(-5.0 - np.arange(N_HEADS, dtype=np.float32))).astype(np.float32)
    idx = np.arange(L, dtype=np.float32)
    rel = idx[:, None] - idx[None, :]
    dmat = np.where(rel >= 0, np.exp(lg[:, None, None] * np.maximum(rel, 0.0)), 0.0)
    w_in = np.exp(lg[:, None] * (idx + 1.0))
    w_tail = np.exp(lg[:, None] * (L - 1.0 - idx))
    bcast = lambda v: jnp.asarray(np.broadcast_to(v[:, :, None], (N_HEADS, L, HEAD_DK)).astype(np.float32))
    g_chunk = [float(v) for v in np.exp(lg * L).astype(np.float32)]
    return jnp.asarray(dmat.astype(np.float32)), bcast(w_in), bcast(w_tail), g_chunk


def _rope_tables(pos0, T):
    half = HEAD_DK // 2
    inv = ROPE_BASE ** (-jnp.arange(half, dtype=F32) / half)
    ang = (jnp.arange(T) + pos0).astype(F32)[:, None] * inv[None, :]
    cos, sin = jnp.cos(ang), jnp.sin(ang)
    return jnp.concatenate([cos, cos], axis=-1), jnp.concatenate([-sin, sin], axis=-1)


def _hgrn_intra(q, k, v, expo, mask_ref, L):
    sub = min(L, HGRN_SUB)
    n_sub = L // sub
    n_lev = mask_ref.shape[0] - 1
    top = len(expo) - n_lev
    assert n_sub in (1, 2) and top == n_sub - 1
    atts = []
    for i in range(n_sub):
        r = slice(i * sub, (i + 1) * sub)
        att = mask_ref[0] * _dot_nt(q[r], k[r])
        for lev in range(n_lev):
            sc = jnp.exp(expo[top + lev][r])
            att = att + mask_ref[lev + 1] * _dot_nt(q[r] * sc, k[r] * sc)
        atts.append(att)
    if n_sub == 1:
        return _dot(atts[0], v)
    cross = _dot_nt(q[sub:] * jnp.exp(expo[0][sub:]), k[:sub] * jnp.exp(expo[0][:sub]))
    return jnp.concatenate([_dot(atts[0], v[:sub]),
                            _dot(jnp.concatenate([cross, atts[1]], axis=1), v)], axis=0)


def _mixer0_body(*refs, L, layer_slot, g_chunk, fused):
    n_src = 3 if fused else 1
    src = refs[:n_src]
    (cos_ref, sin_ref, lbl_ref, ga_ref, gb_ref, sel_ref, mask_ref, dmat_ref, win_ref, wtail_ref,
     sa0_ref, sb0_ref, mix_ref, sa_ref, sb_ref, sat_scr, sbt_scr) = refs[n_src:]
    c = pl.program_id(1)
    last = pl.num_programs(1) - 1
    head_w = IN0_WIDTH // N_HEADS

    @pl.when(c == 0)
    def _():
        for hh in range(N_HEADS):
            sat_scr[hh] = sa0_ref[0, hh].T
            sbt_scr[hh] = sb0_ref[0, hh].T

    lbl = lbl_ref[...]
    e = jnp.exp(lbl - jnp.max(lbl, axis=0, keepdims=True))
    lb_all = jnp.sum(e[:layer_slot + 1], axis=0, keepdims=True) / jnp.sum(e, axis=0, keepdims=True)

    if fused:
        x_ref, g_ref, w_ref = src
        hb = _rms(x_ref[...], g_ref[...]).astype(BF16)
    cosf = cos_ref[...]
    sinf = sin_ref[...]

    def project(hh):
        hcols = slice(hh * head_w, (hh + 1) * head_w)
        return jnp.dot(hb, w_ref[:, hcols], preferred_element_type=F32) if fused else src[0][:, hcols]

    z_next = project(0)
    for hh in range(N_HEADS):
        hs = slice(hh * HEAD_DK, (hh + 1) * HEAD_DK)
        zh = z_next
        if hh + 1 < N_HEADS:
            z_next = project(hh + 1)
        part = lambda j: zh[:, j * HEAD_DK:(j + 1) * HEAD_DK]

        lb = lb_all[:, hs]
        f = lb + (1.0 - lb) * _sigmoid(part(1))
        b, b_last, expo = _decay_exponents(sel_ref, jnp.log(f), L)
        q = part(0)
        k = 1.0 - f
        v = part(2)
        st = sat_scr[hh]
        o = _dot_nt(q * jnp.exp(b), st) + _hgrn_intra(q, k, v, expo, mask_ref, L)
        st_new = st * jnp.exp(b_last) + _dot_tn(v, k * jnp.exp(b_last - b))
        sat_scr[hh] = st_new
        o = _rms(o * _sigmoid(part(3)), ga_ref[...])
        mix_ref[:, hs] = o.astype(mix_ref.dtype)

        rq = part(4)
        rk = part(5)
        rv = part(6)
        rq = rq * cosf + pltpu.roll(rq, HEAD_DK // 2, 1) * sinf
        rk = (rk * cosf + pltpu.roll(rk, HEAD_DK // 2, 1) * sinf) * (HEAD_DK ** -0.5)
        rt = sbt_scr[hh]
        ro = _dot(_dot_nt(rq, rk) * dmat_ref[hh], rv) + win_ref[hh] * _dot_nt(rq, rt)
        rt_new = g_chunk[hh] * rt + _dot_tn(rv, rk * wtail_ref[hh])
        sbt_scr[hh] = rt_new
        ro = _rms(ro, gb_ref[...]) * _silu(part(7))
        mix_ref[:, KW + hh * HEAD_DK:KW + (hh + 1) * HEAD_DK] = ro.astype(mix_ref.dtype)

    @pl.when(c == last)
    def _():
        for hh in range(N_HEADS):
            sa_ref[0, hh] = sat_scr[hh].T
            sb_ref[0, hh] = sbt_scr[hh].T


def _head_major(w):
    d = w.shape[0]
    return w.reshape(d, 8, N_HEADS, HEAD_DK).transpose(0, 2, 1, 3).reshape(d, IN0_WIDTH)


def _mixer0(src, n_streams, T, pos0, sa0, sb0, lb_logits, ga, gb, layer_slot):
    L = min(CHUNK0, T)
    nc = T // L
    fused = len(src) == 3
    sel = jnp.asarray(_hgrn_consts(min(SEL_BLOCK, L))[0], BF16)
    masks = jnp.asarray(_hgrn_consts(min(HGRN_SUB, L))[1])
    dmat, w_in, w_tail, g_chunk = _retention_consts(L)
    cosf, sinf = _rope_tables(pos0, T)
    full = lambda a: pl.BlockSpec(a.shape, lambda s, c: (0,) * a.ndim, pipeline_mode=pl.Buffered(1))
    rows = lambda a: pl.BlockSpec((L, a.shape[1]), lambda s, c: (s * nc + c, 0))
    state_spec = pl.BlockSpec((1, N_HEADS, HEAD_DK, HEAD_DK), lambda s, c: (s, 0, 0, 0))
    state_shape = jax.ShapeDtypeStruct((n_streams, N_HEADS, HEAD_DK, HEAD_DK), F32)
    ga2, gb2 = ga.reshape(1, HEAD_DK), gb.reshape(1, HEAD_DK)
    if fused:
        src = (src[0], src[1].reshape(1, -1), src[2])
        src_specs = [rows(src[0]), full(src[1]), full(src[2])]
    else:
        src_specs = [rows(src[0])]
    consts = (lb_logits, ga2, gb2, sel, masks, dmat, w_in, w_tail)
    return pl.pallas_call(
        functools.partial(_mixer0_body, L=L, layer_slot=layer_slot, g_chunk=g_chunk, fused=fused),
        grid=(n_streams, nc),
        in_specs=src_specs
                 + [pl.BlockSpec((L, HEAD_DK), lambda s, c: (c, 0)), pl.BlockSpec((L, HEAD_DK), lambda s, c: (c, 0))]
                 + [full(a) for a in consts] + [state_spec, state_spec],
        out_specs=[pl.BlockSpec((L, 2 * KW), lambda s, c: (s * nc + c, 0)), state_spec, state_spec],
        out_shape=[jax.ShapeDtypeStruct((n_streams * T, 2 * KW), BF16), state_shape, state_shape],
        scratch_shapes=[pltpu.VMEM((N_HEADS, HEAD_DK, HEAD_DK), F32),
                        pltpu.VMEM((N_HEADS, HEAD_DK, HEAD_DK), F32)],
        compiler_params=pltpu.CompilerParams(dimension_semantics=("arbitrary", "arbitrary"),
                                             vmem_limit_bytes=VMEM_LIMIT_BYTES),
        name="mixer0_T%d" % T,
    )(*src, cosf, sinf, *consts, sa0, sb0)


def _mixer1_body(*refs, L, fused):
    n_src = 3 if fused else 1
    src = refs[:n_src]
    (cw_ref, cb_ref, wq_ref, wk_ref, wv_ref, wgate_ref, bgate_ref, tri_ref, gn_ref, skip_ref,
     c0_ref, n0_ref, m0_ref, conv0_ref, hg_ref, c_ref, n_ref, m_ref, conv_ref, xbuf) = refs[n_src:]
    c = pl.program_id(1)
    last = pl.num_programs(1) - 1
    n_taps = C_CONV - 1

    @pl.when(c == 0)
    def _():
        c_ref[...] = c0_ref[...]
        n_ref[...] = n0_ref[...]
        m_ref[...] = m0_ref[...]
        xbuf[CONV_HDR - n_taps:CONV_HDR, :] = conv0_ref[0]

    if fused:
        x_ref, g_ref, w_ref = src
        hb = _rms(x_ref[...], g_ref[...]).astype(BF16)
        xm = jnp.dot(hb, w_ref[:, :C_INNER], preferred_element_type=F32)
        zg = jnp.dot(hb, w_ref[:, C_INNER:], preferred_element_type=F32)
    else:
        xm = src[0][:, :C_INNER]
        zg = src[0][:, C_INNER:]

    xbuf[CONV_HDR:CONV_HDR + L, :] = xm
    xc = cb_ref[...] + xm * cw_ref[n_taps:n_taps + 1, :]
    for w in range(n_taps):
        xc = xc + xbuf[CONV_HDR - n_taps + w:CONV_HDR - n_taps + w + L, :] * cw_ref[w:w + 1, :]
    tail = xbuf[CONV_HDR + L - n_taps:CONV_HDR + L, :]
    xbuf[CONV_HDR - n_taps:CONV_HDR, :] = tail
    xc = _silu(xc)

    xcb = xc.astype(BF16)
    xmb = xm.astype(BF16)
    qs, ks, vs = [], [], []
    for g in range(C_INNER // BD_GROUP):
        gs = slice(g * BD_GROUP, (g + 1) * BD_GROUP)
        qs.append(jnp.dot(xcb[:, gs], wq_ref[g], preferred_element_type=F32))
        ks.append(jnp.dot(xcb[:, gs], wk_ref[g], preferred_element_type=F32))
        vs.append(jnp.dot(xmb[:, gs], wv_ref[g], preferred_element_type=F32))
    q = jnp.concatenate(qs, axis=1)
    k = jnp.concatenate(ks, axis=1) * (C_DH ** -0.5)
    v = jnp.concatenate(vs, axis=1)

    gates = (_dot(q, wgate_ref[0:C_INNER, :]) + _dot(k, wgate_ref[C_INNER:2 * C_INNER, :])
             + _dot(v, wgate_ref[2 * C_INNER:3 * C_INNER, :]) + bgate_ref[...])
    bcum = _exact_dot(tri_ref[...], _log_sigmoid(gates))
    lane = lax.broadcasted_iota(jnp.int32, gates.shape, 1)
    rows = jnp.where(lane < N_HEADS, gates, bcum).T
    ti = lax.broadcasted_iota(jnp.int32, (L, L), 0)
    si = lax.broadcasted_iota(jnp.int32, (L, L), 1)
    causal = si <= ti

    for hh in range(N_HEADS):
        hs = slice(hh * C_DH, (hh + 1) * C_DH)
        b_col = bcum[:, N_HEADS + hh:N_HEADS + hh + 1]
        i_col = gates[:, hh:hh + 1]
        b_row = rows[N_HEADS + hh:N_HEADS + hh + 1, :]
        i_row = rows[hh:hh + 1, :]
        m_prev = m_ref[0, :, hh:hh + 1]
        cm = c_ref[0, hh]
        nv = n_ref[0, hh:hh + 1, :]
        qh, kh, vh = q[:, hs], k[:, hs], v[:, hs]

        lw = jnp.where(causal, b_col - b_row + i_row, -jnp.inf)
        lp = b_col + m_prev
        mj = jnp.maximum(lp, jnp.max(lw, axis=-1, keepdims=True))
        wgt = jnp.exp(lw - mj)
        wp = jnp.exp(lp - mj)
        s = _dot_nt(qh, kh) * wgt
        num = _dot(s, vh) + wp * _dot_nt(qh, cm)
        den = jnp.sum(s, axis=-1, keepdims=True) + wp * jnp.sum(qh * nv, axis=-1, keepdims=True)
        hcell = num / jnp.maximum(jnp.abs(den), jnp.exp(-mj))

        m_new = mj[L - 1:L, :]
        b_last = b_col[L - 1:L, :]
        ws = jnp.exp(b_last - b_col + i_col - m_new)
        wpl = jnp.exp(b_last + m_prev - m_new)
        c_ref[0, hh] = wpl * cm + _dot_tn(vh * ws, kh)
        n_ref[0, hh:hh + 1, :] = wpl * nv + jnp.sum(kh * ws, axis=0, keepdims=True)
        m_ref[0, :, hh:hh + 1] = m_new

        hc = hcell - jnp.mean(hcell, axis=-1, keepdims=True)
        hn = hc * lax.rsqrt(jnp.mean(hc * hc, axis=-1, keepdims=True) + EPS) * gn_ref[...]
        out = (hn + skip_ref[:, hs] * xc[:, hs]) * _silu(zg[:, hs])
        hg_ref[:, hs] = out.astype(hg_ref.dtype)

    @pl.when(c == last)
    def _():
        conv_ref[0] = tail


def _mixer1(src, n_streams, T, c0, n0, m0, conv0, cw, cb, wq, wk, wv, wgate, bgate, gn, skip):
    L = min(CHUNK1, T)
    nc = T // L
    fused = len(src) == 3
    tri = jnp.asarray(np.tril(np.ones((L, L), np.float32)), BF16)
    full = lambda a: pl.BlockSpec(a.shape, lambda s, c: (0,) * a.ndim, pipeline_mode=pl.Buffered(1))
    rows = lambda a: pl.BlockSpec((L, a.shape[1]), lambda s, c: (s * nc + c, 0))
    per_stream = lambda a: pl.BlockSpec((1,) + a.shape[1:], lambda s, c: (s,) + (0,) * (a.ndim - 1))
    gn2, skip2, cb2 = gn.reshape(1, C_DH), skip.reshape(1, C_INNER), cb.reshape(1, C_INNER)
    m0 = m0.reshape(n_streams, 1, N_HEADS)
    if fused:
        src = (src[0], src[1].reshape(1, -1), src[2])
        src_specs = [rows(src[0]), full(src[1]), full(src[2])]
    else:
        src_specs = [rows(src[0])]
    consts = (cw, cb2, wq, wk, wv, wgate, bgate, tri, gn2, skip2)
    states = (c0, n0, m0, conv0)
    hg, c1, n1, m1, conv1 = pl.pallas_call(
        functools.partial(_mixer1_body, L=L, fused=fused),
        grid=(n_streams, nc),
        in_specs=src_specs + [full(a) for a in consts] + [per_stream(a) for a in states],
        out_specs=[pl.BlockSpec((L, C_INNER), lambda s, c: (s * nc + c, 0))]
                  + [per_stream(a) for a in states],
        out_shape=[jax.ShapeDtypeStruct((n_streams * T, C_INNER), BF16)]
                  + [jax.ShapeDtypeStruct(a.shape, F32) for a in states],
        scratch_shapes=[pltpu.VMEM((CONV_HDR + L, C_INNER), F32)],
        compiler_params=pltpu.CompilerParams(dimension_semantics=("arbitrary", "arbitrary"),
                                             vmem_limit_bytes=VMEM_LIMIT_BYTES),
        name="mixer1_T%d" % T,
    )(*src, *consts, *states)
    return hg, c1, n1, m1.reshape(n_streams, N_HEADS), conv1


def _dense_blockdiag(w):
    rows = w.reshape(-1, BD_GROUP, C_BLOCK)
    tiled = jnp.tile(rows, (1, 1, BD_GROUP // C_BLOCK))
    idx = np.arange(BD_GROUP) // C_BLOCK
    same_block = jnp.asarray(idx[:, None] == idx[None, :])
    return jnp.where(same_block, tiled, 0.0).astype(BF16)


def kernel(x_prompt, x_sample, state_hgrn, state_ret, state_mlstm_c, state_mlstm_n, state_mlstm_m, state_conv,
           norm_mix, norm_ffn, norm_final, w_in0, lb_logits, hgrn_norm, ret_norm, w_out0,
           w_up1, conv_w, conv_b, w_q1, w_k1, w_v1, w_ig, b_ig, w_fg, b_fg, mlstm_norm, mlstm_skip, w_down1,
           w_ffn_gate, w_ffn_up, w_ffn_down):
    bp, tp, d = x_prompt.shape
    bs, ts, _ = x_sample.shape
    mp, ms = bp * tp, bs * ts
    past_len = 2048
    bf = lambda w: w.astype(BF16)
    zeros = lambda *shape: jnp.zeros(shape, F32)

    xp = x_prompt.reshape(mp, d)
    xs = x_sample.reshape(ms, d)

    w_in = bf(_head_major(w_in0[0]))
    ffn0 = (bf(w_out0[0]), norm_ffn[0], bf(w_ffn_gate[0]), bf(w_ffn_up[0]), bf(w_ffn_down[0]), norm_final, False)
    mix_p, hg_p, rt_p = _mixer0((xp, norm_mix[0], w_in), bp, tp, 0, zeros(bp, N_HEADS, HEAD_DK, HEAD_DK),
                                zeros(bp, N_HEADS, HEAD_DK, HEAD_DK), lb_logits, hgrn_norm[0], ret_norm[0], 0)
    mix_s, hg_s, rt_s = _mixer0((_norm_matmul(xs, norm_mix[0], w_in),), bs, ts, past_len, state_hgrn[0],
                                state_ret[0], lb_logits, hgrn_norm[0], ret_norm[0], 0)
    xp = _proj_ffn(xp, mix_p, *ffn0)
    xs = _proj_ffn(xs, mix_s, *ffn0)

    w_up = bf(w_up1[0])
    ffn1 = (bf(w_down1[0]), norm_ffn[1], bf(w_ffn_gate[1]), bf(w_ffn_up[1]), bf(w_ffn_down[1]), norm_final, True)
    wgate = jnp.pad(jnp.concatenate([w_ig[0], w_fg[0]], axis=1), ((0, 0), (0, GATE_LANES - 2 * N_HEADS))).astype(BF16)
    bgate = jnp.pad(jnp.concatenate([b_ig[0], b_fg[0]]), (0, GATE_LANES - 2 * N_HEADS)).reshape(1, GATE_LANES)
    m1_consts = (conv_w[0], conv_b[0], _dense_blockdiag(w_q1[0]), _dense_blockdiag(w_k1[0]),
                 _dense_blockdiag(w_v1[0]), wgate, bgate, mlstm_norm[0], mlstm_skip[0])
    hg1_p, mc_p, mn_p, mm_p, cv_p = _mixer1((xp, norm_mix[1], w_up), bp, tp, zeros(bp, N_HEADS, C_DH, C_DH),
                                            zeros(bp, N_HEADS, C_DH), zeros(bp, N_HEADS),
                                            zeros(bp, C_CONV - 1, C_INNER), *m1_consts)
    hg1_s, mc_s, mn_s, mm_s, cv_s = _mixer1((_norm_matmul(xs, norm_mix[1], w_up),), bs, ts, state_mlstm_c[0],
                                            state_mlstm_n[0], state_mlstm_m[0], state_conv[0], *m1_consts)
    yp = _proj_ffn(xp, hg1_p, *ffn1)
    ys = _proj_ffn(xs, hg1_s, *ffn1)

    lead = lambda a: a[None]
    return (yp.reshape(bp, tp, d), ys.reshape(bs, ts, d),
            lead(hg_p), lead(hg_s), lead(rt_p), lead(rt_s),
            lead(mc_p), lead(mc_s), lead(mn_p), lead(mn_s), lead(mm_p), lead(mm_s), lead(cv_p), lead(cv_s))
```

```python
import functools
import math

import numpy as np
import jax
import jax.numpy as jnp
from jax import lax
from jax.experimental import pallas as pl
from jax.experimental.pallas import tpu as pltpu

F32 = jnp.float32
BF16 = jnp.bfloat16

D_MODEL = 1024
CHUNK0 = 256
CHUNK1 = 256
SEL_BLOCK = 64
HGRN_SUB = 128
EPS = 1e-6
N_HEADS = 4
HEAD_DK = 128
KW = N_HEADS * HEAD_DK
IN0_WIDTH = 8 * KW
ROPE_BASE = 10000.0
C_INNER = 2 * D_MODEL
C_DH = C_INNER // N_HEADS
C_CONV = 4
C_BLOCK = 4
BD_GROUP = 256
D_FF = -(-8 * D_MODEL // (3 * 256)) * 256
GATE_LANES = 128
CONV_HDR = 8
VMEM_LIMIT_BYTES = 56 * 1024 * 1024
ROW_TILE = 256
CAST_STEPS = 16


def _dot(a, b):
    return jnp.dot(a.astype(BF16), b.astype(BF16), preferred_element_type=F32)


def _dot_nt(a, b):
    return lax.dot_general(a.astype(BF16), b.astype(BF16), (((1,), (1,)), ((), ())),
                           preferred_element_type=F32)


def _dot_tn(a, b):
    return lax.dot_general(a.astype(BF16), b.astype(BF16), (((0,), (0,)), ((), ())),
                           preferred_element_type=F32)


def _exact_dot(sel, x):
    hi = x.astype(BF16)
    lo = (x - hi.astype(F32)).astype(BF16)
    return (jnp.dot(sel, hi, preferred_element_type=F32) + jnp.dot(sel, lo, preferred_element_type=F32))


def _rms(x, g):
    return x * lax.rsqrt(jnp.mean(x * x, axis=-1, keepdims=True) + EPS) * g


def _sigmoid(x):
    return 1.0 / (1.0 + jnp.exp(-x))


def _silu(x):
    return x * _sigmoid(x)


def _log_sigmoid(x):
    return jnp.minimum(x, 0.0) - jnp.log(1.0 + jnp.exp(-jnp.abs(x)))


def _norm_matmul_body(x_ref, g_ref, w_ref, o_ref):
    h = _rms(x_ref[...], g_ref[...])
    o_ref[...] = jnp.dot(h.astype(BF16), w_ref[...], preferred_element_type=F32)


def _norm_matmul(x, g, w):
    m, d = x.shape
    n = w.shape[1]
    return pl.pallas_call(
        _norm_matmul_body,
        grid=(m // ROW_TILE,),
        in_specs=[pl.BlockSpec((ROW_TILE, d), lambda i: (i, 0)),
                  pl.BlockSpec((1, d), lambda i: (0, 0)),
                  pl.BlockSpec((d, n), lambda i: (0, 0), pipeline_mode=pl.Buffered(1))],
        out_specs=pl.BlockSpec((ROW_TILE, n), lambda i: (i, 0)),
        out_shape=jax.ShapeDtypeStruct((m, n), F32),
        compiler_params=pltpu.CompilerParams(dimension_semantics=("arbitrary",),
                                             vmem_limit_bytes=VMEM_LIMIT_BYTES),
        name="norm_matmul",
    )(x, g.reshape(1, d), w)


def _cast_weights_body(*refs, n):
    ins, outs = refs[:n], refs[n:]
    n_slot = IN0_WIDTH // KW
    for hh in range(N_HEADS):
        for j in range(n_slot):
            dst = slice((hh * n_slot + j) * HEAD_DK, (hh * n_slot + j + 1) * HEAD_DK)
            src = slice((j * N_HEADS + hh) * HEAD_DK, (j * N_HEADS + hh + 1) * HEAD_DK)
            outs[0][:, dst] = ins[0][:, src].astype(BF16)
    for i_ref, o_ref in zip(ins[1:], outs[1:]):
        o_ref[...] = i_ref[...].astype(BF16)


def _cast_weights(ws):
    flat = [w.reshape(-1, w.shape[-1]) for w in ws]
    spec = lambda w: pl.BlockSpec((w.shape[0] // CAST_STEPS, w.shape[1]), lambda i: (i, 0))
    out = pl.pallas_call(
        functools.partial(_cast_weights_body, n=len(flat)),
        grid=(CAST_STEPS,),
        in_specs=[spec(w) for w in flat],
        out_specs=[spec(w) for w in flat],
        out_shape=[jax.ShapeDtypeStruct(w.shape, BF16) for w in flat],
        compiler_params=pltpu.CompilerParams(dimension_semantics=("arbitrary",),
                                             vmem_limit_bytes=VMEM_LIMIT_BYTES),
        name="cast_weights",
    )(*flat)
    return [o.reshape(w.shape) for o, w in zip(out, ws)]


def _proj_ffn_body(x_ref, a_ref, wo_ref, g_ref, wg_ref, wu_ref, wd_ref, gf_ref, o_ref, *, final_norm):
    x1 = x_ref[...] + jnp.dot(a_ref[...], wo_ref[...], preferred_element_type=F32)
    h = _rms(x1, g_ref[...]).astype(BF16)
    gate = jnp.dot(h, wg_ref[...], preferred_element_type=F32)
    up = jnp.dot(h, wu_ref[...], preferred_element_type=F32)
    t = (_silu(gate) * up).astype(BF16)
    x2 = x1 + jnp.dot(t, wd_ref[...], preferred_element_type=F32)
    if final_norm:
        x2 = _rms(x2, gf_ref[...])
    o_ref[...] = x2


def _proj_ffn(x, a, wo, g, wg, wu, wd, layer, gf, final_norm):
    m, d = x.shape
    ka = a.shape[1]
    ff = wg.shape[2]
    const = lambda shape: pl.BlockSpec(shape, lambda i: (0, 0), pipeline_mode=pl.Buffered(1))
    of_layer = lambda shape: pl.BlockSpec((None,) + shape, lambda i: (layer, 0, 0), pipeline_mode=pl.Buffered(1))
    return pl.pallas_call(
        functools.partial(_proj_ffn_body, final_norm=final_norm),
        grid=(m // ROW_TILE,),
        in_specs=[pl.BlockSpec((ROW_TILE, d), lambda i: (i, 0)),
                  pl.BlockSpec((ROW_TILE, ka), lambda i: (i, 0)),
                  const((ka, d)), const((1, d)), of_layer((d, ff)), of_layer((d, ff)), of_layer((ff, d)),
                  const((1, d))],
        out_specs=pl.BlockSpec((ROW_TILE, d), lambda i: (i, 0)),
        out_shape=jax.ShapeDtypeStruct((m, d), F32),
        compiler_params=pltpu.CompilerParams(dimension_semantics=("arbitrary",),
                                             vmem_limit_bytes=VMEM_LIMIT_BYTES),
        name="proj_ffn",
    )(x, a, wo, g.reshape(1, d), wg, wu, wd, gf.reshape(1, d))


def _hgrn_consts(L):
    r = np.arange(L)[:, None]
    t = np.arange(L)[None, :]
    sels = [t <= r]
    masks = [r == t]
    h = L // 2
    while h >= 1:
        base = (r // (2 * h)) * (2 * h)
        upper = (r % (2 * h)) >= h
        sels.append(np.where(upper, (t >= base + h) & (t <= r), (t > r) & (t <= base + h - 1)))
        masks.append((r // (2 * h) == t // (2 * h)) & upper & ((t % (2 * h)) < h))
        h //= 2
    return np.concatenate(sels, axis=0).astype(np.float32), np.stack(masks).astype(np.float32)


def _decay_exponents(sel_ref, logf, L):
    blk = min(SEL_BLOCK, L)
    nb = L // blk
    n_loc = int(math.log2(blk))
    res = [_exact_dot(sel_ref[...], logf[i * blk:(i + 1) * blk]) for i in range(nb)]
    cum = [r[0:blk] for r in res]
    pre = [jnp.zeros_like(cum[0][0:1])]
    for i in range(nb):
        pre.append(pre[i] + cum[i][blk - 1:blk])
    b = jnp.concatenate([cum[i] + pre[i] for i in range(nb)], axis=0) if nb > 1 else cum[0]
    levels = []
    nbh = nb // 2
    while nbh >= 1:
        parts = []
        for i in range(nb):
            ref = pre[(i // (2 * nbh)) * 2 * nbh + nbh]
            if (i % (2 * nbh)) >= nbh:
                parts.append(cum[i] + (pre[i] - ref))
            else:
                parts.append((ref - pre[i]) - cum[i])
        levels.append(jnp.concatenate(parts, axis=0))
        nbh //= 2
    for lev in range(n_loc):
        parts = [r[(lev + 1) * blk:(lev + 2) * blk] for r in res]
        levels.append(jnp.concatenate(parts, axis=0) if nb > 1 else parts[0])
    return b, pre[nb], levels


def _retention_consts(L):
    lg = np.log1p(-np.exp2(-5.0 - np.arange(N_HEADS, dtype=np.float32))).astype(np.float32)
    idx = np.arange(L, dtype=np.float32)
    rel = idx[:, None] - idx[None, :]
    dmat = np.where(rel >= 0, np.exp(lg[:, None, None] * np.maximum(rel, 0.0)), 0.0)
    w_in = np.exp(lg[:, None] * (idx + 1.0))
    w_tail = np.exp(lg[:, None] * (L - 1.0 - idx))
    bcast = lambda v: jnp.asarray(np.broadcast_to(v[:, :, None], (N_HEADS, L, HEAD_DK)).astype(np.float32))
    g_chunk = [float(v) for v in np.exp(lg * L).astype(np.float32)]
    return jnp.asarray(dmat.astype(np.float32)), bcast(w_in), bcast(w_tail), g_chunk


def _rope_tables(pos0, T):
    half = HEAD_DK // 2
    inv = ROPE_BASE ** (-jnp.arange(half, dtype=F32) / half)
    ang = (jnp.arange(T) + pos0).astype(F32)[:, None] * inv[None, :]
    cos, sin = jnp.cos(ang), jnp.sin(ang)
    return jnp.concatenate([cos, cos], axis=-1), jnp.concatenate([-sin, sin], axis=-1)


def _hgrn_intra(q, k, v, expo, mask_ref, L):
    sub = min(L, HGRN_SUB)
    n_sub = L // sub
    n_lev = mask_ref.shape[0] - 1
    top = len(expo) - n_lev
    assert n_sub in (1, 2) and top == n_sub - 1
    atts = []
    for i in range(n_sub):
        r = slice(i * sub, (i + 1) * sub)
        att = mask_ref[0] * _dot_nt(q[r], k[r])
        for lev in range(n_lev):
            sc = jnp.exp(expo[top + lev][r])
            att = att + mask_ref[lev + 1] * _dot_nt(q[r] * sc, k[r] * sc)
        atts.append(att)
    if n_sub == 1:
        return _dot(atts[0], v)
    cross = _dot_nt(q[sub:] * jnp.exp(expo[0][sub:]), k[:sub] * jnp.exp(expo[0][:sub]))
    return jnp.concatenate([_dot(atts[0], v[:sub]),
                            _dot(jnp.concatenate([cross, atts[1]], axis=1), v)], axis=0)


def _mixer0_body(*refs, L, layer_slot, g_chunk, fused):
    n_src = 3 if fused else 1
    src = refs[:n_src]
    (cos_ref, sin_ref, lbl_ref, ga_ref, gb_ref, sel_ref, mask_ref, dmat_ref, win_ref, wtail_ref,
     sa0_ref, sb0_ref, mix_ref, sa_ref, sb_ref, sat_scr, sbt_scr) = refs[n_src:]
    c = pl.program_id(1)
    last = pl.num_programs(1) - 1
    head_w = IN0_WIDTH // N_HEADS

    @pl.when(c == 0)
    def _():
        for hh in range(N_HEADS):
            sat_scr[hh] = sa0_ref[0, hh].T
            sbt_scr[hh] = sb0_ref[0, hh].T

    lbl = lbl_ref[...]
    e = jnp.exp(lbl - jnp.max(lbl, axis=0, keepdims=True))
    lb_all = jnp.sum(e[:layer_slot + 1], axis=0, keepdims=True) / jnp.sum(e, axis=0, keepdims=True)

    if fused:
        x_ref, g_ref, w_ref = src
        hb = _rms(x_ref[...], g_ref[...]).astype(BF16)
    cosf = cos_ref[...]
    sinf = sin_ref[...]

    def project(hh):
        hcols = slice(hh * head_w, (hh + 1) * head_w)
        return jnp.dot(hb, w_ref[:, hcols], preferred_element_type=F32) if fused else src[0][:, hcols]

    z_next = project(0)
    for hh in range(N_HEADS):
        hs = slice(hh * HEAD_DK, (hh + 1) * HEAD_DK)
        zh = z_next
        if hh + 1 < N_HEADS:
            z_next = project(hh + 1)
        part = lambda j: zh[:, j * HEAD_DK:(j + 1) * HEAD_DK]

        lb = lb_all[:, hs]
        f = lb + (1.0 - lb) * _sigmoid(part(1))
        b, b_last, expo = _decay_exponents(sel_ref, jnp.log(f), L)
        q = part(0)
        k = 1.0 - f
        v = part(2)
        st = sat_scr[hh]
        o = _dot_nt(q * jnp.exp(b), st) + _hgrn_intra(q, k, v, expo, mask_ref, L)
        st_new = st * jnp.exp(b_last) + _dot_tn(v, k * jnp.exp(b_last - b))
        sat_scr[hh] = st_new
        o = _rms(o * _sigmoid(part(3)), ga_ref[...])
        mix_ref[:, hs] = o.astype(mix_ref.dtype)

        rq = part(4)
        rk = part(5)
        rv = part(6)
        rq = rq * cosf + pltpu.roll(rq, HEAD_DK // 2, 1) * sinf
        rk = (rk * cosf + pltpu.roll(rk, HEAD_DK // 2, 1) * sinf) * (HEAD_DK ** -0.5)
        rt = sbt_scr[hh]
        ro = _dot(_dot_nt(rq, rk) * dmat_ref[hh], rv) + win_ref[hh] * _dot_nt(rq, rt)
        rt_new = g_chunk[hh] * rt + _dot_tn(rv, rk * wtail_ref[hh])
        sbt_scr[hh] = rt_new
        ro = _rms(ro, gb_ref[...]) * _silu(part(7))
        mix_ref[:, KW + hh * HEAD_DK:KW + (hh + 1) * HEAD_DK] = ro.astype(mix_ref.dtype)

    @pl.when(c == last)
    def _():
        for hh in range(N_HEADS):
            sa_ref[0, hh] = sat_scr[hh].T
            sb_ref[0, hh] = sbt_scr[hh].T


def _mixer0(src, n_streams, T, pos0, sa0, sb0, lb_logits, ga, gb, layer_slot):
    L = min(CHUNK0, T)
    nc = T // L
    fused = len(src) == 3
    sel = jnp.asarray(_hgrn_consts(min(SEL_BLOCK, L))[0], BF16)
    masks = jnp.asarray(_hgrn_consts(min(HGRN_SUB, L))[1])
    dmat, w_in, w_tail, g_chunk = _retention_consts(L)
    cosf, sinf = _rope_tables(pos0, T)
    full = lambda a: pl.BlockSpec(a.shape, lambda s, c: (0,) * a.ndim, pipeline_mode=pl.Buffered(1))
    rows = lambda a: pl.BlockSpec((L, a.shape[1]), lambda s, c: (s * nc + c, 0))
    state_spec = pl.BlockSpec((1, N_HEADS, HEAD_DK, HEAD_DK), lambda s, c: (s, 0, 0, 0))
    state_shape = jax.ShapeDtypeStruct((n_streams, N_HEADS, HEAD_DK, HEAD_DK), F32)
    ga2, gb2 = ga.reshape(1, HEAD_DK), gb.reshape(1, HEAD_DK)
    if fused:
        src = (src[0], src[1].reshape(1, -1), src[2])
        src_specs = [rows(src[0]), full(src[1]), full(src[2])]
    else:
        src_specs = [rows(src[0])]
    consts = (lb_logits, ga2, gb2, sel, masks, dmat, w_in, w_tail)
    return pl.pallas_call(
        functools.partial(_mixer0_body, L=L, layer_slot=layer_slot, g_chunk=g_chunk, fused=fused),
        grid=(n_streams, nc),
        in_specs=src_specs
                 + [pl.BlockSpec((L, HEAD_DK), lambda s, c: (c, 0)), pl.BlockSpec((L, HEAD_DK), lambda s, c: (c, 0))]
                 + [full(a) for a in consts] + [state_spec, state_spec],
        out_specs=[pl.BlockSpec((L, 2 * KW), lambda s, c: (s * nc + c, 0)), state_spec, state_spec],
        out_shape=[jax.ShapeDtypeStruct((n_streams * T, 2 * KW), BF16), state_shape, state_shape],
        scratch_shapes=[pltpu.VMEM((N_HEADS, HEAD_DK, HEAD_DK), F32),
                        pltpu.VMEM((N_HEADS, HEAD_DK, HEAD_DK), F32)],
        compiler_params=pltpu.CompilerParams(dimension_semantics=("arbitrary", "arbitrary"),
                                             vmem_limit_bytes=VMEM_LIMIT_BYTES),
        name="mixer0_T%d" % T,
    )(*src, cosf, sinf, *consts, sa0, sb0)


def _mixer1_body(*refs, L, fused, zero_init):
    it = iter(refs)
    take = lambda n: [next(it) for _ in range(n)]
    src = take(3 if fused else 1)
    (cw_ref, cb_ref, wq_ref, wk_ref, wv_ref, wgate_ref, bgate_ref, tri_ref, gn_ref, skip_ref) = take(10)
    init = None if zero_init else take(4)
    hg_ref, c_ref, n_ref, m_ref, conv_ref = take(5)
    xbuf = take(1)[0]
    c = pl.program_id(1)
    last = pl.num_programs(1) - 1
    n_taps = C_CONV - 1

    @pl.when(c == 0)
    def _():
        if zero_init:
            c_ref[...] = jnp.zeros_like(c_ref)
            n_ref[...] = jnp.zeros_like(n_ref)
            m_ref[...] = jnp.zeros_like(m_ref)
            xbuf[0:CONV_HDR, :] = jnp.zeros((CONV_HDR, C_INNER), F32)
        else:
            c0_ref, n0_ref, m0_ref, conv0_ref = init
            c_ref[...] = c0_ref[...]
            n_ref[...] = n0_ref[...]
            m_ref[...] = m0_ref[...]
            xbuf[CONV_HDR - n_taps:CONV_HDR, :] = conv0_ref[0]

    if fused:
        x_ref, g_ref, w_ref = src
        hb = _rms(x_ref[...], g_ref[...]).astype(BF16)
        xm = jnp.dot(hb, w_ref[:, :C_INNER], preferred_element_type=F32)
        zg = jnp.dot(hb, w_ref[:, C_INNER:], preferred_element_type=F32)
    else:
        xm = src[0][:, :C_INNER]
        zg = src[0][:, C_INNER:]

    xbuf[CONV_HDR:CONV_HDR + L, :] = xm
    xc = cb_ref[...] + xm * cw_ref[n_taps:n_taps + 1, :]
    for w in range(n_taps):
        xc = xc + xbuf[CONV_HDR - n_taps + w:CONV_HDR - n_taps + w + L, :] * cw_ref[w:w + 1, :]
    tail = xbuf[CONV_HDR + L - n_taps:CONV_HDR + L, :]
    xbuf[CONV_HDR - n_taps:CONV_HDR, :] = tail
    xc = _silu(xc)

    xcb = xc.astype(BF16)
    xmb = xm.astype(BF16)
    qs, ks, vs = [], [], []
    for g in range(C_INNER // BD_GROUP):
        gs = slice(g * BD_GROUP, (g + 1) * BD_GROUP)
        qs.append(jnp.dot(xcb[:, gs], wq_ref[g], preferred_element_type=F32))
        ks.append(jnp.dot(xcb[:, gs], wk_ref[g], preferred_element_type=F32))
        vs.append(jnp.dot(xmb[:, gs], wv_ref[g], preferred_element_type=F32))
    q = jnp.concatenate(qs, axis=1)
    k = jnp.concatenate(ks, axis=1) * (C_DH ** -0.5)
    v = jnp.concatenate(vs, axis=1)

    gates = (_dot(q, wgate_ref[0:C_INNER, :]) + _dot(k, wgate_ref[C_INNER:2 * C_INNER, :])
             + _dot(v, wgate_ref[2 * C_INNER:3 * C_INNER, :]) + bgate_ref[...])
    bcum = _exact_dot(tri_ref[...], _log_sigmoid(gates))
    lane = lax.broadcasted_iota(jnp.int32, gates.shape, 1)
    rows = jnp.where(lane < N_HEADS, gates, bcum).T
    ti = lax.broadcasted_iota(jnp.int32, (L, L), 0)
    si = lax.broadcasted_iota(jnp.int32, (L, L), 1)
    causal = si <= ti

    for hh in range(N_HEADS):
        hs = slice(hh * C_DH, (hh + 1) * C_DH)
        b_col = bcum[:, N_HEADS + hh:N_HEADS + hh + 1]
        i_col = gates[:, hh:hh + 1]
        b_row = rows[N_HEADS + hh:N_HEADS + hh + 1, :]
        i_row = rows[hh:hh + 1, :]
        m_prev = m_ref[0, :, hh:hh + 1]
        cm = c_ref[0, hh]
        nv = n_ref[0, hh:hh + 1, :]
        qh, kh, vh = q[:, hs], k[:, hs], v[:, hs]

        lw = jnp.where(causal, b_col - b_row + i_row, -jnp.inf)
        lp = b_col + m_prev
        mj = jnp.maximum(lp, jnp.max(lw, axis=-1, keepdims=True))
        wgt = jnp.exp(lw - mj)
        wp = jnp.exp(lp - mj)
        s = _dot_nt(qh, kh) * wgt
        num = _dot(s, vh) + wp * _dot_nt(qh, cm)
        den = jnp.sum(s, axis=-1, keepdims=True) + wp * jnp.sum(qh * nv, axis=-1, keepdims=True)
        hcell = num / jnp.maximum(jnp.abs(den), jnp.exp(-mj))

        m_new = mj[L - 1:L, :]
        b_last = b_col[L - 1:L, :]
        ws = jnp.exp(b_last - b_col + i_col - m_new)
        wpl = jnp.exp(b_last + m_prev - m_new)
        c_ref[0, hh] = wpl * cm + _dot_tn(vh * ws, kh)
        n_ref[0, hh:hh + 1, :] = wpl * nv + jnp.sum(kh * ws, axis=0, keepdims=True)
        m_ref[0, :, hh:hh + 1] = m_new

        hc = hcell - jnp.mean(hcell, axis=-1, keepdims=True)
        hn = hc * lax.rsqrt(jnp.mean(hc * hc, axis=-1, keepdims=True) + EPS) * gn_ref[...]
        out = (hn + skip_ref[:, hs] * xc[:, hs]) * _silu(zg[:, hs])
        hg_ref[:, hs] = out.astype(hg_ref.dtype)

    @pl.when(c == last)
    def _():
        conv_ref[0] = tail


def _mixer1(src, n_streams, T, states, cw, cb, wq, wk, wv, wgate, bgate, gn, skip):
    L = min(CHUNK1, T)
    nc = T // L
    fused = len(src) == 3
    zero_init = states is None
    tri = jnp.asarray(np.tril(np.ones((L, L), np.float32)), BF16)
    full = lambda a: pl.BlockSpec(a.shape, lambda s, c: (0,) * a.ndim, pipeline_mode=pl.Buffered(1))
    rows = lambda a: pl.BlockSpec((L, a.shape[1]), lambda s, c: (s * nc + c, 0))
    per_stream = lambda shape: pl.BlockSpec((1,) + shape[1:], lambda s, c: (s,) + (0,) * (len(shape) - 1))
    gn2, skip2, cb2 = gn.reshape(1, C_DH), skip.reshape(1, C_INNER), cb.reshape(1, C_INNER)
    state_shapes = ((n_streams, N_HEADS, C_DH, C_DH), (n_streams, N_HEADS, C_DH), (n_streams, 1, N_HEADS),
                    (n_streams, C_CONV - 1, C_INNER))
    scratch = [pltpu.VMEM((CONV_HDR + L, C_INNER), F32)]
    if fused:
        src = (src[0], src[1].reshape(1, -1), src[2])
        src_specs = [rows(src[0]), full(src[1]), full(src[2])]
    else:
        src_specs = [rows(src[0])]
    consts = (cw, cb2, wq, wk, wv, wgate, bgate, tri, gn2, skip2)
    if not zero_init:
        states = tuple(a.reshape(shape) for a, shape in zip(states, state_shapes))
    hg, c1, n1, m1, conv1 = pl.pallas_call(
        functools.partial(_mixer1_body, L=L, fused=fused, zero_init=zero_init),
        grid=(n_streams, nc),
        in_specs=src_specs + [full(a) for a in consts]
                 + ([] if zero_init else [per_stream(shape) for shape in state_shapes]),
        out_specs=[pl.BlockSpec((L, C_INNER), lambda s, c: (s * nc + c, 0))]
                  + [per_stream(shape) for shape in state_shapes],
        out_shape=[jax.ShapeDtypeStruct((n_streams * T, C_INNER), BF16)]
                  + [jax.ShapeDtypeStruct(shape, F32) for shape in state_shapes],
        scratch_shapes=scratch,
        compiler_params=pltpu.CompilerParams(dimension_semantics=("arbitrary", "arbitrary"),
                                             vmem_limit_bytes=VMEM_LIMIT_BYTES),
        name="mixer1_T%d" % T,
    )(*src, *consts, *(() if zero_init else states))
    return hg, c1, n1, m1.reshape(n_streams, N_HEADS), conv1


def _dense_blockdiag(w):
    rows = w.reshape(-1, BD_GROUP, C_BLOCK)
    tiled = jnp.tile(rows, (1, 1, BD_GROUP // C_BLOCK))
    idx = np.arange(BD_GROUP) // C_BLOCK
    same_block = jnp.asarray(idx[:, None] == idx[None, :])
    return jnp.where(same_block, tiled, 0.0).astype(BF16)


def kernel(x_prompt, x_sample, state_hgrn, state_ret, state_mlstm_c, state_mlstm_n, state_mlstm_m, state_conv,
           norm_mix, norm_ffn, norm_final, w_in0, lb_logits, hgrn_norm, ret_norm, w_out0,
           w_up1, conv_w, conv_b, w_q1, w_k1, w_v1, w_ig, b_ig, w_fg, b_fg, mlstm_norm, mlstm_skip, w_down1,
           w_ffn_gate, w_ffn_up, w_ffn_down):
    bp, tp, d = x_prompt.shape
    bs, ts, _ = x_sample.shape
    mp, ms = bp * tp, bs * ts
    past_len = 2048
    zeros = lambda *shape: jnp.zeros(shape, F32)
    w_in, w_out, w_up, w_down, wf_gate, wf_up, wf_down = _cast_weights(
        [w_in0[0], w_out0[0], w_up1[0], w_down1[0], w_ffn_gate, w_ffn_up, w_ffn_down])

    xp = x_prompt.reshape(mp, d)
    xs = x_sample.reshape(ms, d)

    ffn0 = (w_out, norm_ffn[0], wf_gate, wf_up, wf_down, 0, norm_final, False)
    mix_p, hg_p, rt_p = _mixer0((xp, norm_mix[0], w_in), bp, tp, 0, zeros(bp, N_HEADS, HEAD_DK, HEAD_DK),
                                zeros(bp, N_HEADS, HEAD_DK, HEAD_DK), lb_logits, hgrn_norm[0], ret_norm[0], 0)
    mix_s, hg_s, rt_s = _mixer0((_norm_matmul(xs, norm_mix[0], w_in),), bs, ts, past_len, state_hgrn[0],
                                state_ret[0], lb_logits, hgrn_norm[0], ret_norm[0], 0)
    xp = _proj_ffn(xp, mix_p, *ffn0)
    xs = _proj_ffn(xs, mix_s, *ffn0)

    ffn1 = (w_down, norm_ffn[1], wf_gate, wf_up, wf_down, 1, norm_final, True)
    wgate = jnp.pad(jnp.concatenate([w_ig[0], w_fg[0]], axis=1), ((0, 0), (0, GATE_LANES - 2 * N_HEADS))).astype(BF16)
    bgate = jnp.pad(jnp.concatenate([b_ig[0], b_fg[0]]), (0, GATE_LANES - 2 * N_HEADS)).reshape(1, GATE_LANES)
    m1_consts = (conv_w[0], conv_b[0], _dense_blockdiag(w_q1[0]), _dense_blockdiag(w_k1[0]),
                 _dense_blockdiag(w_v1[0]), wgate, bgate, mlstm_norm[0], mlstm_skip[0])
    hg1_p, mc_p, mn_p, mm_p, cv_p = _mixer1((xp, norm_mix[1], w_up), bp, tp, None, *m1_consts)
    hg1_s, mc_s, mn_s, mm_s, cv_s = _mixer1((_norm_matmul(xs, norm_mix[1], w_up),), bs, ts,
                                            (state_mlstm_c[0], state_mlstm_n[0], state_mlstm_m[0], state_conv[0]),
                                            *m1_consts)
    yp = _proj_ffn(xp, hg1_p, *ffn1)
    ys = _proj_ffn(xs, hg1_s, *ffn1)

    lead = lambda a: a[None]
    return (yp.reshape(bp, tp, d), ys.reshape(bs, ts, d),
            lead(hg_p), lead(hg_s), lead(rt_p), lead(rt_s),
            lead(mc_p), lead(mc_s), lead(mn_p), lead(mn_s), lead(mm_p), lead(mm_s), lead(cv_p), lead(cv_s))
```

```python
import functools
import math

import numpy as np
import jax
import jax.numpy as jnp
from jax import lax
from jax.experimental import pallas as pl
from jax.experimental.pallas import tpu as pltpu

F32 = jnp.float32
BF16 = jnp.bfloat16

D_MODEL = 1024
CHUNK0 = 256
CHUNK1 = 256
STEP_ROWS0 = 128
SEL_BLOCK = 64
HGRN_SUB = 128
EPS = 1e-6
SUBLANES = 8
N_HEADS = 4
HEAD_DK = 128
KW = N_HEADS * HEAD_DK
IN0_WIDTH = 8 * KW
ROPE_BASE = 10000.0
C_INNER = 2 * D_MODEL
C_DH = C_INNER // N_HEADS
C_CONV = 4
C_BLOCK = 4
BD_GROUP = 256
D_FF = -(-8 * D_MODEL // (3 * 256)) * 256
GATE_LANES = 128
CONV_HDR = 8
VMEM_LIMIT_BYTES = 56 * 1024 * 1024
ROW_TILE = 512
CAST_STEPS = 16


def _dot(a, b):
    return jnp.dot(a.astype(BF16), b.astype(BF16), preferred_element_type=F32)


def _dot_nt(a, b):
    return lax.dot_general(a.astype(BF16), b.astype(BF16), (((1,), (1,)), ((), ())),
                           preferred_element_type=F32)


def _dot_tn(a, b):
    return lax.dot_general(a.astype(BF16), b.astype(BF16), (((0,), (0,)), ((), ())),
                           preferred_element_type=F32)


def _exact_dot(sel, x):
    hi = x.astype(BF16)
    lo = (x - hi.astype(F32)).astype(BF16)
    n = x.shape[1]
    both = jnp.dot(sel, jnp.concatenate([hi, lo], axis=1), preferred_element_type=F32)
    return both[:, :n] + both[:, n:]


def _rms(x, g):
    return x * lax.rsqrt(jnp.mean(x * x, axis=-1, keepdims=True) + EPS) * g


def _sigmoid(x):
    return 1.0 / (1.0 + jnp.exp(-x))


def _silu(x):
    return x * _sigmoid(x)


def _log_sigmoid(x):
    return jnp.minimum(x, 0.0) - jnp.log(1.0 + jnp.exp(-jnp.abs(x)))


def _norm_matmul_body(x_ref, g_ref, w_ref, o_ref):
    h = _rms(x_ref[...], g_ref[...])
    o_ref[...] = jnp.dot(h.astype(BF16), w_ref[...], preferred_element_type=F32)


def _norm_matmul(x, g, w):
    m, d = x.shape
    n = w.shape[1]
    return pl.pallas_call(
        _norm_matmul_body,
        grid=(m // ROW_TILE,),
        in_specs=[pl.BlockSpec((ROW_TILE, d), lambda i: (i, 0)),
                  pl.BlockSpec((1, d), lambda i: (0, 0)),
                  pl.BlockSpec((d, n), lambda i: (0, 0), pipeline_mode=pl.Buffered(1))],
        out_specs=pl.BlockSpec((ROW_TILE, n), lambda i: (i, 0)),
        out_shape=jax.ShapeDtypeStruct((m, n), F32),
        compiler_params=pltpu.CompilerParams(dimension_semantics=("arbitrary",),
                                             vmem_limit_bytes=VMEM_LIMIT_BYTES),
        name="norm_matmul",
    )(x, g.reshape(1, d), w)


def _cast_weights_body(*refs, n):
    ins, outs = refs[:n], refs[n:]
    n_slot = IN0_WIDTH // KW
    for hh in range(N_HEADS):
        for j in range(n_slot):
            dst = slice((hh * n_slot + j) * HEAD_DK, (hh * n_slot + j + 1) * HEAD_DK)
            src = slice((j * N_HEADS + hh) * HEAD_DK, (j * N_HEADS + hh + 1) * HEAD_DK)
            outs[0][:, dst] = ins[0][:, src].astype(BF16)
    for i_ref, o_ref in zip(ins[1:], outs[1:]):
        o_ref[...] = i_ref[...].astype(BF16)


def _cast_weights(ws):
    flat = [w.reshape(-1, w.shape[-1]) for w in ws]
    spec = lambda w: pl.BlockSpec((w.shape[0] // CAST_STEPS, w.shape[1]), lambda i: (i, 0))
    out = pl.pallas_call(
        functools.partial(_cast_weights_body, n=len(flat)),
        grid=(CAST_STEPS,),
        in_specs=[spec(w) for w in flat],
        out_specs=[spec(w) for w in flat],
        out_shape=[jax.ShapeDtypeStruct(w.shape, BF16) for w in flat],
        compiler_params=pltpu.CompilerParams(dimension_semantics=("arbitrary",),
                                             vmem_limit_bytes=VMEM_LIMIT_BYTES),
        name="cast_weights",
    )(*flat)
    return [o.reshape(w.shape) for o, w in zip(out, ws)]


def _proj_ffn_body(x_ref, a_ref, wo_ref, g_ref, wg_ref, wu_ref, wd_ref, gf_ref, o_ref, *, final_norm):
    x1 = x_ref[...] + jnp.dot(a_ref[...], wo_ref[...], preferred_element_type=F32)
    h = _rms(x1, g_ref[...]).astype(BF16)
    gate = jnp.dot(h, wg_ref[...], preferred_element_type=F32)
    up = jnp.dot(h, wu_ref[...], preferred_element_type=F32)
    t = (_silu(gate) * up).astype(BF16)
    x2 = x1 + jnp.dot(t, wd_ref[...], preferred_element_type=F32)
    if final_norm:
        x2 = _rms(x2, gf_ref[...])
    o_ref[...] = x2


def _proj_ffn(x, a, wo, g, wg, wu, wd, layer, gf, final_norm):
    m, d = x.shape
    ka = a.shape[1]
    ff = wg.shape[2]
    const = lambda shape: pl.BlockSpec(shape, lambda i: (0, 0), pipeline_mode=pl.Buffered(1))
    of_layer = lambda shape: pl.BlockSpec((None,) + shape, lambda i: (layer, 0, 0), pipeline_mode=pl.Buffered(1))
    return pl.pallas_call(
        functools.partial(_proj_ffn_body, final_norm=final_norm),
        grid=(m // ROW_TILE,),
        in_specs=[pl.BlockSpec((ROW_TILE, d), lambda i: (i, 0)),
                  pl.BlockSpec((ROW_TILE, ka), lambda i: (i, 0)),
                  const((ka, d)), const((1, d)), of_layer((d, ff)), of_layer((d, ff)), of_layer((ff, d)),
                  const((1, d))],
        out_specs=pl.BlockSpec((ROW_TILE, d), lambda i: (i, 0)),
        out_shape=jax.ShapeDtypeStruct((m, d), F32),
        compiler_params=pltpu.CompilerParams(dimension_semantics=("arbitrary",),
                                             vmem_limit_bytes=VMEM_LIMIT_BYTES),
        name="proj_ffn",
    )(x, a, wo, g.reshape(1, d), wg, wu, wd, gf.reshape(1, d))


def _hgrn_consts(L):
    r = np.arange(L)[:, None]
    t = np.arange(L)[None, :]
    sels = [t <= r]
    masks = [r == t]
    h = L // 2
    while h >= 1:
        base = (r // (2 * h)) * (2 * h)
        upper = (r % (2 * h)) >= h
        if h < SUBLANES:
            sels.append(np.where(upper, (t >= base + h) & (t <= r), (t > r) & (t <= base + h - 1)))
        masks.append((r // (2 * h) == t // (2 * h)) & upper & ((t % (2 * h)) < h))
        h //= 2
    return np.concatenate(sels, axis=0).astype(np.float32), np.stack(masks).astype(np.float32)


def _decay_exponents(sel_ref, logf, L):
    blk = min(SEL_BLOCK, L)
    nb = L // blk
    res = [_exact_dot(sel_ref[...], logf[i * blk:(i + 1) * blk]) for i in range(nb)]
    cum = [r[0:blk] for r in res]
    pre = [jnp.zeros_like(cum[0][0:1])]
    for i in range(nb):
        pre.append(pre[i] + cum[i][blk - 1:blk])
    cat = lambda parts: jnp.concatenate(parts, axis=0) if len(parts) > 1 else parts[0]
    b = cat([cum[i] + pre[i] for i in range(nb)])
    levels = []
    nbh = nb // 2
    while nbh >= 1:
        parts = []
        for i in range(nb):
            ref = pre[(i // (2 * nbh)) * 2 * nbh + nbh]
            if (i % (2 * nbh)) >= nbh:
                parts.append(cum[i] + (pre[i] - ref))
            else:
                parts.append((ref - pre[i]) - cum[i])
        levels.append(cat(parts))
        nbh //= 2
    h = blk // 2
    while h >= SUBLANES:
        parts = []
        for i in range(nb):
            for base in range(0, blk, 2 * h):
                ref = cum[i][base + h - 1:base + h]
                parts += [ref - cum[i][base:base + h], cum[i][base + h:base + 2 * h] - ref]
        levels.append(cat(parts))
        h //= 2
    for lev in range(sel_ref.shape[0] // blk - 1):
        levels.append(cat([r[(lev + 1) * blk:(lev + 2) * blk] for r in res]))
    return b, pre[nb], levels


def _retention_consts(L):
    lg = np.log1p(-np.exp2(-5.0 - np.arange(N_HEADS, dtype=np.float32))).astype(np.float32)
    idx = np.arange(L, dtype=np.float32)
    rel = idx[:, None] - idx[None, :]
    dmat = np.where(rel >= 0, np.exp(lg[:, None, None] * np.maximum(rel, 0.0)), 0.0)
    w_in = np.exp(lg[:, None] * (idx + 1.0))
    w_tail = np.exp(lg[:, None] * (L - 1.0 - idx))
    bcast = lambda v: jnp.asarray(np.broadcast_to(v[:, :, None], (N_HEADS, L, HEAD_DK)).astype(np.float32))
    g_chunk = [float(v) for v in np.exp(lg * L).astype(np.float32)]
    return jnp.asarray(dmat.astype(np.float32)), bcast(w_in), bcast(w_tail), g_chunk


def _rope_tables(pos0, T):
    half = HEAD_DK // 2
    inv = ROPE_BASE ** (-jnp.arange(half, dtype=F32) / half)
    ang = (jnp.arange(T) + pos0).astype(F32)[:, None] * inv[None, :]
    cos, sin = jnp.cos(ang), jnp.sin(ang)
    return jnp.concatenate([cos, cos], axis=-1), jnp.concatenate([-sin, sin], axis=-1)


def _hgrn_intra(q, k, v, expo, mask_ref, L):
    sub = min(L, HGRN_SUB)
    n_sub = L // sub
    n_lev = mask_ref.shape[0] - 1
    top = len(expo) - n_lev
    assert n_sub in (1, 2) and top == n_sub - 1
    atts = []
    for i in range(n_sub):
        r = slice(i * sub, (i + 1) * sub)
        att = mask_ref[0] * _dot_nt(q[r], k[r])
        for lev in range(n_lev):
            sc = jnp.exp(expo[top + lev][r])
            att = att + mask_ref[lev + 1] * _dot_nt(q[r] * sc, k[r] * sc)
        atts.append(att)
    if n_sub == 1:
        return _dot(atts[0], v)
    cross = _dot_nt(q[sub:] * jnp.exp(expo[0][sub:]), k[:sub] * jnp.exp(expo[0][:sub]))
    return jnp.concatenate([_dot(atts[0], v[:sub]),
                            _dot(jnp.concatenate([cross, atts[1]], axis=1), v)], axis=0)


def _mixer0_body(*refs, L, layer_slot, g_chunk, fused):
    n_src = 3 if fused else 1
    src = refs[:n_src]
    (cos_ref, sin_ref, lbl_ref, ga_ref, gb_ref, sel_ref, mask_ref, dmat_ref, win_ref, wtail_ref,
     sa0_ref, sb0_ref, mix_ref, sa_ref, sb_ref, sat_scr, sbt_scr) = refs[n_src:]
    c = pl.program_id(1)
    last = pl.num_programs(1) - 1
    head_w = IN0_WIDTH // N_HEADS
    G = sa_ref.shape[0]

    @pl.when(c == 0)
    def _():
        for gi in range(G):
            for hh in range(N_HEADS):
                sat_scr[gi * N_HEADS + hh] = sa0_ref[gi, hh].T
                sbt_scr[gi * N_HEADS + hh] = sb0_ref[gi, hh].T

    lbl = lbl_ref[...]
    e = jnp.exp(lbl - jnp.max(lbl, axis=0, keepdims=True))
    lb_all = jnp.sum(e[:layer_slot + 1], axis=0, keepdims=True) / jnp.sum(e, axis=0, keepdims=True)

    if fused:
        x_ref, g_ref, w_ref = src
        hb = _rms(x_ref[...], g_ref[...]).astype(BF16)
    cosf = cos_ref[...]
    sinf = sin_ref[...]

    def project(hh):
        hcols = slice(hh * head_w, (hh + 1) * head_w)
        return jnp.dot(hb, w_ref[:, hcols], preferred_element_type=F32) if fused else src[0][:, hcols]

    z_next = project(0)
    for hh in range(N_HEADS):
        hs = slice(hh * HEAD_DK, (hh + 1) * HEAD_DK)
        z_head = z_next
        if hh + 1 < N_HEADS:
            z_next = project(hh + 1)
        lb = lb_all[:, hs]
        for gi in range(G):
            rs = slice(gi * L, (gi + 1) * L)
            si = gi * N_HEADS + hh
            zh = z_head[rs]
            part = lambda j: zh[:, j * HEAD_DK:(j + 1) * HEAD_DK]

            f = lb + (1.0 - lb) * _sigmoid(part(1))
            b, b_last, expo = _decay_exponents(sel_ref, jnp.log(f), L)
            q = part(0)
            k = 1.0 - f
            v = part(2)
            st = sat_scr[si]
            o = _dot_nt(q * jnp.exp(b), st) + _hgrn_intra(q, k, v, expo, mask_ref, L)
            sat_scr[si] = st * jnp.exp(b_last) + _dot_tn(v, k * jnp.exp(b_last - b))
            o = _rms(o * _sigmoid(part(3)), ga_ref[...])
            mix_ref[rs, hs] = o.astype(mix_ref.dtype)

            rq = part(4)
            rk = part(5)
            rv = part(6)
            rq = rq * cosf + pltpu.roll(rq, HEAD_DK // 2, 1) * sinf
            rk = (rk * cosf + pltpu.roll(rk, HEAD_DK // 2, 1) * sinf) * (HEAD_DK ** -0.5)
            rt = sbt_scr[si]
            ro = _dot(_dot_nt(rq, rk) * dmat_ref[hh], rv) + win_ref[hh] * _dot_nt(rq, rt)
            sbt_scr[si] = g_chunk[hh] * rt + _dot_tn(rv, rk * wtail_ref[hh])
            ro = _rms(ro, gb_ref[...]) * _silu(part(7))
            mix_ref[rs, KW + hh * HEAD_DK:KW + (hh + 1) * HEAD_DK] = ro.astype(mix_ref.dtype)

    @pl.when(c == last)
    def _():
        for gi in range(G):
            for hh in range(N_HEADS):
                sa_ref[gi, hh] = sat_scr[gi * N_HEADS + hh].T
                sb_ref[gi, hh] = sbt_scr[gi * N_HEADS + hh].T


def _mixer0(src, n_streams, T, pos0, sa0, sb0, lb_logits, ga, gb, layer_slot):
    L = min(CHUNK0, T)
    nc = T // L
    G = max(1, min(n_streams, STEP_ROWS0 // L)) if nc == 1 else 1
    fused = len(src) == 3
    sel = jnp.asarray(_hgrn_consts(min(SEL_BLOCK, L))[0], BF16)
    masks = jnp.asarray(_hgrn_consts(min(HGRN_SUB, L))[1])
    dmat, w_in, w_tail, g_chunk = _retention_consts(L)
    cosf, sinf = _rope_tables(pos0, T)
    full = lambda a: pl.BlockSpec(a.shape, lambda s, c: (0,) * a.ndim, pipeline_mode=pl.Buffered(1))
    rows = lambda a: pl.BlockSpec((G * L, a.shape[1]), lambda s, c: (s * nc + c, 0))
    state_spec = pl.BlockSpec((G, N_HEADS, HEAD_DK, HEAD_DK), lambda s, c: (s, 0, 0, 0))
    state_shape = jax.ShapeDtypeStruct((n_streams, N_HEADS, HEAD_DK, HEAD_DK), F32)
    ga2, gb2 = ga.reshape(1, HEAD_DK), gb.reshape(1, HEAD_DK)
    if fused:
        src = (src[0], src[1].reshape(1, -1), src[2])
        src_specs = [rows(src[0]), full(src[1]), full(src[2])]
    else:
        src_specs = [rows(src[0])]
    consts = (lb_logits, ga2, gb2, sel, masks, dmat, w_in, w_tail)
    return pl.pallas_call(
        functools.partial(_mixer0_body, L=L, layer_slot=layer_slot, g_chunk=g_chunk, fused=fused),
        grid=(n_streams // G, nc),
        in_specs=src_specs
                 + [pl.BlockSpec((L, HEAD_DK), lambda s, c: (c, 0)), pl.BlockSpec((L, HEAD_DK), lambda s, c: (c, 0))]
                 + [full(a) for a in consts] + [state_spec, state_spec],
        out_specs=[pl.BlockSpec((G * L, 2 * KW), lambda s, c: (s * nc + c, 0)), state_spec, state_spec],
        out_shape=[jax.ShapeDtypeStruct((n_streams * T, 2 * KW), BF16), state_shape, state_shape],
        scratch_shapes=[pltpu.VMEM((G * N_HEADS, HEAD_DK, HEAD_DK), F32),
                        pltpu.VMEM((G * N_HEADS, HEAD_DK, HEAD_DK), F32)],
        compiler_params=pltpu.CompilerParams(dimension_semantics=("arbitrary", "arbitrary"),
                                             vmem_limit_bytes=VMEM_LIMIT_BYTES),
        name="mixer0_T%d" % T,
    )(*src, cosf, sinf, *consts, sa0, sb0)


def _mixer1_body(*refs, L, fused, zero_init):
    it = iter(refs)
    take = lambda n: [next(it) for _ in range(n)]
    src = take(3 if fused else 1)
    (cw_ref, cb_ref, wq_ref, wk_ref, wv_ref, wgate_ref, bgate_ref, tri_ref, gn_ref, skip_ref) = take(10)
    init = None if zero_init else take(4)
    hg_ref, c_ref, n_ref, m_ref, conv_ref = take(5)
    xbuf = take(1)[0]
    c = pl.program_id(1)
    last = pl.num_programs(1) - 1
    n_taps = C_CONV - 1
    kv_state = zero_init

    @pl.when(c == 0)
    def _():
        if zero_init:
            c_ref[...] = jnp.zeros_like(c_ref)
            n_ref[...] = jnp.zeros_like(n_ref)
            m_ref[...] = jnp.zeros_like(m_ref)
            xbuf[0:CONV_HDR, :] = jnp.zeros((CONV_HDR, C_INNER), F32)
        else:
            c0_ref, n0_ref, m0_ref, conv0_ref = init
            c_ref[...] = c0_ref[...]
            n_ref[...] = n0_ref[...]
            m_ref[...] = m0_ref[...]
            xbuf[CONV_HDR - n_taps:CONV_HDR, :] = conv0_ref[0]

    if fused:
        x_ref, g_ref, w_ref = src
        hb = _rms(x_ref[...], g_ref[...]).astype(BF16)
        xm = jnp.dot(hb, w_ref[:, :C_INNER], preferred_element_type=F32)
        zg = jnp.dot(hb, w_ref[:, C_INNER:], preferred_element_type=F32)
    else:
        xm = src[0][:, :C_INNER]
        zg = src[0][:, C_INNER:]

    xbuf[CONV_HDR:CONV_HDR + L, :] = xm
    xc = cb_ref[...] + xm * cw_ref[n_taps:n_taps + 1, :]
    for w in range(n_taps):
        xc = xc + xbuf[CONV_HDR - n_taps + w:CONV_HDR - n_taps + w + L, :] * cw_ref[w:w + 1, :]
    tail = xbuf[CONV_HDR + L - n_taps:CONV_HDR + L, :]
    xbuf[CONV_HDR - n_taps:CONV_HDR, :] = tail
    xc = _silu(xc)

    xcb = xc.astype(BF16)
    xmb = xm.astype(BF16)
    qs, ks, vs = [], [], []
    for g in range(C_INNER // BD_GROUP):
        gs = slice(g * BD_GROUP, (g + 1) * BD_GROUP)
        qs.append(jnp.dot(xcb[:, gs], wq_ref[g], preferred_element_type=F32))
        ks.append(jnp.dot(xcb[:, gs], wk_ref[g], preferred_element_type=F32))
        vs.append(jnp.dot(xmb[:, gs], wv_ref[g], preferred_element_type=F32))
    q = jnp.concatenate(qs, axis=1)
    k = jnp.concatenate(ks, axis=1) * (C_DH ** -0.5)
    v = jnp.concatenate(vs, axis=1)

    gates = (_dot(q, wgate_ref[0:C_INNER, :]) + _dot(k, wgate_ref[C_INNER:2 * C_INNER, :])
             + _dot(v, wgate_ref[2 * C_INNER:3 * C_INNER, :]) + bgate_ref[...])
    bcum = _exact_dot(tri_ref[...], _log_sigmoid(gates))
    lane = lax.broadcasted_iota(jnp.int32, gates.shape, 1)
    rows = jnp.where(lane < N_HEADS, gates, bcum).T
    ti = lax.broadcasted_iota(jnp.int32, (L, L), 0)
    si = lax.broadcasted_iota(jnp.int32, (L, L), 1)
    causal = si <= ti

    for hh in range(N_HEADS):
        hs = slice(hh * C_DH, (hh + 1) * C_DH)
        b_col = bcum[:, N_HEADS + hh:N_HEADS + hh + 1]
        i_col = gates[:, hh:hh + 1]
        b_row = rows[N_HEADS + hh:N_HEADS + hh + 1, :]
        i_row = rows[hh:hh + 1, :]
        m_prev = m_ref[0, :, hh:hh + 1]
        cm = c_ref[0, hh]
        nv = n_ref[0, hh:hh + 1, :]
        qh, kh, vh = q[:, hs], k[:, hs], v[:, hs]

        lw = jnp.where(causal, b_col - b_row + i_row, -jnp.inf)
        lp = b_col + m_prev
        mj = jnp.maximum(lp, jnp.max(lw, axis=-1, keepdims=True))
        wgt = jnp.exp(lw - mj)
        wp = jnp.exp(lp - mj)
        s = _dot_nt(qh, kh) * wgt
        num = _dot(s, vh) + wp * (_dot(qh, cm) if kv_state else _dot_nt(qh, cm))
        den = jnp.sum(s, axis=-1, keepdims=True) + wp * jnp.sum(qh * nv, axis=-1, keepdims=True)
        hcell = num / jnp.maximum(jnp.abs(den), jnp.exp(-mj))

        m_new = mj[L - 1:L, :]
        b_last = b_col[L - 1:L, :]
        ws = jnp.exp(b_last - b_col + i_col - m_new)
        wpl = jnp.exp(b_last + m_prev - m_new)
        c_ref[0, hh] = wpl * cm + (_dot_tn(kh, vh * ws) if kv_state else _dot_tn(vh * ws, kh))
        n_ref[0, hh:hh + 1, :] = wpl * nv + jnp.sum(kh * ws, axis=0, keepdims=True)
        m_ref[0, :, hh:hh + 1] = m_new

        hc = hcell - jnp.mean(hcell, axis=-1, keepdims=True)
        hn = hc * lax.rsqrt(jnp.mean(hc * hc, axis=-1, keepdims=True) + EPS) * gn_ref[...]
        out = (hn + skip_ref[:, hs] * xc[:, hs]) * _silu(zg[:, hs])
        hg_ref[:, hs] = out.astype(hg_ref.dtype)

    @pl.when(c == last)
    def _():
        conv_ref[0] = tail
        if kv_state:
            for hh in range(N_HEADS):
                c_ref[0, hh] = c_ref[0, hh].T


def _mixer1(src, n_streams, T, states, cw, cb, wq, wk, wv, wgate, bgate, gn, skip):
    L = min(CHUNK1, T)
    nc = T // L
    fused = len(src) == 3
    zero_init = states is None
    tri = jnp.asarray(np.tril(np.ones((L, L), np.float32)), BF16)
    full = lambda a: pl.BlockSpec(a.shape, lambda s, c: (0,) * a.ndim, pipeline_mode=pl.Buffered(1))
    rows = lambda a: pl.BlockSpec((L, a.shape[1]), lambda s, c: (s * nc + c, 0))
    per_stream = lambda shape: pl.BlockSpec((1,) + shape[1:], lambda s, c: (s,) + (0,) * (len(shape) - 1))
    gn2, skip2, cb2 = gn.reshape(1, C_DH), skip.reshape(1, C_INNER), cb.reshape(1, C_INNER)
    state_shapes = ((n_streams, N_HEADS, C_DH, C_DH), (n_streams, N_HEADS, C_DH), (n_streams, 1, N_HEADS),
                    (n_streams, C_CONV - 1, C_INNER))
    scratch = [pltpu.VMEM((CONV_HDR + L, C_INNER), F32)]
    if fused:
        src = (src[0], src[1].reshape(1, -1), src[2])
        src_specs = [rows(src[0]), full(src[1]), full(src[2])]
    else:
        src_specs = [rows(src[0])]
    consts = (cw, cb2, wq, wk, wv, wgate, bgate, tri, gn2, skip2)
    if not zero_init:
        states = tuple(a.reshape(shape) for a, shape in zip(states, state_shapes))
    hg, c1, n1, m1, conv1 = pl.pallas_call(
        functools.partial(_mixer1_body, L=L, fused=fused, zero_init=zero_init),
        grid=(n_streams, nc),
        in_specs=src_specs + [full(a) for a in consts]
                 + ([] if zero_init else [per_stream(shape) for shape in state_shapes]),
        out_specs=[pl.BlockSpec((L, C_INNER), lambda s, c: (s * nc + c, 0))]
                  + [per_stream(shape) for shape in state_shapes],
        out_shape=[jax.ShapeDtypeStruct((n_streams * T, C_INNER), BF16)]
                  + [jax.ShapeDtypeStruct(shape, F32) for shape in state_shapes],
        scratch_shapes=scratch,
        compiler_params=pltpu.CompilerParams(dimension_semantics=("arbitrary", "arbitrary"),
                                             vmem_limit_bytes=VMEM_LIMIT_BYTES),
        name="mixer1_T%d" % T,
    )(*src, *consts, *(() if zero_init else states))
    return hg, c1, n1, m1.reshape(n_streams, N_HEADS), conv1


def _dense_blockdiag(w):
    rows = w.reshape(-1, BD_GROUP, C_BLOCK)
    tiled = jnp.tile(rows, (1, 1, BD_GROUP // C_BLOCK))
    idx = np.arange(BD_GROUP) // C_BLOCK
    same_block = jnp.asarray(idx[:, None] == idx[None, :])
    return jnp.where(same_block, tiled, 0.0).astype(BF16)


def kernel(x_prompt, x_sample, state_hgrn, state_ret, state_mlstm_c, state_mlstm_n, state_mlstm_m, state_conv,
           norm_mix, norm_ffn, norm_final, w_in0, lb_logits, hgrn_norm, ret_norm, w_out0,
           w_up1, conv_w, conv_b, w_q1, w_k1, w_v1, w_ig, b_ig, w_fg, b_fg, mlstm_norm, mlstm_skip, w_down1,
           w_ffn_gate, w_ffn_up, w_ffn_down):
    bp, tp, d = x_prompt.shape
    bs, ts, _ = x_sample.shape
    mp, ms = bp * tp, bs * ts
    past_len = 2048
    zeros = lambda *shape: jnp.zeros(shape, F32)
    w_in, w_out, w_up, w_down, wf_gate, wf_up, wf_down = _cast_weights(
        [w_in0[0], w_out0[0], w_up1[0], w_down1[0], w_ffn_gate, w_ffn_up, w_ffn_down])

    xp = x_prompt.reshape(mp, d)
    xs = x_sample.reshape(ms, d)

    ffn0 = (w_out, norm_ffn[0], wf_gate, wf_up, wf_down, 0, norm_final, False)
    mix_p, hg_p, rt_p = _mixer0((xp, norm_mix[0], w_in), bp, tp, 0, zeros(bp, N_HEADS, HEAD_DK, HEAD_DK),
                                zeros(bp, N_HEADS, HEAD_DK, HEAD_DK), lb_logits, hgrn_norm[0], ret_norm[0], 0)
    mix_s, hg_s, rt_s = _mixer0((_norm_matmul(xs, norm_mix[0], w_in),), bs, ts, past_len, state_hgrn[0],
                                state_ret[0], lb_logits, hgrn_norm[0], ret_norm[0], 0)
    xp = _proj_ffn(xp, mix_p, *ffn0)
    xs = _proj_ffn(xs, mix_s, *ffn0)

    ffn1 = (w_down, norm_ffn[1], wf_gate, wf_up, wf_down, 1, norm_final, True)
    wgate = jnp.pad(jnp.concatenate([w_ig[0], w_fg[0]], axis=1), ((0, 0), (0, GATE_LANES - 2 * N_HEADS))).astype(BF16)
    bgate = jnp.pad(jnp.concatenate([b_ig[0], b_fg[0]]), (0, GATE_LANES - 2 * N_HEADS)).reshape(1, GATE_LANES)
    m1_consts = (conv_w[0], conv_b[0], _dense_blockdiag(w_q1[0]), _dense_blockdiag(w_k1[0]),
                 _dense_blockdiag(w_v1[0]), wgate, bgate, mlstm_norm[0], mlstm_skip[0])
    hg1_p, mc_p, mn_p, mm_p, cv_p = _mixer1((xp, norm_mix[1], w_up), bp, tp, None, *m1_consts)
    hg1_s, mc_s, mn_s, mm_s, cv_s = _mixer1((_norm_matmul(xs, norm_mix[1], w_up),), bs, ts,
                                            (state_mlstm_c[0], state_mlstm_n[0], state_mlstm_m[0], state_conv[0]),
                                            *m1_consts)
    yp = _proj_ffn(xp, hg1_p, *ffn1)
    ys = _proj_ffn(xs, hg1_s, *ffn1)

    lead = lambda a: a[None]
    return (yp.reshape(bp, tp, d), ys.reshape(bs, ts, d),
            lead(hg_p), lead(hg_s), lead(rt_p), lead(rt_s),
            lead(mc_p), lead(mc_s), lead(mn_p), lead(mn_s), lead(mm_p), lead(mm_s), lead(cv_p), lead(cv_s))
```

```python
import functools
import math

import numpy as np
import jax
import jax.numpy as jnp
from jax import lax
from jax.experimental import pallas as pl
from jax.experimental.pallas import tpu as pltpu

F32 = jnp.float32
BF16 = jnp.bfloat16

D_MODEL = 1024
CHUNK0 = 256
CHUNK1 = 256
STEP_ROWS0 = 128
STEP_ROWS1 = 32
SEL_BLOCK = 64
HGRN_SUB = 128
EPS = 1e-6
SUBLANES = 8
N_HEADS = 4
HEAD_DK = 128
KW = N_HEADS * HEAD_DK
IN0_WIDTH = 8 * KW
ROPE_BASE = 10000.0
C_INNER = 2 * D_MODEL
C_DH = C_INNER // N_HEADS
C_CONV = 4
C_BLOCK = 4
BD_GROUP = 256
D_FF = -(-8 * D_MODEL // (3 * 256)) * 256
GATE_LANES = 128
CONV_HDR = 8
VMEM_LIMIT_BYTES = 56 * 1024 * 1024
ROW_TILE = 512
CAST_STEPS = 16
SIDE_BLOCKS = 32


def _dot(a, b):
    return jnp.dot(a.astype(BF16), b.astype(BF16), preferred_element_type=F32)


def _dot_nt(a, b):
    return lax.dot_general(a.astype(BF16), b.astype(BF16), (((1,), (1,)), ((), ())),
                           preferred_element_type=F32)


def _dot_tn(a, b):
    return lax.dot_general(a.astype(BF16), b.astype(BF16), (((0,), (0,)), ((), ())),
                           preferred_element_type=F32)


def _exact_dot(sel, x):
    hi = x.astype(BF16)
    lo = (x - hi.astype(F32)).astype(BF16)
    n = x.shape[1]
    both = jnp.dot(sel, jnp.concatenate([hi, lo], axis=1), preferred_element_type=F32)
    return both[:, :n] + both[:, n:]


def _rms(x, g):
    return x * lax.rsqrt(jnp.mean(x * x, axis=-1, keepdims=True) + EPS) * g


def _sigmoid(x):
    return 1.0 / (1.0 + jnp.exp(-x))


def _silu(x):
    return x * _sigmoid(x)


def _log_sigmoid(x):
    return jnp.minimum(x, 0.0) - jnp.log(1.0 + jnp.exp(-jnp.abs(x)))


def _norm_matmul_body(x_ref, g_ref, w_ref, o_ref):
    h = _rms(x_ref[...], g_ref[...])
    o_ref[...] = jnp.dot(h.astype(BF16), w_ref[...], preferred_element_type=F32)


def _norm_matmul(x, g, w):
    m, d = x.shape
    n = w.shape[1]
    return pl.pallas_call(
        _norm_matmul_body,
        grid=(m // ROW_TILE,),
        in_specs=[pl.BlockSpec((ROW_TILE, d), lambda i: (i, 0)),
                  pl.BlockSpec((1, d), lambda i: (0, 0)),
                  pl.BlockSpec((d, n), lambda i: (0, 0), pipeline_mode=pl.Buffered(1))],
        out_specs=pl.BlockSpec((ROW_TILE, n), lambda i: (i, 0)),
        out_shape=jax.ShapeDtypeStruct((m, n), F32),
        compiler_params=pltpu.CompilerParams(dimension_semantics=("arbitrary",),
                                             vmem_limit_bytes=VMEM_LIMIT_BYTES),
        name="norm_matmul",
    )(x, g.reshape(1, d), w)


def _cast_weights_body(*refs, n):
    ins, outs = refs[:n], refs[n:]
    n_slot = IN0_WIDTH // KW
    for hh in range(N_HEADS):
        for j in range(n_slot):
            dst = slice((hh * n_slot + j) * HEAD_DK, (hh * n_slot + j + 1) * HEAD_DK)
            src = slice((j * N_HEADS + hh) * HEAD_DK, (j * N_HEADS + hh + 1) * HEAD_DK)
            outs[0][:, dst] = ins[0][:, src].astype(BF16)
    for i_ref, o_ref in zip(ins[1:], outs[1:]):
        o_ref[...] = i_ref[...].astype(BF16)


def _cast_weights(ws):
    flat = [w.reshape(-1, w.shape[-1]) for w in ws]
    spec = lambda w: pl.BlockSpec((w.shape[0] // CAST_STEPS, w.shape[1]), lambda i: (i, 0))
    out = pl.pallas_call(
        functools.partial(_cast_weights_body, n=len(flat)),
        grid=(CAST_STEPS,),
        in_specs=[spec(w) for w in flat],
        out_specs=[spec(w) for w in flat],
        out_shape=[jax.ShapeDtypeStruct(w.shape, BF16) for w in flat],
        compiler_params=pltpu.CompilerParams(dimension_semantics=("arbitrary",),
                                             vmem_limit_bytes=VMEM_LIMIT_BYTES),
        name="cast_weights",
    )(*flat)
    return [o.reshape(w.shape) for o, w in zip(out, ws)]


def _proj_ffn_body(x_ref, a_ref, wo_ref, g_ref, wg_ref, wu_ref, wd_ref, gf_ref, o_ref, *, final_norm):
    x1 = x_ref[...] + jnp.dot(a_ref[...], wo_ref[...], preferred_element_type=F32)
    h = _rms(x1, g_ref[...]).astype(BF16)
    gate = jnp.dot(h, wg_ref[...], preferred_element_type=F32)
    up = jnp.dot(h, wu_ref[...], preferred_element_type=F32)
    t = (_silu(gate) * up).astype(BF16)
    x2 = x1 + jnp.dot(t, wd_ref[...], preferred_element_type=F32)
    if final_norm:
        x2 = _rms(x2, gf_ref[...])
    o_ref[...] = x2


def _proj_ffn(x, a, wo, g, wg, wu, wd, layer, gf, final_norm):
    m, d = x.shape
    ka = a.shape[1]
    ff = wg.shape[2]
    const = lambda shape: pl.BlockSpec(shape, lambda i: (0, 0), pipeline_mode=pl.Buffered(1))
    of_layer = lambda shape: pl.BlockSpec((None,) + shape, lambda i: (layer, 0, 0), pipeline_mode=pl.Buffered(1))
    return pl.pallas_call(
        functools.partial(_proj_ffn_body, final_norm=final_norm),
        grid=(m // ROW_TILE,),
        in_specs=[pl.BlockSpec((ROW_TILE, d), lambda i: (i, 0)),
                  pl.BlockSpec((ROW_TILE, ka), lambda i: (i, 0)),
                  const((ka, d)), const((1, d)), of_layer((d, ff)), of_layer((d, ff)), of_layer((ff, d)),
                  const((1, d))],
        out_specs=pl.BlockSpec((ROW_TILE, d), lambda i: (i, 0)),
        out_shape=jax.ShapeDtypeStruct((m, d), F32),
        compiler_params=pltpu.CompilerParams(dimension_semantics=("arbitrary",),
                                             vmem_limit_bytes=VMEM_LIMIT_BYTES),
        name="proj_ffn",
    )(x, a, wo, g.reshape(1, d), wg, wu, wd, gf.reshape(1, d))


def _hgrn_consts(L):
    r = np.arange(L)[:, None]
    t = np.arange(L)[None, :]
    sels = [t <= r]
    masks = [r == t]
    h = L // 2
    while h >= 1:
        base = (r // (2 * h)) * (2 * h)
        upper = (r % (2 * h)) >= h
        if h < SUBLANES:
            sels.append(np.where(upper, (t >= base + h) & (t <= r), (t > r) & (t <= base + h - 1)))
        masks.append((r // (2 * h) == t // (2 * h)) & upper & ((t % (2 * h)) < h))
        h //= 2
    return np.concatenate(sels, axis=0).astype(np.float32), np.stack(masks).astype(np.float32)


def _decay_exponents(sel_ref, logf, L):
    blk = min(SEL_BLOCK, L)
    nb = L // blk
    res = [_exact_dot(sel_ref[...], logf[i * blk:(i + 1) * blk]) for i in range(nb)]
    cum = [r[0:blk] for r in res]
    pre = [jnp.zeros_like(cum[0][0:1])]
    for i in range(nb):
        pre.append(pre[i] + cum[i][blk - 1:blk])
    cat = lambda parts: jnp.concatenate(parts, axis=0) if len(parts) > 1 else parts[0]
    b = cat([cum[i] + pre[i] for i in range(nb)])
    levels = []
    nbh = nb // 2
    while nbh >= 1:
        parts = []
        for i in range(nb):
            ref = pre[(i // (2 * nbh)) * 2 * nbh + nbh]
            if (i % (2 * nbh)) >= nbh:
                parts.append(cum[i] + (pre[i] - ref))
            else:
                parts.append((ref - pre[i]) - cum[i])
        levels.append(cat(parts))
        nbh //= 2
    h = blk // 2
    while h >= SUBLANES:
        parts = []
        for i in range(nb):
            for base in range(0, blk, 2 * h):
                ref = cum[i][base + h - 1:base + h]
                parts += [ref - cum[i][base:base + h], cum[i][base + h:base + 2 * h] - ref]
        levels.append(cat(parts))
        h //= 2
    for lev in range(sel_ref.shape[0] // blk - 1):
        levels.append(cat([r[(lev + 1) * blk:(lev + 2) * blk] for r in res]))
    return b, pre[nb], levels


def _retention_consts(L):
    lg = np.log1p(-np.exp2(-5.0 - np.arange(N_HEADS, dtype=np.float32))).astype(np.float32)
    idx = np.arange(L, dtype=np.float32)
    rel = idx[:, None] - idx[None, :]
    dmat = np.where(rel >= 0, np.exp(lg[:, None, None] * np.maximum(rel, 0.0)), 0.0)
    w_in = np.exp(lg[:, None] * (idx + 1.0))
    w_tail = np.exp(lg[:, None] * (L - 1.0 - idx))
    bcast = lambda v: jnp.asarray(np.broadcast_to(v[:, :, None], (N_HEADS, L, HEAD_DK)).astype(np.float32))
    g_chunk = [float(v) for v in np.exp(lg * L).astype(np.float32)]
    return jnp.asarray(dmat.astype(np.float32)), bcast(w_in), bcast(w_tail), g_chunk


def _rope_tables(pos0, T):
    half = HEAD_DK // 2
    inv = ROPE_BASE ** (-jnp.arange(half, dtype=F32) / half)
    ang = (jnp.arange(T) + pos0).astype(F32)[:, None] * inv[None, :]
    cos, sin = jnp.cos(ang), jnp.sin(ang)
    return jnp.concatenate([cos, cos], axis=-1), jnp.concatenate([-sin, sin], axis=-1)


def _hgrn_intra(q, k, v, expo, mask_ref, L):
    sub = min(L, HGRN_SUB)
    n_sub = L // sub
    n_lev = mask_ref.shape[0] - 1
    top = len(expo) - n_lev
    assert n_sub in (1, 2) and top == n_sub - 1
    atts = []
    for i in range(n_sub):
        r = slice(i * sub, (i + 1) * sub)
        att = mask_ref[0] * _dot_nt(q[r], k[r])
        for lev in range(n_lev):
            sc = jnp.exp(expo[top + lev][r])
            att = att + mask_ref[lev + 1] * _dot_nt(q[r] * sc, k[r] * sc)
        atts.append(att)
    if n_sub == 1:
        return _dot(atts[0], v)
    cross = _dot_nt(q[sub:] * jnp.exp(expo[0][sub:]), k[:sub] * jnp.exp(expo[0][:sub]))
    return jnp.concatenate([_dot(atts[0], v[:sub]),
                            _dot(jnp.concatenate([cross, atts[1]], axis=1), v)], axis=0)


def _mixer0_body(*refs, L, layer_slot, g_chunk, fused, n_side, side_every):
    it = iter(refs)
    take = lambda n: [next(it) for _ in range(n)]
    src = take(3 if fused else 1)
    (cos_ref, sin_ref, lbl_ref, ga_ref, gb_ref, sel_ref, mask_ref, dmat_ref, win_ref, wtail_ref,
     sa0_ref, sb0_ref) = take(12)
    side_in = take(n_side)
    mix_ref, sa_ref, sb_ref = take(3)
    side_out = take(n_side)
    sat_scr, sbt_scr = take(2)
    c = pl.program_id(1)
    last = pl.num_programs(1) - 1
    head_w = IN0_WIDTH // N_HEADS
    G = sa_ref.shape[0]

    if n_side:
        @pl.when((pl.program_id(0) * pl.num_programs(1) + c) % side_every == 0)
        def _():
            for i_ref, o_ref in zip(side_in, side_out):
                o_ref[...] = i_ref[...].astype(BF16)

    @pl.when(c == 0)
    def _():
        for gi in range(G):
            for hh in range(N_HEADS):
                sat_scr[gi * N_HEADS + hh] = sa0_ref[gi, hh].T
                sbt_scr[gi * N_HEADS + hh] = sb0_ref[gi, hh].T

    lbl = lbl_ref[...]
    e = jnp.exp(lbl - jnp.max(lbl, axis=0, keepdims=True))
    lb_all = jnp.sum(e[:layer_slot + 1], axis=0, keepdims=True) / jnp.sum(e, axis=0, keepdims=True)

    if fused:
        x_ref, g_ref, w_ref = src
        hb = _rms(x_ref[...], g_ref[...]).astype(BF16)
    cosf = cos_ref[...]
    sinf = sin_ref[...]

    def project(hh):
        hcols = slice(hh * head_w, (hh + 1) * head_w)
        return jnp.dot(hb, w_ref[:, hcols], preferred_element_type=F32) if fused else src[0][:, hcols]

    z_next = project(0)
    for hh in range(N_HEADS):
        hs = slice(hh * HEAD_DK, (hh + 1) * HEAD_DK)
        z_head = z_next
        if hh + 1 < N_HEADS:
            z_next = project(hh + 1)
        lb = lb_all[:, hs]
        for gi in range(G):
            rs = slice(gi * L, (gi + 1) * L)
            si = gi * N_HEADS + hh
            zh = z_head[rs]
            part = lambda j: zh[:, j * HEAD_DK:(j + 1) * HEAD_DK]

            f = lb + (1.0 - lb) * _sigmoid(part(1))
            b, b_last, expo = _decay_exponents(sel_ref, jnp.log(f), L)
            q = part(0)
            k = 1.0 - f
            v = part(2)
            st = sat_scr[si]
            o = _dot_nt(q * jnp.exp(b), st) + _hgrn_intra(q, k, v, expo, mask_ref, L)
            sat_scr[si] = st * jnp.exp(b_last) + _dot_tn(v, k * jnp.exp(b_last - b))
            o = _rms(o * _sigmoid(part(3)), ga_ref[...])
            mix_ref[rs, hs] = o.astype(mix_ref.dtype)

            rq = part(4)
            rk = part(5)
            rv = part(6)
            rq = rq * cosf + pltpu.roll(rq, HEAD_DK // 2, 1) * sinf
            rk = (rk * cosf + pltpu.roll(rk, HEAD_DK // 2, 1) * sinf) * (HEAD_DK ** -0.5)
            rt = sbt_scr[si]
            ro = _dot(_dot_nt(rq, rk) * dmat_ref[hh], rv) + win_ref[hh] * _dot_nt(rq, rt)
            sbt_scr[si] = g_chunk[hh] * rt + _dot_tn(rv, rk * wtail_ref[hh])
            ro = _rms(ro, gb_ref[...]) * _silu(part(7))
            mix_ref[rs, KW + hh * HEAD_DK:KW + (hh + 1) * HEAD_DK] = ro.astype(mix_ref.dtype)

    @pl.when(c == last)
    def _():
        for gi in range(G):
            for hh in range(N_HEADS):
                sa_ref[gi, hh] = sat_scr[gi * N_HEADS + hh].T
                sb_ref[gi, hh] = sbt_scr[gi * N_HEADS + hh].T


def _mixer0(src, n_streams, T, pos0, sa0, sb0, lb_logits, ga, gb, layer_slot, side_cast=()):
    L = min(CHUNK0, T)
    nc = T // L
    G = max(1, min(n_streams, STEP_ROWS0 // L)) if nc == 1 else 1
    fused = len(src) == 3
    sel = jnp.asarray(_hgrn_consts(min(SEL_BLOCK, L))[0], BF16)
    masks = jnp.asarray(_hgrn_consts(min(HGRN_SUB, L))[1])
    dmat, w_in, w_tail, g_chunk = _retention_consts(L)
    cosf, sinf = _rope_tables(pos0, T)
    full = lambda a: pl.BlockSpec(a.shape, lambda s, c: (0,) * a.ndim, pipeline_mode=pl.Buffered(1))
    rows = lambda a: pl.BlockSpec((G * L, a.shape[1]), lambda s, c: (s * nc + c, 0))
    state_spec = pl.BlockSpec((G, N_HEADS, HEAD_DK, HEAD_DK), lambda s, c: (s, 0, 0, 0))
    state_shape = jax.ShapeDtypeStruct((n_streams, N_HEADS, HEAD_DK, HEAD_DK), F32)
    ga2, gb2 = ga.reshape(1, HEAD_DK), gb.reshape(1, HEAD_DK)
    if fused:
        src = (src[0], src[1].reshape(1, -1), src[2])
        src_specs = [rows(src[0]), full(src[1]), full(src[2])]
    else:
        src_specs = [rows(src[0])]
    consts = (lb_logits, ga2, gb2, sel, masks, dmat, w_in, w_tail)
    side = [w.reshape(-1, w.shape[-1]) for w in side_cast]
    per_block = (n_streams // G) * nc // SIDE_BLOCKS if side else 1
    side_spec = lambda w: pl.BlockSpec((w.shape[0] // SIDE_BLOCKS, w.shape[1]),
                                       lambda s, c: ((s * nc + c) // per_block, 0))
    outs = pl.pallas_call(
        functools.partial(_mixer0_body, L=L, layer_slot=layer_slot, g_chunk=g_chunk, fused=fused,
                          n_side=len(side), side_every=per_block),
        grid=(n_streams // G, nc),
        in_specs=src_specs
                 + [pl.BlockSpec((L, HEAD_DK), lambda s, c: (c, 0)), pl.BlockSpec((L, HEAD_DK), lambda s, c: (c, 0))]
                 + [full(a) for a in consts] + [state_spec, state_spec] + [side_spec(w) for w in side],
        out_specs=[pl.BlockSpec((G * L, 2 * KW), lambda s, c: (s * nc + c, 0)), state_spec, state_spec]
                  + [side_spec(w) for w in side],
        out_shape=[jax.ShapeDtypeStruct((n_streams * T, 2 * KW), BF16), state_shape, state_shape]
                  + [jax.ShapeDtypeStruct(w.shape, BF16) for w in side],
        scratch_shapes=[pltpu.VMEM((G * N_HEADS, HEAD_DK, HEAD_DK), F32),
                        pltpu.VMEM((G * N_HEADS, HEAD_DK, HEAD_DK), F32)],
        compiler_params=pltpu.CompilerParams(dimension_semantics=("arbitrary", "arbitrary"),
                                             vmem_limit_bytes=VMEM_LIMIT_BYTES),
        name="mixer0_T%d" % T,
    )(*src, cosf, sinf, *consts, sa0, sb0, *side)
    return outs[0], outs[1], outs[2], [o.reshape(w.shape) for o, w in zip(outs[3:], side_cast)]


def _mixer1_body(*refs, L, fused, zero_init, single_chunk):
    it = iter(refs)
    take = lambda n: [next(it) for _ in range(n)]
    src = take(3 if fused else 1)
    (cw_ref, cb_ref, wq_ref, wk_ref, wv_ref, wgate_ref, bgate_ref, tri_ref, gn_ref, skip_ref) = take(10)
    init = None if zero_init else take(4)
    hg_ref, c_ref, n_ref, m_ref, conv_ref = take(5)
    xbuf = take(1)[0]
    c = pl.program_id(1)
    last = pl.num_programs(1) - 1
    n_taps = C_CONV - 1
    G = c_ref.shape[0]
    kv_state = zero_init

    @pl.when(c == 0)
    def _():
        if zero_init:
            c_ref[...] = jnp.zeros_like(c_ref)
            n_ref[...] = jnp.zeros_like(n_ref)
            m_ref[...] = jnp.zeros_like(m_ref)
            xbuf[:, 0:CONV_HDR, :] = jnp.zeros((G, CONV_HDR, C_INNER), F32)
        else:
            c0_ref, n0_ref, m0_ref, conv0_ref = init
            if not single_chunk:
                c_ref[...] = c0_ref[...]
                n_ref[...] = n0_ref[...]
                m_ref[...] = m0_ref[...]
            xbuf[:, CONV_HDR - n_taps:CONV_HDR, :] = conv0_ref[...]

    c_in, n_in, m_in = init[:3] if single_chunk and not zero_init else (c_ref, n_ref, m_ref)

    if fused:
        x_ref, g_ref, w_ref = src
        hb = _rms(x_ref[...], g_ref[...]).astype(BF16)
        xm = jnp.dot(hb, w_ref[:, :C_INNER], preferred_element_type=F32)
        zg = jnp.dot(hb, w_ref[:, C_INNER:], preferred_element_type=F32)
    else:
        xm = src[0][:, :C_INNER]
        zg = src[0][:, C_INNER:]

    xcs, tails = [], []
    for gi in range(G):
        xg = xm[gi * L:(gi + 1) * L]
        xbuf[gi, CONV_HDR:CONV_HDR + L, :] = xg
        acc = cb_ref[...] + xg * cw_ref[n_taps:n_taps + 1, :]
        for w in range(n_taps):
            acc = acc + xbuf[gi, CONV_HDR - n_taps + w:CONV_HDR - n_taps + w + L, :] * cw_ref[w:w + 1, :]
        tails.append(xbuf[gi, CONV_HDR + L - n_taps:CONV_HDR + L, :])
        xbuf[gi, CONV_HDR - n_taps:CONV_HDR, :] = tails[gi]
        xcs.append(acc)
    xc = _silu(jnp.concatenate(xcs, axis=0) if G > 1 else xcs[0])

    xcb = xc.astype(BF16)
    xmb = xm.astype(BF16)
    qs, ks, vs = [], [], []
    for g in range(C_INNER // BD_GROUP):
        gs = slice(g * BD_GROUP, (g + 1) * BD_GROUP)
        qs.append(jnp.dot(xcb[:, gs], wq_ref[g], preferred_element_type=F32))
        ks.append(jnp.dot(xcb[:, gs], wk_ref[g], preferred_element_type=F32))
        vs.append(jnp.dot(xmb[:, gs], wv_ref[g], preferred_element_type=F32))
    q = jnp.concatenate(qs, axis=1)
    k = jnp.concatenate(ks, axis=1) * (C_DH ** -0.5)
    v = jnp.concatenate(vs, axis=1)

    gates = (_dot(q, wgate_ref[0:C_INNER, :]) + _dot(k, wgate_ref[C_INNER:2 * C_INNER, :])
             + _dot(v, wgate_ref[2 * C_INNER:3 * C_INNER, :]) + bgate_ref[...])
    bcum = _exact_dot(tri_ref[...], _log_sigmoid(gates))
    lane = lax.broadcasted_iota(jnp.int32, gates.shape, 1)
    rows = jnp.where(lane < N_HEADS, gates, bcum).T
    ti = lax.broadcasted_iota(jnp.int32, (L, L), 0)
    si = lax.broadcasted_iota(jnp.int32, (L, L), 1)
    causal = si <= ti

    for gi, hh in [(gi, hh) for gi in range(G) for hh in range(N_HEADS)]:
        hs = slice(hh * C_DH, (hh + 1) * C_DH)
        rs = slice(gi * L, (gi + 1) * L)
        b_col = bcum[rs, N_HEADS + hh:N_HEADS + hh + 1]
        i_col = gates[rs, hh:hh + 1]
        b_row = rows[N_HEADS + hh:N_HEADS + hh + 1, rs]
        i_row = rows[hh:hh + 1, rs]
        m_prev = m_in[gi, :, hh:hh + 1]
        cm = c_in[gi, hh]
        nv = n_in[gi, hh:hh + 1, :]
        qh, kh, vh = q[rs, hs], k[rs, hs], v[rs, hs]

        lw = jnp.where(causal, b_col - b_row + i_row, -jnp.inf)
        lp = b_col + m_prev
        mj = jnp.maximum(lp, jnp.max(lw, axis=-1, keepdims=True))
        wgt = jnp.exp(lw - mj)
        wp = jnp.exp(lp - mj)
        s = _dot_nt(qh, kh) * wgt
        num = _dot(s, vh) + wp * (_dot(qh, cm) if kv_state else _dot_nt(qh, cm))
        den = jnp.sum(s, axis=-1, keepdims=True) + wp * jnp.sum(qh * nv, axis=-1, keepdims=True)
        hcell = num / jnp.maximum(jnp.abs(den), jnp.exp(-mj))

        m_new = mj[L - 1:L, :]
        b_last = b_col[L - 1:L, :]
        ws = jnp.exp(b_last - b_col + i_col - m_new)
        wpl = jnp.exp(b_last + m_prev - m_new)
        c_ref[gi, hh] = wpl * cm + (_dot_tn(kh, vh * ws) if kv_state else _dot_tn(vh * ws, kh))
        n_ref[gi, hh:hh + 1, :] = wpl * nv + jnp.sum(kh * ws, axis=0, keepdims=True)
        m_ref[gi, :, hh:hh + 1] = m_new

        hc = hcell - jnp.mean(hcell, axis=-1, keepdims=True)
        hn = hc * lax.rsqrt(jnp.mean(hc * hc, axis=-1, keepdims=True) + EPS) * gn_ref[...]
        out = (hn + skip_ref[:, hs] * xc[rs, hs]) * _silu(zg[rs, hs])
        hg_ref[rs, hs] = out.astype(hg_ref.dtype)

    @pl.when(c == last)
    def _():
        for gi in range(G):
            conv_ref[gi] = tails[gi]
            if kv_state:
                for hh in range(N_HEADS):
                    c_ref[gi, hh] = c_ref[gi, hh].T


def _mixer1(src, n_streams, T, states, cw, cb, wq, wk, wv, wgate, bgate, gn, skip):
    L = min(CHUNK1, T)
    nc = T // L
    G = max(1, min(n_streams, STEP_ROWS1 // L)) if nc == 1 else 1
    fused = len(src) == 3
    zero_init = states is None
    tri = jnp.asarray(np.kron(np.eye(G, dtype=np.float32), np.tril(np.ones((L, L), np.float32))), BF16)
    full = lambda a: pl.BlockSpec(a.shape, lambda s, c: (0,) * a.ndim, pipeline_mode=pl.Buffered(1))
    rows = lambda a: pl.BlockSpec((G * L, a.shape[1]), lambda s, c: (s * nc + c, 0))
    per_stream = lambda shape: pl.BlockSpec((G,) + shape[1:], lambda s, c: (s,) + (0,) * (len(shape) - 1))
    gn2, skip2, cb2 = gn.reshape(1, C_DH), skip.reshape(1, C_INNER), cb.reshape(1, C_INNER)
    state_shapes = ((n_streams, N_HEADS, C_DH, C_DH), (n_streams, N_HEADS, C_DH), (n_streams, 1, N_HEADS),
                    (n_streams, C_CONV - 1, C_INNER))
    scratch = [pltpu.VMEM((G, CONV_HDR + L, C_INNER), F32)]
    if fused:
        src = (src[0], src[1].reshape(1, -1), src[2])
        src_specs = [rows(src[0]), full(src[1]), full(src[2])]
    else:
        src_specs = [rows(src[0])]
    consts = (cw, cb2, wq, wk, wv, wgate, bgate, tri, gn2, skip2)
    if not zero_init:
        states = tuple(a.reshape(shape) for a, shape in zip(states, state_shapes))
    hg, c1, n1, m1, conv1 = pl.pallas_call(
        functools.partial(_mixer1_body, L=L, fused=fused, zero_init=zero_init, single_chunk=nc == 1),
        grid=(n_streams // G, nc),
        in_specs=src_specs + [full(a) for a in consts]
                 + ([] if zero_init else [per_stream(shape) for shape in state_shapes]),
        out_specs=[pl.BlockSpec((G * L, C_INNER), lambda s, c: (s * nc + c, 0))]
                  + [per_stream(shape) for shape in state_shapes],
        out_shape=[jax.ShapeDtypeStruct((n_streams * T, C_INNER), BF16)]
                  + [jax.ShapeDtypeStruct(shape, F32) for shape in state_shapes],
        scratch_shapes=scratch,
        compiler_params=pltpu.CompilerParams(dimension_semantics=("arbitrary", "arbitrary"),
                                             vmem_limit_bytes=VMEM_LIMIT_BYTES),
        name="mixer1_T%d" % T,
    )(*src, *consts, *(() if zero_init else states))
    return hg, c1, n1, m1.reshape(n_streams, N_HEADS), conv1


def _dense_blockdiag(w):
    rows = w.reshape(-1, BD_GROUP, C_BLOCK)
    tiled = jnp.tile(rows, (1, 1, BD_GROUP // C_BLOCK))
    idx = np.arange(BD_GROUP) // C_BLOCK
    same_block = jnp.asarray(idx[:, None] == idx[None, :])
    return jnp.where(same_block, tiled, 0.0).astype(BF16)


def kernel(x_prompt, x_sample, state_hgrn, state_ret, state_mlstm_c, state_mlstm_n, state_mlstm_m, state_conv,
           norm_mix, norm_ffn, norm_final, w_in0, lb_logits, hgrn_norm, ret_norm, w_out0,
           w_up1, conv_w, conv_b, w_q1, w_k1, w_v1, w_ig, b_ig, w_fg, b_fg, mlstm_norm, mlstm_skip, w_down1,
           w_ffn_gate, w_ffn_up, w_ffn_down):
    bp, tp, d = x_prompt.shape
    bs, ts, _ = x_sample.shape
    mp, ms = bp * tp, bs * ts
    past_len = 2048
    zeros = lambda *shape: jnp.zeros(shape, F32)
    w_in, = _cast_weights([w_in0[0]])
    later_weights = [w_out0[0], w_up1[0], w_down1[0], w_ffn_gate, w_ffn_up, w_ffn_down]

    xp = x_prompt.reshape(mp, d)
    xs = x_sample.reshape(ms, d)

    mix_p, hg_p, rt_p, (w_out, w_up, w_down, wf_gate, wf_up, wf_down) = _mixer0(
        (xp, norm_mix[0], w_in), bp, tp, 0, zeros(bp, N_HEADS, HEAD_DK, HEAD_DK),
        zeros(bp, N_HEADS, HEAD_DK, HEAD_DK), lb_logits, hgrn_norm[0], ret_norm[0], 0, side_cast=later_weights)
    ffn0 = (w_out, norm_ffn[0], wf_gate, wf_up, wf_down, 0, norm_final, False)
    mix_s, hg_s, rt_s, _ = _mixer0((_norm_matmul(xs, norm_mix[0], w_in),), bs, ts, past_len, state_hgrn[0],
                                   state_ret[0], lb_logits, hgrn_norm[0], ret_norm[0], 0)
    xp = _proj_ffn(xp, mix_p, *ffn0)
    xs = _proj_ffn(xs, mix_s, *ffn0)

    ffn1 = (w_down, norm_ffn[1], wf_gate, wf_up, wf_down, 1, norm_final, True)
    wgate = jnp.pad(jnp.concatenate([w_ig[0], w_fg[0]], axis=1), ((0, 0), (0, GATE_LANES - 2 * N_HEADS))).astype(BF16)
    bgate = jnp.pad(jnp.concatenate([b_ig[0], b_fg[0]]), (0, GATE_LANES - 2 * N_HEADS)).reshape(1, GATE_LANES)
    m1_consts = (conv_w[0], conv_b[0], _dense_blockdiag(w_q1[0]), _dense_blockdiag(w_k1[0]),
                 _dense_blockdiag(w_v1[0]), wgate, bgate, mlstm_norm[0], mlstm_skip[0])
    hg1_p, mc_p, mn_p, mm_p, cv_p = _mixer1((xp, norm_mix[1], w_up), bp, tp, None, *m1_consts)
    hg1_s, mc_s, mn_s, mm_s, cv_s = _mixer1((_norm_matmul(xs, norm_mix[1], w_up),), bs, ts,
                                            (state_mlstm_c[0], state_mlstm_n[0], state_mlstm_m[0], state_conv[0]),
                                            *m1_consts)
    yp = _proj_ffn(xp, hg1_p, *ffn1)
    ys = _proj_ffn(xs, hg1_s, *ffn1)

    lead = lambda a: a[None]
    return (yp.reshape(bp, tp, d), ys.reshape(bs, ts, d),
            lead(hg_p), lead(hg_s), lead(rt_p), lead(rt_s),
            lead(mc_p), lead(mc_s), lead(mn_p), lead(mn_s), lead(mm_p), lead(mm_s), lead(cv_p), lead(cv_s))
```

```python
import functools
import math

import numpy as np
import jax
import jax.numpy as jnp
from jax import lax
from jax.experimental import pallas as pl
from jax.experimental.pallas import tpu as pltpu

F32 = jnp.float32
BF16 = jnp.bfloat16

D_MODEL = 1024
CHUNK0 = 256
CHUNK1 = 256
STEP_ROWS0 = 128
STEP_ROWS1 = 32
SEL_BLOCK = 64
HGRN_SUB = 128
EPS = 1e-6
SUBLANES = 8
N_HEADS = 4
HEAD_DK = 128
KW = N_HEADS * HEAD_DK
IN0_WIDTH = 8 * KW
ROPE_BASE = 10000.0
C_INNER = 2 * D_MODEL
C_DH = C_INNER // N_HEADS
C_CONV = 4
C_BLOCK = 4
BD_GROUP = 256
D_FF = -(-8 * D_MODEL // (3 * 256)) * 256
GATE_LANES = 128
CONV_HDR = 8
VMEM_LIMIT_BYTES = 56 * 1024 * 1024
ROW_TILE = 512
CAST_STEPS = 16
SIDE_BLOCKS = 32


def _dot(a, b):
    return jnp.dot(a.astype(BF16), b.astype(BF16), preferred_element_type=F32)


def _dot_nt(a, b):
    return lax.dot_general(a.astype(BF16), b.astype(BF16), (((1,), (1,)), ((), ())),
                           preferred_element_type=F32)


def _dot_tn(a, b):
    return lax.dot_general(a.astype(BF16), b.astype(BF16), (((0,), (0,)), ((), ())),
                           preferred_element_type=F32)


def _exact_dot(sel, x):
    hi = x.astype(BF16)
    lo = (x - hi.astype(F32)).astype(BF16)
    n = x.shape[1]
    both = jnp.dot(sel, jnp.concatenate([hi, lo], axis=1), preferred_element_type=F32)
    return both[:, :n] + both[:, n:]


def _rms(x, g):
    return x * lax.rsqrt(jnp.mean(x * x, axis=-1, keepdims=True) + EPS) * g


def _sigmoid(x):
    return 1.0 / (1.0 + jnp.exp(-x))


def _silu(x):
    return x * _sigmoid(x)


def _log_sigmoid(x):
    return jnp.minimum(x, 0.0) - jnp.log(1.0 + jnp.exp(-jnp.abs(x)))


def _norm_matmul_body(x_ref, g_ref, w_ref, o_ref):
    h = _rms(x_ref[...], g_ref[...])
    o_ref[...] = jnp.dot(h.astype(BF16), w_ref[...], preferred_element_type=F32)


def _norm_matmul(x, g, w):
    m, d = x.shape
    n = w.shape[1]
    return pl.pallas_call(
        _norm_matmul_body,
        grid=(m // ROW_TILE,),
        in_specs=[pl.BlockSpec((ROW_TILE, d), lambda i: (i, 0)),
                  pl.BlockSpec((1, d), lambda i: (0, 0)),
                  pl.BlockSpec((d, n), lambda i: (0, 0), pipeline_mode=pl.Buffered(1))],
        out_specs=pl.BlockSpec((ROW_TILE, n), lambda i: (i, 0)),
        out_shape=jax.ShapeDtypeStruct((m, n), F32),
        compiler_params=pltpu.CompilerParams(dimension_semantics=("arbitrary",),
                                             vmem_limit_bytes=VMEM_LIMIT_BYTES),
        name="norm_matmul",
    )(x, g.reshape(1, d), w)


def _cast_weights_body(*refs, n):
    ins, outs = refs[:n], refs[n:]
    n_slot = IN0_WIDTH // KW
    for hh in range(N_HEADS):
        for j in range(n_slot):
            dst = slice((hh * n_slot + j) * HEAD_DK, (hh * n_slot + j + 1) * HEAD_DK)
            src = slice((j * N_HEADS + hh) * HEAD_DK, (j * N_HEADS + hh + 1) * HEAD_DK)
            outs[0][:, dst] = ins[0][:, src].astype(BF16)
    for i_ref, o_ref in zip(ins[1:], outs[1:]):
        o_ref[...] = i_ref[...].astype(BF16)


def _cast_weights(ws):
    flat = [w.reshape(-1, w.shape[-1]) for w in ws]
    spec = lambda w: pl.BlockSpec((w.shape[0] // CAST_STEPS, w.shape[1]), lambda i: (i, 0))
    out = pl.pallas_call(
        functools.partial(_cast_weights_body, n=len(flat)),
        grid=(CAST_STEPS,),
        in_specs=[spec(w) for w in flat],
        out_specs=[spec(w) for w in flat],
        out_shape=[jax.ShapeDtypeStruct(w.shape, BF16) for w in flat],
        compiler_params=pltpu.CompilerParams(dimension_semantics=("arbitrary",),
                                             vmem_limit_bytes=VMEM_LIMIT_BYTES),
        name="cast_weights",
    )(*flat)
    return [o.reshape(w.shape) for o, w in zip(out, ws)]


def _proj_ffn_body(x_ref, a_ref, wo_ref, g_ref, wg_ref, wu_ref, wd_ref, gf_ref, o_ref, *, final_norm):
    x1 = x_ref[...] + jnp.dot(a_ref[...], wo_ref[...], preferred_element_type=F32)
    h = _rms(x1, g_ref[...]).astype(BF16)
    gate = jnp.dot(h, wg_ref[...], preferred_element_type=F32)
    up = jnp.dot(h, wu_ref[...], preferred_element_type=F32)
    t = (_silu(gate) * up).astype(BF16)
    x2 = x1 + jnp.dot(t, wd_ref[...], preferred_element_type=F32)
    if final_norm:
        x2 = _rms(x2, gf_ref[...])
    o_ref[...] = x2


def _proj_ffn(x, a, wo, g, wg, wu, wd, layer, gf, final_norm):
    m, d = x.shape
    ka = a.shape[1]
    ff = wg.shape[2]
    const = lambda shape: pl.BlockSpec(shape, lambda i: (0, 0), pipeline_mode=pl.Buffered(1))
    of_layer = lambda shape: pl.BlockSpec((None,) + shape, lambda i: (layer, 0, 0), pipeline_mode=pl.Buffered(1))
    return pl.pallas_call(
        functools.partial(_proj_ffn_body, final_norm=final_norm),
        grid=(m // ROW_TILE,),
        in_specs=[pl.BlockSpec((ROW_TILE, d), lambda i: (i, 0)),
                  pl.BlockSpec((ROW_TILE, ka), lambda i: (i, 0)),
                  const((ka, d)), const((1, d)), of_layer((d, ff)), of_layer((d, ff)), of_layer((ff, d)),
                  const((1, d))],
        out_specs=pl.BlockSpec((ROW_TILE, d), lambda i: (i, 0)),
        out_shape=jax.ShapeDtypeStruct((m, d), F32),
        compiler_params=pltpu.CompilerParams(dimension_semantics=("arbitrary",),
                                             vmem_limit_bytes=VMEM_LIMIT_BYTES),
        name="proj_ffn",
    )(x, a, wo, g.reshape(1, d), wg, wu, wd, gf.reshape(1, d))


def _hgrn_consts(L):
    r = np.arange(L)[:, None]
    t = np.arange(L)[None, :]
    sels = [t <= r]
    masks = [r == t]
    h = L // 2
    while h >= 1:
        base = (r // (2 * h)) * (2 * h)
        upper = (r % (2 * h)) >= h
        if h < SUBLANES:
            sels.append(np.where(upper, (t >= base + h) & (t <= r), (t > r) & (t <= base + h - 1)))
        masks.append((r // (2 * h) == t // (2 * h)) & upper & ((t % (2 * h)) < h))
        h //= 2
    return np.concatenate(sels, axis=0).astype(np.float32), np.stack(masks).astype(np.float32)


def _decay_exponents(sel_ref, logf, L):
    blk = min(SEL_BLOCK, L)
    nb = L // blk
    res = [_exact_dot(sel_ref[...], logf[i * blk:(i + 1) * blk]) for i in range(nb)]
    cum = [r[0:blk] for r in res]
    pre = [jnp.zeros_like(cum[0][0:1])]
    for i in range(nb):
        pre.append(pre[i] + cum[i][blk - 1:blk])
    cat = lambda parts: jnp.concatenate(parts, axis=0) if len(parts) > 1 else parts[0]
    b = cat([cum[i] + pre[i] for i in range(nb)])
    levels = []
    nbh = nb // 2
    while nbh >= 1:
        parts = []
        for i in range(nb):
            ref = pre[(i // (2 * nbh)) * 2 * nbh + nbh]
            if (i % (2 * nbh)) >= nbh:
                parts.append(cum[i] + (pre[i] - ref))
            else:
                parts.append((ref - pre[i]) - cum[i])
        levels.append(cat(parts))
        nbh //= 2
    h = blk // 2
    while h >= SUBLANES:
        parts = []
        for i in range(nb):
            for base in range(0, blk, 2 * h):
                ref = cum[i][base + h - 1:base + h]
                parts += [ref - cum[i][base:base + h], cum[i][base + h:base + 2 * h] - ref]
        levels.append(cat(parts))
        h //= 2
    for lev in range(sel_ref.shape[0] // blk - 1):
        levels.append(cat([r[(lev + 1) * blk:(lev + 2) * blk] for r in res]))
    return b, pre[nb], levels


def _retention_consts(L):
    lg = np.log1p(-np.exp2(-5.0 - np.arange(N_HEADS, dtype=np.float32))).astype(np.float32)
    idx = np.arange(L, dtype=np.float32)
    rel = idx[:, None] - idx[None, :]
    dmat = np.where(rel >= 0, np.exp(lg[:, None, None] * np.maximum(rel, 0.0)), 0.0)
    w_in = np.exp(lg[:, None] * (idx + 1.0))
    w_tail = np.exp(lg[:, None] * (L - 1.0 - idx))
    bcast = lambda v: jnp.asarray(np.broadcast_to(v[:, :, None], (N_HEADS, L, HEAD_DK)).astype(np.float32))
    g_chunk = [float(v) for v in np.exp(lg * L).astype(np.float32)]
    return jnp.asarray(dmat.astype(np.float32)), bcast(w_in), bcast(w_tail), g_chunk


def _rope_tables(pos0, T):
    half = HEAD_DK // 2
    inv = ROPE_BASE ** (-jnp.arange(half, dtype=F32) / half)
    ang = (jnp.arange(T) + pos0).astype(F32)[:, None] * inv[None, :]
    cos, sin = jnp.cos(ang), jnp.sin(ang)
    return jnp.concatenate([cos, cos], axis=-1), jnp.concatenate([-sin, sin], axis=-1)


def _hgrn_intra(units, mask_ref, L):
    sub = min(L, HGRN_SUB)
    n_sub = L // sub
    n_lev = mask_ref.shape[0] - 1
    top = len(units[0][3]) - n_lev
    assert n_sub in (1, 2) and top == n_sub - 1
    pieces = [(u, slice(i * sub, (i + 1) * sub)) for u in range(len(units)) for i in range(n_sub)]
    atts = [mask_ref[0] * _dot_nt(units[u][0][r], units[u][1][r]) for u, r in pieces]
    for lev in range(n_lev):
        for p, (u, r) in enumerate(pieces):
            q, k, _, expo = units[u]
            sc = jnp.exp(expo[top + lev][r])
            atts[p] = atts[p] + mask_ref[lev + 1] * _dot_nt(q[r] * sc, k[r] * sc)
    if n_sub == 1:
        return [_dot(att, units[u][2]) for att, (u, _) in zip(atts, pieces)]
    outs = []
    for u, (q, k, v, expo) in enumerate(units):
        cross = _dot_nt(q[sub:] * jnp.exp(expo[0][sub:]), k[:sub] * jnp.exp(expo[0][:sub]))
        outs.append(jnp.concatenate([_dot(atts[2 * u], v[:sub]),
                                     _dot(jnp.concatenate([cross, atts[2 * u + 1]], axis=1), v)], axis=0))
    return outs


def _mixer0_body(*refs, L, layer_slot, g_chunk, fused, n_side, side_every):
    it = iter(refs)
    take = lambda n: [next(it) for _ in range(n)]
    src = take(3 if fused else 1)
    (cos_ref, sin_ref, lbl_ref, ga_ref, gb_ref, sel_ref, mask_ref, dmat_ref, win_ref, wtail_ref,
     sa0_ref, sb0_ref) = take(12)
    side_in = take(n_side)
    mix_ref, sa_ref, sb_ref = take(3)
    side_out = take(n_side)
    sat_scr, sbt_scr = take(2)
    c = pl.program_id(1)
    last = pl.num_programs(1) - 1
    head_w = IN0_WIDTH // N_HEADS
    G = sa_ref.shape[0]

    if n_side:
        @pl.when((pl.program_id(0) * pl.num_programs(1) + c) % side_every == 0)
        def _():
            for i_ref, o_ref in zip(side_in, side_out):
                o_ref[...] = i_ref[...].astype(BF16)

    @pl.when(c == 0)
    def _():
        for gi in range(G):
            for hh in range(N_HEADS):
                sat_scr[gi * N_HEADS + hh] = sa0_ref[gi, hh].T
                sbt_scr[gi * N_HEADS + hh] = sb0_ref[gi, hh].T

    lbl = lbl_ref[...]
    e = jnp.exp(lbl - jnp.max(lbl, axis=0, keepdims=True))
    lb_all = jnp.sum(e[:layer_slot + 1], axis=0, keepdims=True) / jnp.sum(e, axis=0, keepdims=True)

    if fused:
        x_ref, g_ref, w_ref = src
        hb = _rms(x_ref[...], g_ref[...]).astype(BF16)
    cosf = cos_ref[...]
    sinf = sin_ref[...]

    def project(hh):
        hcols = slice(hh * head_w, (hh + 1) * head_w)
        return jnp.dot(hb, w_ref[:, hcols], preferred_element_type=F32) if fused else src[0][:, hcols]

    hg_units, ret_units = [], []
    z_next = project(0)
    for hh in range(N_HEADS):
        hs = slice(hh * HEAD_DK, (hh + 1) * HEAD_DK)
        z_head = z_next
        if hh + 1 < N_HEADS:
            z_next = project(hh + 1)
        lb = lb_all[:, hs]
        for gi in range(G):
            rs = slice(gi * L, (gi + 1) * L)
            si = gi * N_HEADS + hh
            zh = z_head[rs]
            part = lambda j: zh[:, j * HEAD_DK:(j + 1) * HEAD_DK]
            f = lb + (1.0 - lb) * _sigmoid(part(1))
            b, b_last, expo = _decay_exponents(sel_ref, jnp.log(f), L)
            hg_units.append((rs, hs, si, part(0), 1.0 - f, part(2), part(3), b, b_last, expo))
            rq = part(4)
            rk = part(5)
            rq = rq * cosf + pltpu.roll(rq, HEAD_DK // 2, 1) * sinf
            rk = (rk * cosf + pltpu.roll(rk, HEAD_DK // 2, 1) * sinf) * (HEAD_DK ** -0.5)
            ret_units.append((rs, hh, si, rq, rk, part(6), part(7)))

    states = [sat_scr[u[2]] for u in hg_units]
    inter = [_dot_nt(u[3] * jnp.exp(u[7]), st) for u, st in zip(hg_units, states)]
    intra = _hgrn_intra([(u[3], u[4], u[5], u[9]) for u in hg_units], mask_ref, L)
    for (rs, hs, si, q, k, v, og, b, b_last, _), st, o1, o2 in zip(hg_units, states, inter, intra):
        sat_scr[si] = st * jnp.exp(b_last) + _dot_tn(v, k * jnp.exp(b_last - b))
        o = _rms((o1 + o2) * _sigmoid(og), ga_ref[...])
        mix_ref[rs, hs] = o.astype(mix_ref.dtype)

    states = [sbt_scr[si] for _, _, si, _, _, _, _ in ret_units]
    scores = [_dot_nt(rq, rk) * dmat_ref[hh] for _, hh, _, rq, rk, _, _ in ret_units]
    inter = [_dot_nt(u[3], rt) for u, rt in zip(ret_units, states)]
    outs = [_dot(sc, u[5]) + win_ref[u[1]] * it_ for u, sc, it_ in zip(ret_units, scores, inter)]
    for (rs, hh, si, rq, rk, rv, rg), rt, ro in zip(ret_units, states, outs):
        sbt_scr[si] = g_chunk[hh] * rt + _dot_tn(rv, rk * wtail_ref[hh])
        ro = _rms(ro, gb_ref[...]) * _silu(rg)
        mix_ref[rs, KW + hh * HEAD_DK:KW + (hh + 1) * HEAD_DK] = ro.astype(mix_ref.dtype)

    @pl.when(c == last)
    def _():
        for gi in range(G):
            for hh in range(N_HEADS):
                sa_ref[gi, hh] = sat_scr[gi * N_HEADS + hh].T
                sb_ref[gi, hh] = sbt_scr[gi * N_HEADS + hh].T


def _mixer0(src, n_streams, T, pos0, sa0, sb0, lb_logits, ga, gb, layer_slot, side_cast=()):
    L = min(CHUNK0, T)
    nc = T // L
    G = max(1, min(n_streams, STEP_ROWS0 // L)) if nc == 1 else 1
    fused = len(src) == 3
    sel = jnp.asarray(_hgrn_consts(min(SEL_BLOCK, L))[0], BF16)
    masks = jnp.asarray(_hgrn_consts(min(HGRN_SUB, L))[1])
    dmat, w_in, w_tail, g_chunk = _retention_consts(L)
    cosf, sinf = _rope_tables(pos0, T)
    full = lambda a: pl.BlockSpec(a.shape, lambda s, c: (0,) * a.ndim, pipeline_mode=pl.Buffered(1))
    rows = lambda a: pl.BlockSpec((G * L, a.shape[1]), lambda s, c: (s * nc + c, 0))
    state_spec = pl.BlockSpec((G, N_HEADS, HEAD_DK, HEAD_DK), lambda s, c: (s, 0, 0, 0))
    state_shape = jax.ShapeDtypeStruct((n_streams, N_HEADS, HEAD_DK, HEAD_DK), F32)
    ga2, gb2 = ga.reshape(1, HEAD_DK), gb.reshape(1, HEAD_DK)
    if fused:
        src = (src[0], src[1].reshape(1, -1), src[2])
        src_specs = [rows(src[0]), full(src[1]), full(src[2])]
    else:
        src_specs = [rows(src[0])]
    consts = (lb_logits, ga2, gb2, sel, masks, dmat, w_in, w_tail)
    side = [w.reshape(-1, w.shape[-1]) for w in side_cast]
    per_block = (n_streams // G) * nc // SIDE_BLOCKS if side else 1
    side_spec = lambda w: pl.BlockSpec((w.shape[0] // SIDE_BLOCKS, w.shape[1]),
                                       lambda s, c: ((s * nc + c) // per_block, 0))
    outs = pl.pallas_call(
        functools.partial(_mixer0_body, L=L, layer_slot=layer_slot, g_chunk=g_chunk, fused=fused,
                          n_side=len(side), side_every=per_block),
        grid=(n_streams // G, nc),
        in_specs=src_specs
                 + [pl.BlockSpec((L, HEAD_DK), lambda s, c: (c, 0)), pl.BlockSpec((L, HEAD_DK), lambda s, c: (c, 0))]
                 + [full(a) for a in consts] + [state_spec, state_spec] + [side_spec(w) for w in side],
        out_specs=[pl.BlockSpec((G * L, 2 * KW), lambda s, c: (s * nc + c, 0)), state_spec, state_spec]
                  + [side_spec(w) for w in side],
        out_shape=[jax.ShapeDtypeStruct((n_streams * T, 2 * KW), BF16), state_shape, state_shape]
                  + [jax.ShapeDtypeStruct(w.shape, BF16) for w in side],
        scratch_shapes=[pltpu.VMEM((G * N_HEADS, HEAD_DK, HEAD_DK), F32),
                        pltpu.VMEM((G * N_HEADS, HEAD_DK, HEAD_DK), F32)],
        compiler_params=pltpu.CompilerParams(dimension_semantics=("arbitrary", "arbitrary"),
                                             vmem_limit_bytes=VMEM_LIMIT_BYTES),
        name="mixer0_T%d" % T,
    )(*src, cosf, sinf, *consts, sa0, sb0, *side)
    return outs[0], outs[1], outs[2], [o.reshape(w.shape) for o, w in zip(outs[3:], side_cast)]


def _mixer1_body(*refs, L, fused, zero_init, single_chunk):
    it = iter(refs)
    take = lambda n: [next(it) for _ in range(n)]
    src = take(3 if fused else 1)
    (cw_ref, cb_ref, wq_ref, wk_ref, wv_ref, wgate_ref, bgate_ref, tri_ref, gn_ref, skip_ref) = take(10)
    init = None if zero_init else take(4)
    hg_ref, c_ref, n_ref, m_ref, conv_ref = take(5)
    xbuf = take(1)[0]
    c = pl.program_id(1)
    last = pl.num_programs(1) - 1
    n_taps = C_CONV - 1
    G = c_ref.shape[0]
    kv_state = zero_init

    @pl.when(c == 0)
    def _():
        if zero_init:
            c_ref[...] = jnp.zeros_like(c_ref)
            n_ref[...] = jnp.zeros_like(n_ref)
            m_ref[...] = jnp.zeros_like(m_ref)
            xbuf[:, 0:CONV_HDR, :] = jnp.zeros((G, CONV_HDR, C_INNER), F32)
        else:
            c0_ref, n0_ref, m0_ref, conv0_ref = init
            if not single_chunk:
                c_ref[...] = c0_ref[...]
                n_ref[...] = n0_ref[...]
                m_ref[...] = m0_ref[...]
            xbuf[:, CONV_HDR - n_taps:CONV_HDR, :] = conv0_ref[...]

    c_in, n_in, m_in = init[:3] if single_chunk and not zero_init else (c_ref, n_ref, m_ref)

    if fused:
        x_ref, g_ref, w_ref = src
        hb = _rms(x_ref[...], g_ref[...]).astype(BF16)
        xm = jnp.dot(hb, w_ref[:, :C_INNER], preferred_element_type=F32)
        zg = jnp.dot(hb, w_ref[:, C_INNER:], preferred_element_type=F32)
    else:
        xm = src[0][:, :C_INNER]
        zg = src[0][:, C_INNER:]

    xcs, tails = [], []
    for gi in range(G):
        xg = xm[gi * L:(gi + 1) * L]
        xbuf[gi, CONV_HDR:CONV_HDR + L, :] = xg
        acc = cb_ref[...] + xg * cw_ref[n_taps:n_taps + 1, :]
        for w in range(n_taps):
            acc = acc + xbuf[gi, CONV_HDR - n_taps + w:CONV_HDR - n_taps + w + L, :] * cw_ref[w:w + 1, :]
        tails.append(xbuf[gi, CONV_HDR + L - n_taps:CONV_HDR + L, :])
        xbuf[gi, CONV_HDR - n_taps:CONV_HDR, :] = tails[gi]
        xcs.append(acc)
    xc = _silu(jnp.concatenate(xcs, axis=0) if G > 1 else xcs[0])

    xcb = xc.astype(BF16)
    xmb = xm.astype(BF16)
    qs, ks, vs = [], [], []
    for g in range(C_INNER // BD_GROUP):
        gs = slice(g * BD_GROUP, (g + 1) * BD_GROUP)
        qs.append(jnp.dot(xcb[:, gs], wq_ref[g], preferred_element_type=F32))
        ks.append(jnp.dot(xcb[:, gs], wk_ref[g], preferred_element_type=F32))
        vs.append(jnp.dot(xmb[:, gs], wv_ref[g], preferred_element_type=F32))
    q = jnp.concatenate(qs, axis=1)
    k = jnp.concatenate(ks, axis=1) * (C_DH ** -0.5)
    v = jnp.concatenate(vs, axis=1)

    gates = (_dot(q, wgate_ref[0:C_INNER, :]) + _dot(k, wgate_ref[C_INNER:2 * C_INNER, :])
             + _dot(v, wgate_ref[2 * C_INNER:3 * C_INNER, :]) + bgate_ref[...])
    bcum = _exact_dot(tri_ref[...], _log_sigmoid(gates))
    lane = lax.broadcasted_iota(jnp.int32, gates.shape, 1)
    rows = jnp.where(lane < N_HEADS, gates, bcum).T
    ti = lax.broadcasted_iota(jnp.int32, (L, L), 0)
    si = lax.broadcasted_iota(jnp.int32, (L, L), 1)
    causal = si <= ti

    units = []
    for gi, hh in [(gi, hh) for gi in range(G) for hh in range(N_HEADS)]:
        hs = slice(hh * C_DH, (hh + 1) * C_DH)
        rs = slice(gi * L, (gi + 1) * L)
        b_col = bcum[rs, N_HEADS + hh:N_HEADS + hh + 1]
        i_col = gates[rs, hh:hh + 1]
        b_row = rows[N_HEADS + hh:N_HEADS + hh + 1, rs]
        i_row = rows[hh:hh + 1, rs]
        m_prev = m_in[gi, :, hh:hh + 1]
        lw = jnp.where(causal, b_col - b_row + i_row, -jnp.inf)
        lp = b_col + m_prev
        mj = jnp.maximum(lp, jnp.max(lw, axis=-1, keepdims=True))
        m_new = mj[L - 1:L, :]
        b_last = b_col[L - 1:L, :]
        units.append(dict(
            gi=gi, hh=hh, hs=hs, rs=rs, mj=mj, m_new=m_new, wgt=jnp.exp(lw - mj), wp=jnp.exp(lp - mj),
            ws=jnp.exp(b_last - b_col + i_col - m_new),
            wpl=jnp.exp(b_last + m_prev - m_new),
            cm=c_in[gi, hh],
            nv=n_in[gi, hh:hh + 1, :],
            q=q[rs, hs], k=k[rs, hs], v=v[rs, hs]))

    scores = [_dot_nt(u['q'], u['k']) * u['wgt'] for u in units]
    inter = [_dot(u['q'], u['cm']) if kv_state else _dot_nt(u['q'], u['cm']) for u in units]
    cells = []
    for u, s, it_ in zip(units, scores, inter):
        num = _dot(s, u['v']) + u['wp'] * it_
        den = jnp.sum(s, axis=-1, keepdims=True) + u['wp'] * jnp.sum(u['q'] * u['nv'], axis=-1, keepdims=True)
        cells.append(num / jnp.maximum(jnp.abs(den), jnp.exp(-u['mj'])))
    for u in units:
        gi, hh, vw = u['gi'], u['hh'], u['v'] * u['ws']
        c_ref[gi, hh] = u['wpl'] * u['cm'] + (_dot_tn(u['k'], vw) if kv_state else _dot_tn(vw, u['k']))
        n_ref[gi, hh:hh + 1, :] = u['wpl'] * u['nv'] + jnp.sum(u['k'] * u['ws'], axis=0, keepdims=True)
        m_ref[gi, :, hh:hh + 1] = u['m_new']
    for u, hcell in zip(units, cells):
        hs, rs = u['hs'], u['rs']
        hc = hcell - jnp.mean(hcell, axis=-1, keepdims=True)
        hn = hc * lax.rsqrt(jnp.mean(hc * hc, axis=-1, keepdims=True) + EPS) * gn_ref[...]
        out = (hn + skip_ref[:, hs] * xc[rs, hs]) * _silu(zg[rs, hs])
        hg_ref[rs, hs] = out.astype(hg_ref.dtype)

    @pl.when(c == last)
    def _():
        for gi in range(G):
            conv_ref[gi] = tails[gi]
            if kv_state:
                for hh in range(N_HEADS):
                    c_ref[gi, hh] = c_ref[gi, hh].T


def _mixer1(src, n_streams, T, states, cw, cb, wq, wk, wv, wgate, bgate, gn, skip):
    L = min(CHUNK1, T)
    nc = T // L
    G = max(1, min(n_streams, STEP_ROWS1 // L)) if nc == 1 else 1
    fused = len(src) == 3
    zero_init = states is None
    tri = jnp.asarray(np.kron(np.eye(G, dtype=np.float32), np.tril(np.ones((L, L), np.float32))), BF16)
    full = lambda a: pl.BlockSpec(a.shape, lambda s, c: (0,) * a.ndim, pipeline_mode=pl.Buffered(1))
    rows = lambda a: pl.BlockSpec((G * L, a.shape[1]), lambda s, c: (s * nc + c, 0))
    per_stream = lambda shape: pl.BlockSpec((G,) + shape[1:], lambda s, c: (s,) + (0,) * (len(shape) - 1))
    gn2, skip2, cb2 = gn.reshape(1, C_DH), skip.reshape(1, C_INNER), cb.reshape(1, C_INNER)
    state_shapes = ((n_streams, N_HEADS, C_DH, C_DH), (n_streams, N_HEADS, C_DH), (n_streams, 1, N_HEADS),
                    (n_streams, C_CONV - 1, C_INNER))
    scratch = [pltpu.VMEM((G, CONV_HDR + L, C_INNER), F32)]
    if fused:
        src = (src[0], src[1].reshape(1, -1), src[2])
        src_specs = [rows(src[0]), full(src[1]), full(src[2])]
    else:
        src_specs = [rows(src[0])]
    consts = (cw, cb2, wq, wk, wv, wgate, bgate, tri, gn2, skip2)
    if not zero_init:
        states = tuple(a.reshape(shape) for a, shape in zip(states, state_shapes))
    hg, c1, n1, m1, conv1 = pl.pallas_call(
        functools.partial(_mixer1_body, L=L, fused=fused, zero_init=zero_init, single_chunk=nc == 1),
        grid=(n_streams // G, nc),
        in_specs=src_specs + [full(a) for a in consts]
                 + ([] if zero_init else [per_stream(shape) for shape in state_shapes]),
        out_specs=[pl.BlockSpec((G * L, C_INNER), lambda s, c: (s * nc + c, 0))]
                  + [per_stream(shape) for shape in state_shapes],
        out_shape=[jax.ShapeDtypeStruct((n_streams * T, C_INNER), BF16)]
                  + [jax.ShapeDtypeStruct(shape, F32) for shape in state_shapes],
        scratch_shapes=scratch,
        compiler_params=pltpu.CompilerParams(dimension_semantics=("arbitrary", "arbitrary"),
                                             vmem_limit_bytes=VMEM_LIMIT_BYTES),
        name="mixer1_T%d" % T,
    )(*src, *consts, *(() if zero_init else states))
    return hg, c1, n1, m1.reshape(n_streams, N_HEADS), conv1


def _dense_blockdiag(w):
    rows = w.reshape(-1, BD_GROUP, C_BLOCK)
    tiled = jnp.tile(rows, (1, 1, BD_GROUP // C_BLOCK))
    idx = np.arange(BD_GROUP) // C_BLOCK
    same_block = jnp.asarray(idx[:, None] == idx[None, :])
    return jnp.where(same_block, tiled, 0.0).astype(BF16)


def kernel(x_prompt, x_sample, state_hgrn, state_ret, state_mlstm_c, state_mlstm_n, state_mlstm_m, state_conv,
           norm_mix, norm_ffn, norm_final, w_in0, lb_logits, hgrn_norm, ret_norm, w_out0,
           w_up1, conv_w, conv_b, w_q1, w_k1, w_v1, w_ig, b_ig, w_fg, b_fg, mlstm_norm, mlstm_skip, w_down1,
           w_ffn_gate, w_ffn_up, w_ffn_down):
    bp, tp, d = x_prompt.shape
    bs, ts, _ = x_sample.shape
    mp, ms = bp * tp, bs * ts
    past_len = 2048
    zeros = lambda *shape: jnp.zeros(shape, F32)
    w_in, = _cast_weights([w_in0[0]])
    later_weights = [w_out0[0], w_up1[0], w_down1[0], w_ffn_gate, w_ffn_up, w_ffn_down]

    xp = x_prompt.reshape(mp, d)
    xs = x_sample.reshape(ms, d)

    mix_p, hg_p, rt_p, (w_out, w_up, w_down, wf_gate, wf_up, wf_down) = _mixer0(
        (xp, norm_mix[0], w_in), bp, tp, 0, zeros(bp, N_HEADS, HEAD_DK, HEAD_DK),
        zeros(bp, N_HEADS, HEAD_DK, HEAD_DK), lb_logits, hgrn_norm[0], ret_norm[0], 0, side_cast=later_weights)
    ffn0 = (w_out, norm_ffn[0], wf_gate, wf_up, wf_down, 0, norm_final, False)
    mix_s, hg_s, rt_s, _ = _mixer0((_norm_matmul(xs, norm_mix[0], w_in),), bs, ts, past_len, state_hgrn[0],
                                   state_ret[0], lb_logits, hgrn_norm[0], ret_norm[0], 0)
    xp = _proj_ffn(xp, mix_p, *ffn0)
    xs = _proj_ffn(xs, mix_s, *ffn0)

    ffn1 = (w_down, norm_ffn[1], wf_gate, wf_up, wf_down, 1, norm_final, True)
    wgate = jnp.pad(jnp.concatenate([w_ig[0], w_fg[0]], axis=1), ((0, 0), (0, GATE_LANES - 2 * N_HEADS))).astype(BF16)
    bgate = jnp.pad(jnp.concatenate([b_ig[0], b_fg[0]]), (0, GATE_LANES - 2 * N_HEADS)).reshape(1, GATE_LANES)
    m1_consts = (conv_w[0], conv_b[0], _dense_blockdiag(w_q1[0]), _dense_blockdiag(w_k1[0]),
                 _dense_blockdiag(w_v1[0]), wgate, bgate, mlstm_norm[0], mlstm_skip[0])
    hg1_p, mc_p, mn_p, mm_p, cv_p = _mixer1((xp, norm_mix[1], w_up), bp, tp, None, *m1_consts)
    hg1_s, mc_s, mn_s, mm_s, cv_s = _mixer1((_norm_matmul(xs, norm_mix[1], w_up),), bs, ts,
                                            (state_mlstm_c[0], state_mlstm_n[0], state_mlstm_m[0], state_conv[0]),
                                            *m1_consts)
    yp = _proj_ffn(xp, hg1_p, *ffn1)
    ys = _proj_ffn(xs, hg1_s, *ffn1)

    lead = lambda a: a[None]
    return (yp.reshape(bp, tp, d), ys.reshape(bs, ts, d),
            lead(hg_p), lead(hg_s), lead(rt_p), lead(rt_s),
            lead(mc_p), lead(mc_s), lead(mn_p), lead(mn_s), lead(mm_p), lead(mm_s), lead(cv_p), lead(cv_s))
```

```python
import functools
import math

import numpy as np
import jax
import jax.numpy as jnp
from jax import lax
from jax.experimental import pallas as pl
from jax.experimental.pallas import tpu as pltpu

F32 = jnp.float32
BF16 = jnp.bfloat16

D_MODEL = 1024
CHUNK0 = 256
CHUNK1 = 256
STEP_ROWS0 = 128
STEP_ROWS1 = 32
SEL_BLOCK = 64
HGRN_SUB = 128
EPS = 1e-6
SUBLANES = 8
N_HEADS = 4
HEAD_DK = 128
KW = N_HEADS * HEAD_DK
IN0_WIDTH = 8 * KW
ROPE_BASE = 10000.0
C_INNER = 2 * D_MODEL
C_DH = C_INNER // N_HEADS
C_CONV = 4
C_BLOCK = 4
BD_GROUP = 256
D_FF = -(-8 * D_MODEL // (3 * 256)) * 256
GATE_LANES = 128
CONV_HDR = 8
VMEM_LIMIT_BYTES = 56 * 1024 * 1024
ROW_TILE = 512
CAST_STEPS = 16
SIDE_BLOCKS = 32


def _dot(a, b):
    return jnp.dot(a.astype(BF16), b.astype(BF16), preferred_element_type=F32)


def _dot_nt(a, b):
    return lax.dot_general(a.astype(BF16), b.astype(BF16), (((1,), (1,)), ((), ())),
                           preferred_element_type=F32)


def _dot_tn(a, b):
    return lax.dot_general(a.astype(BF16), b.astype(BF16), (((0,), (0,)), ((), ())),
                           preferred_element_type=F32)


def _exact_dot(sel, x):
    hi = x.astype(BF16)
    lo = (x - hi.astype(F32)).astype(BF16)
    n = x.shape[1]
    both = jnp.dot(sel, jnp.concatenate([hi, lo], axis=1), preferred_element_type=F32)
    return both[:, :n] + both[:, n:]


def _rms(x, g):
    return x * lax.rsqrt(jnp.mean(x * x, axis=-1, keepdims=True) + EPS) * g


def _sigmoid(x):
    return 1.0 / (1.0 + jnp.exp(-x))


def _silu(x):
    return x * _sigmoid(x)


def _log_sigmoid(x):
    return jnp.minimum(x, 0.0) - jnp.log(1.0 + jnp.exp(-jnp.abs(x)))


def _norm_matmul_body(x_ref, g_ref, w_ref, o_ref):
    h = _rms(x_ref[...], g_ref[...])
    o_ref[...] = jnp.dot(h.astype(BF16), w_ref[...], preferred_element_type=F32)


def _norm_matmul(x, g, w):
    m, d = x.shape
    n = w.shape[1]
    return pl.pallas_call(
        _norm_matmul_body,
        grid=(m // ROW_TILE,),
        in_specs=[pl.BlockSpec((ROW_TILE, d), lambda i: (i, 0)),
                  pl.BlockSpec((1, d), lambda i: (0, 0)),
                  pl.BlockSpec((d, n), lambda i: (0, 0), pipeline_mode=pl.Buffered(1))],
        out_specs=pl.BlockSpec((ROW_TILE, n), lambda i: (i, 0)),
        out_shape=jax.ShapeDtypeStruct((m, n), F32),
        compiler_params=pltpu.CompilerParams(dimension_semantics=("arbitrary",),
                                             vmem_limit_bytes=VMEM_LIMIT_BYTES),
        name="norm_matmul",
    )(x, g.reshape(1, d), w)


def _cast_weights_body(*refs, n):
    ins, outs = refs[:n], refs[n:]
    n_slot = IN0_WIDTH // KW
    for hh in range(N_HEADS):
        for j in range(n_slot):
            dst = slice((hh * n_slot + j) * HEAD_DK, (hh * n_slot + j + 1) * HEAD_DK)
            src = slice((j * N_HEADS + hh) * HEAD_DK, (j * N_HEADS + hh + 1) * HEAD_DK)
            outs[0][:, dst] = ins[0][:, src].astype(BF16)
    for i_ref, o_ref in zip(ins[1:], outs[1:]):
        o_ref[...] = i_ref[...].astype(BF16)


def _cast_weights(ws):
    flat = [w.reshape(-1, w.shape[-1]) for w in ws]
    spec = lambda w: pl.BlockSpec((w.shape[0] // CAST_STEPS, w.shape[1]), lambda i: (i, 0))
    out = pl.pallas_call(
        functools.partial(_cast_weights_body, n=len(flat)),
        grid=(CAST_STEPS,),
        in_specs=[spec(w) for w in flat],
        out_specs=[spec(w) for w in flat],
        out_shape=[jax.ShapeDtypeStruct(w.shape, BF16) for w in flat],
        compiler_params=pltpu.CompilerParams(dimension_semantics=("arbitrary",),
                                             vmem_limit_bytes=VMEM_LIMIT_BYTES),
        name="cast_weights",
    )(*flat)
    return [o.reshape(w.shape) for o, w in zip(out, ws)]


def _proj_ffn_body(x_ref, a_ref, wo_ref, g_ref, wg_ref, wu_ref, wd_ref, gf_ref, o_ref, *, final_norm):
    x1 = x_ref[...] + jnp.dot(a_ref[...], wo_ref[...], preferred_element_type=F32)
    h = _rms(x1, g_ref[...]).astype(BF16)
    gate = jnp.dot(h, wg_ref[...], preferred_element_type=F32)
    up = jnp.dot(h, wu_ref[...], preferred_element_type=F32)
    t = (_silu(gate) * up).astype(BF16)
    x2 = x1 + jnp.dot(t, wd_ref[...], preferred_element_type=F32)
    if final_norm:
        x2 = _rms(x2, gf_ref[...])
    o_ref[...] = x2


def _proj_ffn_pair_body(xa_ref, aa_ref, xb_ref, ab_ref, wo_ref, g_ref, wg_ref, wu_ref, wd_ref, gf_ref,
                        oa_ref, ob_ref, *, final_norm, n_a):
    i = pl.program_id(0)
    shared = (wo_ref, g_ref, wg_ref, wu_ref, wd_ref, gf_ref)

    @pl.when(i < n_a)
    def _():
        _proj_ffn_body(xa_ref, aa_ref, *shared, oa_ref, final_norm=final_norm)

    @pl.when(i >= n_a)
    def _():
        _proj_ffn_body(xb_ref, ab_ref, *shared, ob_ref, final_norm=final_norm)


def _proj_ffn(xa, aa, xb, ab, wo, g, wg, wu, wd, layer, gf, final_norm):
    d = xa.shape[1]
    ka = aa.shape[1]
    ff = wg.shape[2]
    n_a, n_b = xa.shape[0] // ROW_TILE, xb.shape[0] // ROW_TILE
    const = lambda shape: pl.BlockSpec(shape, lambda i: (0, 0), pipeline_mode=pl.Buffered(1))
    of_layer = lambda shape: pl.BlockSpec((None,) + shape, lambda i: (layer, 0, 0), pipeline_mode=pl.Buffered(1))
    rows_a = lambda w: pl.BlockSpec((ROW_TILE, w), lambda i: (jnp.minimum(i, n_a - 1), 0))
    rows_b = lambda w: pl.BlockSpec((ROW_TILE, w), lambda i: (jnp.maximum(i - n_a, 0), 0))
    return pl.pallas_call(
        functools.partial(_proj_ffn_pair_body, final_norm=final_norm, n_a=n_a),
        grid=(n_a + n_b,),
        in_specs=[rows_a(d), rows_a(ka), rows_b(d), rows_b(ka),
                  const((ka, d)), const((1, d)), of_layer((d, ff)), of_layer((d, ff)), of_layer((ff, d)),
                  const((1, d))],
        out_specs=[rows_a(d), rows_b(d)],
        out_shape=[jax.ShapeDtypeStruct(xa.shape, F32), jax.ShapeDtypeStruct(xb.shape, F32)],
        compiler_params=pltpu.CompilerParams(dimension_semantics=("arbitrary",),
                                             vmem_limit_bytes=VMEM_LIMIT_BYTES),
        name="proj_ffn",
    )(xa, aa, xb, ab, wo, g.reshape(1, d), wg, wu, wd, gf.reshape(1, d))


def _hgrn_consts(L):
    r = np.arange(L)[:, None]
    t = np.arange(L)[None, :]
    sels = [t <= r]
    masks = [r == t]
    h = L // 2
    while h >= 1:
        base = (r // (2 * h)) * (2 * h)
        upper = (r % (2 * h)) >= h
        if h < SUBLANES:
            sels.append(np.where(upper, (t >= base + h) & (t <= r), (t > r) & (t <= base + h - 1)))
        masks.append((r // (2 * h) == t // (2 * h)) & upper & ((t % (2 * h)) < h))
        h //= 2
    return np.concatenate(sels, axis=0).astype(np.float32), np.stack(masks).astype(np.float32)


def _decay_exponents(sel_ref, logf, L):
    blk = min(SEL_BLOCK, L)
    nb = L // blk
    res = [_exact_dot(sel_ref[...], logf[i * blk:(i + 1) * blk]) for i in range(nb)]
    cum = [r[0:blk] for r in res]
    pre = [jnp.zeros_like(cum[0][0:1])]
    for i in range(nb):
        pre.append(pre[i] + cum[i][blk - 1:blk])
    cat = lambda parts: jnp.concatenate(parts, axis=0) if len(parts) > 1 else parts[0]
    b = cat([cum[i] + pre[i] for i in range(nb)])
    levels = []
    nbh = nb // 2
    while nbh >= 1:
        parts = []
        for i in range(nb):
            ref = pre[(i // (2 * nbh)) * 2 * nbh + nbh]
            if (i % (2 * nbh)) >= nbh:
                parts.append(cum[i] + (pre[i] - ref))
            else:
                parts.append((ref - pre[i]) - cum[i])
        levels.append(cat(parts))
        nbh //= 2
    h = blk // 2
    while h >= SUBLANES:
        parts = []
        for i in range(nb):
            for base in range(0, blk, 2 * h):
                ref = cum[i][base + h - 1:base + h]
                parts += [ref - cum[i][base:base + h], cum[i][base + h:base + 2 * h] - ref]
        levels.append(cat(parts))
        h //= 2
    for lev in range(sel_ref.shape[0] // blk - 1):
        levels.append(cat([r[(lev + 1) * blk:(lev + 2) * blk] for r in res]))
    return b, pre[nb], levels


def _retention_consts(L):
    lg = np.log1p(-np.exp2(-5.0 - np.arange(N_HEADS, dtype=np.float32))).astype(np.float32)
    idx = np.arange(L, dtype=np.float32)
    rel = idx[:, None] - idx[None, :]
    dmat = np.where(rel >= 0, np.exp(lg[:, None, None] * np.maximum(rel, 0.0)), 0.0)
    w_in = np.exp(lg[:, None] * (idx + 1.0))
    w_tail = np.exp(lg[:, None] * (L - 1.0 - idx))
    bcast = lambda v: jnp.asarray(np.broadcast_to(v[:, :, None], (N_HEADS, L, HEAD_DK)).astype(np.float32))
    g_chunk = [float(v) for v in np.exp(lg * L).astype(np.float32)]
    return jnp.asarray(dmat.astype(np.float32)), bcast(w_in), bcast(w_tail), g_chunk


def _rope_tables(pos0, T):
    half = HEAD_DK // 2
    inv = ROPE_BASE ** (-jnp.arange(half, dtype=F32) / half)
    ang = (jnp.arange(T) + pos0).astype(F32)[:, None] * inv[None, :]
    cos, sin = jnp.cos(ang), jnp.sin(ang)
    return jnp.concatenate([cos, cos], axis=-1), jnp.concatenate([-sin, sin], axis=-1)


def _hgrn_intra(units, mask_ref, L):
    sub = min(L, HGRN_SUB)
    n_sub = L // sub
    n_lev = mask_ref.shape[0] - 1
    top = len(units[0][3]) - n_lev
    assert n_sub in (1, 2) and top == n_sub - 1
    pieces = [(u, slice(i * sub, (i + 1) * sub)) for u in range(len(units)) for i in range(n_sub)]
    atts = [mask_ref[0] * _dot_nt(units[u][0][r], units[u][1][r]) for u, r in pieces]
    for lev in range(n_lev):
        for p, (u, r) in enumerate(pieces):
            q, k, _, expo = units[u]
            sc = jnp.exp(expo[top + lev][r])
            atts[p] = atts[p] + mask_ref[lev + 1] * _dot_nt(q[r] * sc, k[r] * sc)
    if n_sub == 1:
        return [_dot(att, units[u][2]) for att, (u, _) in zip(atts, pieces)]
    outs = []
    for u, (q, k, v, expo) in enumerate(units):
        cross = _dot_nt(q[sub:] * jnp.exp(expo[0][sub:]), k[:sub] * jnp.exp(expo[0][:sub]))
        outs.append(jnp.concatenate([_dot(atts[2 * u], v[:sub]),
                                     _dot(jnp.concatenate([cross, atts[2 * u + 1]], axis=1), v)], axis=0))
    return outs


def _mixer0_body(*refs, L, layer_slot, g_chunk, fused, n_side, side_every):
    it = iter(refs)
    take = lambda n: [next(it) for _ in range(n)]
    src = take(3 if fused else 1)
    (cos_ref, sin_ref, lbl_ref, ga_ref, gb_ref, sel_ref, mask_ref, dmat_ref, win_ref, wtail_ref,
     sa0_ref, sb0_ref) = take(12)
    side_in = take(n_side)
    mix_ref, sa_ref, sb_ref = take(3)
    side_out = take(n_side)
    sat_scr, sbt_scr = take(2)
    c = pl.program_id(1)
    last = pl.num_programs(1) - 1
    head_w = IN0_WIDTH // N_HEADS
    G = sa_ref.shape[0]

    if n_side:
        @pl.when((pl.program_id(0) * pl.num_programs(1) + c) % side_every == 0)
        def _():
            for i_ref, o_ref in zip(side_in, side_out):
                o_ref[...] = i_ref[...].astype(BF16)

    @pl.when(c == 0)
    def _():
        for gi in range(G):
            for hh in range(N_HEADS):
                sat_scr[gi * N_HEADS + hh] = sa0_ref[gi, hh].T
                sbt_scr[gi * N_HEADS + hh] = sb0_ref[gi, hh].T

    lbl = lbl_ref[...]
    e = jnp.exp(lbl - jnp.max(lbl, axis=0, keepdims=True))
    lb_all = jnp.sum(e[:layer_slot + 1], axis=0, keepdims=True) / jnp.sum(e, axis=0, keepdims=True)

    if fused:
        x_ref, g_ref, w_ref = src
        hb = _rms(x_ref[...], g_ref[...]).astype(BF16)
    cosf = cos_ref[...]
    sinf = sin_ref[...]

    def project(hh):
        hcols = slice(hh * head_w, (hh + 1) * head_w)
        return jnp.dot(hb, w_ref[:, hcols], preferred_element_type=F32) if fused else src[0][:, hcols]

    hg_units, ret_units = [], []
    z_next = project(0)
    for hh in range(N_HEADS):
        hs = slice(hh * HEAD_DK, (hh + 1) * HEAD_DK)
        z_head = z_next
        if hh + 1 < N_HEADS:
            z_next = project(hh + 1)
        lb = lb_all[:, hs]
        for gi in range(G):
            rs = slice(gi * L, (gi + 1) * L)
            si = gi * N_HEADS + hh
            zh = z_head[rs]
            part = lambda j: zh[:, j * HEAD_DK:(j + 1) * HEAD_DK]
            f = lb + (1.0 - lb) * _sigmoid(part(1))
            b, b_last, expo = _decay_exponents(sel_ref, jnp.log(f), L)
            hg_units.append((rs, hs, si, part(0), 1.0 - f, part(2), part(3), b, b_last, expo))
            rq = part(4)
            rk = part(5)
            rq = rq * cosf + pltpu.roll(rq, HEAD_DK // 2, 1) * sinf
            rk = (rk * cosf + pltpu.roll(rk, HEAD_DK // 2, 1) * sinf) * (HEAD_DK ** -0.5)
            ret_units.append((rs, hh, si, rq, rk, part(6), part(7)))

    states = [sat_scr[u[2]] for u in hg_units]
    inter = [_dot_nt(u[3] * jnp.exp(u[7]), st) for u, st in zip(hg_units, states)]
    intra = _hgrn_intra([(u[3], u[4], u[5], u[9]) for u in hg_units], mask_ref, L)
    for (rs, hs, si, q, k, v, og, b, b_last, _), st, o1, o2 in zip(hg_units, states, inter, intra):
        sat_scr[si] = st * jnp.exp(b_last) + _dot_tn(v, k * jnp.exp(b_last - b))
        o = _rms((o1 + o2) * _sigmoid(og), ga_ref[...])
        mix_ref[rs, hs] = o.astype(mix_ref.dtype)

    states = [sbt_scr[si] for _, _, si, _, _, _, _ in ret_units]
    scores = [_dot_nt(rq, rk) * dmat_ref[hh] for _, hh, _, rq, rk, _, _ in ret_units]
    inter = [_dot_nt(u[3], rt) for u, rt in zip(ret_units, states)]
    outs = [_dot(sc, u[5]) + win_ref[u[1]] * it_ for u, sc, it_ in zip(ret_units, scores, inter)]
    for (rs, hh, si, rq, rk, rv, rg), rt, ro in zip(ret_units, states, outs):
        sbt_scr[si] = g_chunk[hh] * rt + _dot_tn(rv, rk * wtail_ref[hh])
        ro = _rms(ro, gb_ref[...]) * _silu(rg)
        mix_ref[rs, KW + hh * HEAD_DK:KW + (hh + 1) * HEAD_DK] = ro.astype(mix_ref.dtype)

    @pl.when(c == last)
    def _():
        for gi in range(G):
            for hh in range(N_HEADS):
                sa_ref[gi, hh] = sat_scr[gi * N_HEADS + hh].T
                sb_ref[gi, hh] = sbt_scr[gi * N_HEADS + hh].T


def _mixer0(src, n_streams, T, pos0, sa0, sb0, lb_logits, ga, gb, layer_slot, side_cast=()):
    L = min(CHUNK0, T)
    nc = T // L
    G = max(1, min(n_streams, STEP_ROWS0 // L)) if nc == 1 else 1
    fused = len(src) == 3
    sel = jnp.asarray(_hgrn_consts(min(SEL_BLOCK, L))[0], BF16)
    masks = jnp.asarray(_hgrn_consts(min(HGRN_SUB, L))[1])
    dmat, w_in, w_tail, g_chunk = _retention_consts(L)
    cosf, sinf = _rope_tables(pos0, T)
    full = lambda a: pl.BlockSpec(a.shape, lambda s, c: (0,) * a.ndim, pipeline_mode=pl.Buffered(1))
    rows = lambda a: pl.BlockSpec((G * L, a.shape[1]), lambda s, c: (s * nc + c, 0))
    state_spec = pl.BlockSpec((G, N_HEADS, HEAD_DK, HEAD_DK), lambda s, c: (s, 0, 0, 0))
    state_shape = jax.ShapeDtypeStruct((n_streams, N_HEADS, HEAD_DK, HEAD_DK), F32)
    ga2, gb2 = ga.reshape(1, HEAD_DK), gb.reshape(1, HEAD_DK)
    if fused:
        src = (src[0], src[1].reshape(1, -1), src[2])
        src_specs = [rows(src[0]), full(src[1]), full(src[2])]
    else:
        src_specs = [rows(src[0])]
    consts = (lb_logits, ga2, gb2, sel, masks, dmat, w_in, w_tail)
    side = [w.reshape(-1, w.shape[-1]) for w in side_cast]
    per_block = (n_streams // G) * nc // SIDE_BLOCKS if side else 1
    side_spec = lambda w: pl.BlockSpec((w.shape[0] // SIDE_BLOCKS, w.shape[1]),
                                       lambda s, c: ((s * nc + c) // per_block, 0))
    outs = pl.pallas_call(
        functools.partial(_mixer0_body, L=L, layer_slot=layer_slot, g_chunk=g_chunk, fused=fused,
                          n_side=len(side), side_every=per_block),
        grid=(n_streams // G, nc),
        in_specs=src_specs
                 + [pl.BlockSpec((L, HEAD_DK), lambda s, c: (c, 0)), pl.BlockSpec((L, HEAD_DK), lambda s, c: (c, 0))]
                 + [full(a) for a in consts] + [state_spec, state_spec] + [side_spec(w) for w in side],
        out_specs=[pl.BlockSpec((G * L, 2 * KW), lambda s, c: (s * nc + c, 0)), state_spec, state_spec]
                  + [side_spec(w) for w in side],
        out_shape=[jax.ShapeDtypeStruct((n_streams * T, 2 * KW), BF16), state_shape, state_shape]
                  + [jax.ShapeDtypeStruct(w.shape, BF16) for w in side],
        scratch_shapes=[pltpu.VMEM((G * N_HEADS, HEAD_DK, HEAD_DK), F32),
                        pltpu.VMEM((G * N_HEADS, HEAD_DK, HEAD_DK), F32)],
        compiler_params=pltpu.CompilerParams(dimension_semantics=("arbitrary", "arbitrary"),
                                             vmem_limit_bytes=VMEM_LIMIT_BYTES),
        name="mixer0_T%d" % T,
    )(*src, cosf, sinf, *consts, sa0, sb0, *side)
    return outs[0], outs[1], outs[2], [o.reshape(w.shape) for o, w in zip(outs[3:], side_cast)]


def _mixer1_body(*refs, L, fused, zero_init, single_chunk):
    it = iter(refs)
    take = lambda n: [next(it) for _ in range(n)]
    src = take(3 if fused else 1)
    (cw_ref, cb_ref, wq_ref, wk_ref, wv_ref, wgate_ref, bgate_ref, tri_ref, gn_ref, skip_ref) = take(10)
    init = None if zero_init else take(4)
    hg_ref, c_ref, n_ref, m_ref, conv_ref = take(5)
    xbuf = take(1)[0]
    c = pl.program_id(1)
    last = pl.num_programs(1) - 1
    n_taps = C_CONV - 1
    G = c_ref.shape[0]
    kv_state = zero_init

    @pl.when(c == 0)
    def _():
        if zero_init:
            c_ref[...] = jnp.zeros_like(c_ref)
            n_ref[...] = jnp.zeros_like(n_ref)
            m_ref[...] = jnp.zeros_like(m_ref)
            xbuf[:, 0:CONV_HDR, :] = jnp.zeros((G, CONV_HDR, C_INNER), F32)
        else:
            c0_ref, n0_ref, m0_ref, conv0_ref = init
            if not single_chunk:
                c_ref[...] = c0_ref[...]
                n_ref[...] = n0_ref[...]
                m_ref[...] = m0_ref[...]
            xbuf[:, CONV_HDR - n_taps:CONV_HDR, :] = conv0_ref[...]

    c_in, n_in, m_in = init[:3] if single_chunk and not zero_init else (c_ref, n_ref, m_ref)

    if fused:
        x_ref, g_ref, w_ref = src
        hb = _rms(x_ref[...], g_ref[...]).astype(BF16)
        xm = jnp.dot(hb, w_ref[:, :C_INNER], preferred_element_type=F32)
        zg = jnp.dot(hb, w_ref[:, C_INNER:], preferred_element_type=F32)
    else:
        xm = src[0][:, :C_INNER]
        zg = src[0][:, C_INNER:]

    xcs, tails = [], []
    for gi in range(G):
        xg = xm[gi * L:(gi + 1) * L]
        xbuf[gi, CONV_HDR:CONV_HDR + L, :] = xg
        acc = cb_ref[...] + xg * cw_ref[n_taps:n_taps + 1, :]
        for w in range(n_taps):
            acc = acc + xbuf[gi, CONV_HDR - n_taps + w:CONV_HDR - n_taps + w + L, :] * cw_ref[w:w + 1, :]
        tails.append(xbuf[gi, CONV_HDR + L - n_taps:CONV_HDR + L, :])
        xbuf[gi, CONV_HDR - n_taps:CONV_HDR, :] = tails[gi]
        xcs.append(acc)
    xc = _silu(jnp.concatenate(xcs, axis=0) if G > 1 else xcs[0])

    xcb = xc.astype(BF16)
    xmb = xm.astype(BF16)
    qs, ks, vs = [], [], []
    for g in range(C_INNER // BD_GROUP):
        gs = slice(g * BD_GROUP, (g + 1) * BD_GROUP)
        qs.append(jnp.dot(xcb[:, gs], wq_ref[g], preferred_element_type=F32))
        ks.append(jnp.dot(xcb[:, gs], wk_ref[g], preferred_element_type=F32))
        vs.append(jnp.dot(xmb[:, gs], wv_ref[g], preferred_element_type=F32))
    q = jnp.concatenate(qs, axis=1).astype(BF16)
    k = (jnp.concatenate(ks, axis=1) * (C_DH ** -0.5)).astype(BF16)
    v = jnp.concatenate(vs, axis=1).astype(BF16)

    gates = (_dot(q, wgate_ref[0:C_INNER, :]) + _dot(k, wgate_ref[C_INNER:2 * C_INNER, :])
             + _dot(v, wgate_ref[2 * C_INNER:3 * C_INNER, :]) + bgate_ref[...])
    bcum = _exact_dot(tri_ref[...], _log_sigmoid(gates))
    lane = lax.broadcasted_iota(jnp.int32, gates.shape, 1)
    rows = jnp.where(lane < N_HEADS, gates, bcum).T
    ti = lax.broadcasted_iota(jnp.int32, (L, L), 0)
    si = lax.broadcasted_iota(jnp.int32, (L, L), 1)
    causal = si <= ti

    units = []
    for gi, hh in [(gi, hh) for gi in range(G) for hh in range(N_HEADS)]:
        hs = slice(hh * C_DH, (hh + 1) * C_DH)
        rs = slice(gi * L, (gi + 1) * L)
        b_col = bcum[rs, N_HEADS + hh:N_HEADS + hh + 1]
        i_col = gates[rs, hh:hh + 1]
        b_row = rows[N_HEADS + hh:N_HEADS + hh + 1, rs]
        i_row = rows[hh:hh + 1, rs]
        m_prev = m_in[gi, :, hh:hh + 1]
        lw = jnp.where(causal, b_col - b_row + i_row, -jnp.inf)
        lp = b_col + m_prev
        mj = jnp.maximum(lp, jnp.max(lw, axis=-1, keepdims=True))
        m_new = mj[L - 1:L, :]
        b_last = b_col[L - 1:L, :]
        units.append(dict(
            gi=gi, hh=hh, hs=hs, rs=rs, mj=mj, m_new=m_new, wgt=jnp.exp(lw - mj), wp=jnp.exp(lp - mj),
            ws=jnp.exp(b_last - b_col + i_col - m_new),
            wpl=jnp.exp(b_last + m_prev - m_new),
            cm=c_in[gi, hh],
            nv=n_in[gi, hh:hh + 1, :],
            q=q[rs, hs], k=k[rs, hs], v=v[rs, hs]))

    scores = [_dot_nt(u['q'], u['k']) * u['wgt'] for u in units]
    inter = [_dot(u['q'], u['cm']) if kv_state else _dot_nt(u['q'], u['cm']) for u in units]
    cells = []
    for u, s, it_ in zip(units, scores, inter):
        num = _dot(s, u['v']) + u['wp'] * it_
        den = jnp.sum(s, axis=-1, keepdims=True) + u['wp'] * jnp.sum(u['q'] * u['nv'], axis=-1, keepdims=True)
        cells.append(num / jnp.maximum(jnp.abs(den), jnp.exp(-u['mj'])))
    for u in units:
        gi, hh, vw = u['gi'], u['hh'], u['v'] * u['ws']
        c_ref[gi, hh] = u['wpl'] * u['cm'] + (_dot_tn(u['k'], vw) if kv_state else _dot_tn(vw, u['k']))
        n_ref[gi, hh:hh + 1, :] = u['wpl'] * u['nv'] + jnp.sum(u['k'] * u['ws'], axis=0, keepdims=True)
        m_ref[gi, :, hh:hh + 1] = u['m_new']
    for u, hcell in zip(units, cells):
        hs, rs = u['hs'], u['rs']
        hc = hcell - jnp.mean(hcell, axis=-1, keepdims=True)
        hn = hc * lax.rsqrt(jnp.mean(hc * hc, axis=-1, keepdims=True) + EPS) * gn_ref[...]
        out = (hn + skip_ref[:, hs] * xc[rs, hs]) * _silu(zg[rs, hs])
        hg_ref[rs, hs] = out.astype(hg_ref.dtype)

    @pl.when(c == last)
    def _():
        for gi in range(G):
            conv_ref[gi] = tails[gi]
            if kv_state:
                for hh in range(N_HEADS):
                    c_ref[gi, hh] = c_ref[gi, hh].T


def _mixer1(src, n_streams, T, states, cw, cb, wq, wk, wv, wgate, bgate, gn, skip):
    L = min(CHUNK1, T)
    nc = T // L
    G = max(1, min(n_streams, STEP_ROWS1 // L)) if nc == 1 else 1
    fused = len(src) == 3
    zero_init = states is None
    tri = jnp.asarray(np.kron(np.eye(G, dtype=np.float32), np.tril(np.ones((L, L), np.float32))), BF16)
    full = lambda a: pl.BlockSpec(a.shape, lambda s, c: (0,) * a.ndim, pipeline_mode=pl.Buffered(1))
    rows = lambda a: pl.BlockSpec((G * L, a.shape[1]), lambda s, c: (s * nc + c, 0))
    per_stream = lambda shape: pl.BlockSpec((G,) + shape[1:], lambda s, c: (s,) + (0,) * (len(shape) - 1))
    gn2, skip2, cb2 = gn.reshape(1, C_DH), skip.reshape(1, C_INNER), cb.reshape(1, C_INNER)
    state_shapes = ((n_streams, N_HEADS, C_DH, C_DH), (n_streams, N_HEADS, C_DH), (n_streams, 1, N_HEADS),
                    (n_streams, C_CONV - 1, C_INNER))
    scratch = [pltpu.VMEM((G, CONV_HDR + L, C_INNER), F32)]
    if fused:
        src = (src[0], src[1].reshape(1, -1), src[2])
        src_specs = [rows(src[0]), full(src[1]), full(src[2])]
    else:
        src_specs = [rows(src[0])]
    consts = (cw, cb2, wq, wk, wv, wgate, bgate, tri, gn2, skip2)
    if not zero_init:
        states = tuple(a.reshape(shape) for a, shape in zip(states, state_shapes))
    hg, c1, n1, m1, conv1 = pl.pallas_call(
        functools.partial(_mixer1_body, L=L, fused=fused, zero_init=zero_init, single_chunk=nc == 1),
        grid=(n_streams // G, nc),
        in_specs=src_specs + [full(a) for a in consts]
                 + ([] if zero_init else [per_stream(shape) for shape in state_shapes]),
        out_specs=[pl.BlockSpec((G * L, C_INNER), lambda s, c: (s * nc + c, 0))]
                  + [per_stream(shape) for shape in state_shapes],
        out_shape=[jax.ShapeDtypeStruct((n_streams * T, C_INNER), BF16)]
                  + [jax.ShapeDtypeStruct(shape, F32) for shape in state_shapes],
        scratch_shapes=scratch,
        compiler_params=pltpu.CompilerParams(dimension_semantics=("arbitrary", "arbitrary"),
                                             vmem_limit_bytes=VMEM_LIMIT_BYTES),
        name="mixer1_T%d" % T,
    )(*src, *consts, *(() if zero_init else states))
    return hg, c1, n1, m1.reshape(n_streams, N_HEADS), conv1


def _dense_blockdiag(w):
    rows = w.reshape(-1, BD_GROUP, C_BLOCK)
    tiled = jnp.tile(rows, (1, 1, BD_GROUP // C_BLOCK))
    idx = np.arange(BD_GROUP) // C_BLOCK
    same_block = jnp.asarray(idx[:, None] == idx[None, :])
    return jnp.where(same_block, tiled, 0.0).astype(BF16)


def kernel(x_prompt, x_sample, state_hgrn, state_ret, state_mlstm_c, state_mlstm_n, state_mlstm_m, state_conv,
           norm_mix, norm_ffn, norm_final, w_in0, lb_logits, hgrn_norm, ret_norm, w_out0,
           w_up1, conv_w, conv_b, w_q1, w_k1, w_v1, w_ig, b_ig, w_fg, b_fg, mlstm_norm, mlstm_skip, w_down1,
           w_ffn_gate, w_ffn_up, w_ffn_down):
    bp, tp, d = x_prompt.shape
    bs, ts, _ = x_sample.shape
    mp, ms = bp * tp, bs * ts
    past_len = 2048
    zeros = lambda *shape: jnp.zeros(shape, F32)
    w_in, = _cast_weights([w_in0[0]])
    later_weights = [w_out0[0], w_up1[0], w_down1[0], w_ffn_gate, w_ffn_up, w_ffn_down]

    xp = x_prompt.reshape(mp, d)
    xs = x_sample.reshape(ms, d)

    mix_p, hg_p, rt_p, (w_out, w_up, w_down, wf_gate, wf_up, wf_down) = _mixer0(
        (xp, norm_mix[0], w_in), bp, tp, 0, zeros(bp, N_HEADS, HEAD_DK, HEAD_DK),
        zeros(bp, N_HEADS, HEAD_DK, HEAD_DK), lb_logits, hgrn_norm[0], ret_norm[0], 0, side_cast=later_weights)
    ffn0 = (w_out, norm_ffn[0], wf_gate, wf_up, wf_down, 0, norm_final, False)
    mix_s, hg_s, rt_s, _ = _mixer0((_norm_matmul(xs, norm_mix[0], w_in),), bs, ts, past_len, state_hgrn[0],
                                   state_ret[0], lb_logits, hgrn_norm[0], ret_norm[0], 0)
    xp, xs = _proj_ffn(xp, mix_p, xs, mix_s, *ffn0)

    ffn1 = (w_down, norm_ffn[1], wf_gate, wf_up, wf_down, 1, norm_final, True)
    wgate = jnp.pad(jnp.concatenate([w_ig[0], w_fg[0]], axis=1), ((0, 0), (0, GATE_LANES - 2 * N_HEADS))).astype(BF16)
    bgate = jnp.pad(jnp.concatenate([b_ig[0], b_fg[0]]), (0, GATE_LANES - 2 * N_HEADS)).reshape(1, GATE_LANES)
    m1_consts = (conv_w[0], conv_b[0], _dense_blockdiag(w_q1[0]), _dense_blockdiag(w_k1[0]),
                 _dense_blockdiag(w_v1[0]), wgate, bgate, mlstm_norm[0], mlstm_skip[0])
    hg1_p, mc_p, mn_p, mm_p, cv_p = _mixer1((xp, norm_mix[1], w_up), bp, tp, None, *m1_consts)
    hg1_s, mc_s, mn_s, mm_s, cv_s = _mixer1((_norm_matmul(xs, norm_mix[1], w_up),), bs, ts,
                                            (state_mlstm_c[0], state_mlstm_n[0], state_mlstm_m[0], state_conv[0]),
                                            *m1_consts)
    yp, ys = _proj_ffn(xp, hg1_p, xs, hg1_s, *ffn1)

    lead = lambda a: a[None]
    return (yp.reshape(bp, tp, d), ys.reshape(bs, ts, d),
            lead(hg_p), lead(hg_s), lead(rt_p), lead(rt_s),
            lead(mc_p), lead(mc_s), lead(mn_p), lead(mn_s), lead(mm_p), lead(mm_s), lead(cv_p), lead(cv_s))
```

```python
import functools

import numpy as np
import jax
import jax.numpy as jnp
from jax import lax
from jax.experimental import pallas as pl
from jax.experimental.pallas import tpu as pltpu

F32 = jnp.float32
BF16 = jnp.bfloat16

D_MODEL = 1024
CHUNK0 = 256
CHUNK1 = 256
STEP_ROWS0 = 128
STEP_ROWS1 = 32
SEL_BLOCK = 64
HGRN_SUB = 128
EPS = 1e-6
SUBLANES = 8
N_HEADS = 4
HEAD_DK = 128
KW = N_HEADS * HEAD_DK
IN0_WIDTH = 8 * KW
ROPE_BASE = 10000.0
C_INNER = 2 * D_MODEL
C_DH = C_INNER // N_HEADS
C_CONV = 4
C_BLOCK = 4
BD_GROUP = 256
D_FF = -(-8 * D_MODEL // (3 * 256)) * 256
GATE_LANES = 128
CONV_HDR = 8
VMEM_LIMIT_BYTES = 56 * 1024 * 1024
ROW_TILE = 512
CAST_STEPS = 4
SIDE_BLOCKS = 32


def _dot(a, b):
    return jnp.dot(a.astype(BF16), b.astype(BF16), preferred_element_type=F32)


def _dot_nt(a, b):
    return lax.dot_general(a.astype(BF16), b.astype(BF16), (((1,), (1,)), ((), ())),
                           preferred_element_type=F32)


def _dot_tn(a, b):
    return lax.dot_general(a.astype(BF16), b.astype(BF16), (((0,), (0,)), ((), ())),
                           preferred_element_type=F32)


def _exact_dot(sel, x):
    hi = x.astype(BF16)
    lo = (x - hi.astype(F32)).astype(BF16)
    n = x.shape[1]
    both = jnp.dot(sel, jnp.concatenate([hi, lo], axis=1), preferred_element_type=F32)
    return both[:, :n] + both[:, n:]


def _rms(x, g):
    return x * lax.rsqrt(jnp.mean(x * x, axis=-1, keepdims=True) + EPS) * g


def _sigmoid(x):
    return 1.0 / (1.0 + jnp.exp(-x))


def _silu(x):
    return x * _sigmoid(x)


def _log_sigmoid(x):
    return jnp.minimum(x, 0.0) - jnp.log(1.0 + jnp.exp(-jnp.abs(x)))


def _norm_matmul_body(x_ref, g_ref, w_ref, o_ref):
    h = _rms(x_ref[...], g_ref[...])
    o_ref[...] = jnp.dot(h.astype(BF16), w_ref[...], preferred_element_type=F32)


def _norm_matmul(x, g, w):
    m, d = x.shape
    n = w.shape[1]
    return pl.pallas_call(
        _norm_matmul_body,
        grid=(m // ROW_TILE,),
        in_specs=[pl.BlockSpec((ROW_TILE, d), lambda i: (i, 0)),
                  pl.BlockSpec((1, d), lambda i: (0, 0)),
                  pl.BlockSpec((d, n), lambda i: (0, 0), pipeline_mode=pl.Buffered(1))],
        out_specs=pl.BlockSpec((ROW_TILE, n), lambda i: (i, 0)),
        out_shape=jax.ShapeDtypeStruct((m, n), F32),
        compiler_params=pltpu.CompilerParams(dimension_semantics=("arbitrary",),
                                             vmem_limit_bytes=VMEM_LIMIT_BYTES),
        name="norm_matmul",
    )(x, g.reshape(1, d), w)


def _cast_weights_body(*refs, n):
    ins, outs = refs[:n], refs[n:]
    n_slot = IN0_WIDTH // KW
    for hh in range(N_HEADS):
        for j in range(n_slot):
            dst = slice((hh * n_slot + j) * HEAD_DK, (hh * n_slot + j + 1) * HEAD_DK)
            src = slice((j * N_HEADS + hh) * HEAD_DK, (j * N_HEADS + hh + 1) * HEAD_DK)
            outs[0][:, dst] = ins[0][:, src].astype(BF16)
    for i_ref, o_ref in zip(ins[1:], outs[1:]):
        o_ref[...] = i_ref[...].astype(BF16)


def _cast_weights(ws):
    flat = [w.reshape(-1, w.shape[-1]) for w in ws]
    spec = lambda w: pl.BlockSpec((w.shape[0] // CAST_STEPS, w.shape[1]), lambda i: (i, 0))
    out = pl.pallas_call(
        functools.partial(_cast_weights_body, n=len(flat)),
        grid=(CAST_STEPS,),
        in_specs=[spec(w) for w in flat],
        out_specs=[spec(w) for w in flat],
        out_shape=[jax.ShapeDtypeStruct(w.shape, BF16) for w in flat],
        compiler_params=pltpu.CompilerParams(dimension_semantics=("arbitrary",),
                                             vmem_limit_bytes=VMEM_LIMIT_BYTES),
        name="cast_weights",
    )(*flat)
    return [o.reshape(w.shape) for o, w in zip(out, ws)]


def _proj_ffn_body(x_ref, a_ref, wo_ref, g_ref, wg_ref, wu_ref, wd_ref, gf_ref, o_ref, *, final_norm):
    x1 = x_ref[...] + jnp.dot(a_ref[...], wo_ref[...], preferred_element_type=F32)
    h = _rms(x1, g_ref[...]).astype(BF16)
    gate = jnp.dot(h, wg_ref[...], preferred_element_type=F32)
    up = jnp.dot(h, wu_ref[...], preferred_element_type=F32)
    t = (_silu(gate) * up).astype(BF16)
    x2 = x1 + jnp.dot(t, wd_ref[...], preferred_element_type=F32)
    if final_norm:
        x2 = _rms(x2, gf_ref[...])
    o_ref[...] = x2


def _proj_ffn_pair_body(xa_ref, aa_ref, xb_ref, ab_ref, wo_ref, g_ref, wg_ref, wu_ref, wd_ref, gf_ref,
                        oa_ref, ob_ref, *, final_norm, n_a):
    i = pl.program_id(0)
    shared = (wo_ref, g_ref, wg_ref, wu_ref, wd_ref, gf_ref)

    @pl.when(i < n_a)
    def _():
        _proj_ffn_body(xa_ref, aa_ref, *shared, oa_ref, final_norm=final_norm)

    @pl.when(i >= n_a)
    def _():
        _proj_ffn_body(xb_ref, ab_ref, *shared, ob_ref, final_norm=final_norm)


def _proj_ffn(xa, aa, xb, ab, wo, g, wg, wu, wd, layer, gf, final_norm):
    d = xa.shape[1]
    ka = aa.shape[1]
    ff = wg.shape[2]
    n_a, n_b = xa.shape[0] // ROW_TILE, xb.shape[0] // ROW_TILE
    const = lambda shape: pl.BlockSpec(shape, lambda i: (0, 0), pipeline_mode=pl.Buffered(1))
    of_layer = lambda shape: pl.BlockSpec((None,) + shape, lambda i: (layer, 0, 0), pipeline_mode=pl.Buffered(1))
    rows_a = lambda w: pl.BlockSpec((ROW_TILE, w), lambda i: (jnp.minimum(i, n_a - 1), 0))
    rows_b = lambda w: pl.BlockSpec((ROW_TILE, w), lambda i: (jnp.maximum(i - n_a, 0), 0))
    return pl.pallas_call(
        functools.partial(_proj_ffn_pair_body, final_norm=final_norm, n_a=n_a),
        grid=(n_a + n_b,),
        in_specs=[rows_a(d), rows_a(ka), rows_b(d), rows_b(ka),
                  const((ka, d)), const((1, d)), of_layer((d, ff)), of_layer((d, ff)), of_layer((ff, d)),
                  const((1, d))],
        out_specs=[rows_a(d), rows_b(d)],
        out_shape=[jax.ShapeDtypeStruct(xa.shape, F32), jax.ShapeDtypeStruct(xb.shape, F32)],
        compiler_params=pltpu.CompilerParams(dimension_semantics=("arbitrary",),
                                             vmem_limit_bytes=VMEM_LIMIT_BYTES),
        name="proj_ffn",
    )(xa, aa, xb, ab, wo, g.reshape(1, d), wg, wu, wd, gf.reshape(1, d))


def _hgrn_consts(L):
    r = np.arange(L)[:, None]
    t = np.arange(L)[None, :]
    sels = [t <= r]
    masks = [r == t]
    h = L // 2
    while h >= 1:
        base = (r // (2 * h)) * (2 * h)
        upper = (r % (2 * h)) >= h
        if h < SUBLANES:
            sels.append(np.where(upper, (t >= base + h) & (t <= r), (t > r) & (t <= base + h - 1)))
        masks.append((r // (2 * h) == t // (2 * h)) & upper & ((t % (2 * h)) < h))
        h //= 2
    return np.concatenate(sels, axis=0).astype(np.float32), np.stack(masks).astype(np.float32)


def _decay_exponents(sel_ref, logf, L):
    blk = min(SEL_BLOCK, L)
    nb = L // blk
    res = [_exact_dot(sel_ref[...], logf[i * blk:(i + 1) * blk]) for i in range(nb)]
    cum = [r[0:blk] for r in res]
    pre = [jnp.zeros_like(cum[0][0:1])]
    for i in range(nb):
        pre.append(pre[i] + cum[i][blk - 1:blk])
    cat = lambda parts: jnp.concatenate(parts, axis=0) if len(parts) > 1 else parts[0]
    b = cat([cum[i] + pre[i] for i in range(nb)])
    levels = []
    nbh = nb // 2
    while nbh >= 1:
        parts = []
        for i in range(nb):
            ref = pre[(i // (2 * nbh)) * 2 * nbh + nbh]
            if (i % (2 * nbh)) >= nbh:
                parts.append(cum[i] + (pre[i] - ref))
            else:
                parts.append((ref - pre[i]) - cum[i])
        levels.append(cat(parts))
        nbh //= 2
    h = blk // 2
    while h >= SUBLANES:
        parts = []
        for i in range(nb):
            for base in range(0, blk, 2 * h):
                ref = cum[i][base + h - 1:base + h]
                parts += [ref - cum[i][base:base + h], cum[i][base + h:base + 2 * h] - ref]
        levels.append(cat(parts))
        h //= 2
    for lev in range(sel_ref.shape[0] // blk - 1):
        levels.append(cat([r[(lev + 1) * blk:(lev + 2) * blk] for r in res]))
    return b, pre[nb], levels


def _retention_consts(L):
    lg = np.log1p(-np.exp2(-5.0 - np.arange(N_HEADS, dtype=np.float32))).astype(np.float32)
    idx = np.arange(L, dtype=np.float32)
    rel = idx[:, None] - idx[None, :]
    dmat = np.where(rel >= 0, np.exp(lg[:, None, None] * np.maximum(rel, 0.0)), 0.0)
    w_in = np.exp(lg[:, None] * (idx + 1.0))
    w_tail = np.exp(lg[:, None] * (L - 1.0 - idx))
    bcast = lambda v: jnp.asarray(np.broadcast_to(v[:, :, None], (N_HEADS, L, HEAD_DK)).astype(np.float32))
    g_chunk = [float(v) for v in np.exp(lg * L).astype(np.float32)]
    return jnp.asarray(dmat.astype(np.float32)), bcast(w_in), bcast(w_tail), g_chunk


def _rope_tables(pos0, T):
    half = HEAD_DK // 2
    inv = ROPE_BASE ** (-jnp.arange(half, dtype=F32) / half)
    ang = (jnp.arange(T) + pos0).astype(F32)[:, None] * inv[None, :]
    cos, sin = jnp.cos(ang), jnp.sin(ang)
    return jnp.concatenate([cos, cos], axis=-1), jnp.concatenate([-sin, sin], axis=-1)


def _hgrn_intra(units, mask_ref, L):
    sub = min(L, HGRN_SUB)
    n_sub = L // sub
    n_lev = mask_ref.shape[0] - 1
    top = len(units[0][3]) - n_lev
    assert n_sub in (1, 2) and top == n_sub - 1
    pieces = [(u, slice(i * sub, (i + 1) * sub)) for u in range(len(units)) for i in range(n_sub)]
    atts = [mask_ref[0] * _dot_nt(units[u][0][r], units[u][1][r]) for u, r in pieces]
    for lev in range(n_lev):
        for p, (u, r) in enumerate(pieces):
            q, k, _, expo = units[u]
            sc = jnp.exp(expo[top + lev][r])
            atts[p] = atts[p] + mask_ref[lev + 1] * _dot_nt(q[r] * sc, k[r] * sc)
    if n_sub == 1:
        return [_dot(att, units[u][2]) for att, (u, _) in zip(atts, pieces)]
    outs = []
    for u, (q, k, v, expo) in enumerate(units):
        cross = _dot_nt(q[sub:] * jnp.exp(expo[0][sub:]), k[:sub] * jnp.exp(expo[0][:sub]))
        outs.append(jnp.concatenate([_dot(atts[2 * u], v[:sub]),
                                     _dot(jnp.concatenate([cross, atts[2 * u + 1]], axis=1), v)], axis=0))
    return outs


def _mixer0_body(*refs, L, layer_slot, g_chunk, fused, n_side, side_every):
    it = iter(refs)
    take = lambda n: [next(it) for _ in range(n)]
    src = take(3 if fused else 1)
    (cos_ref, sin_ref, lbl_ref, ga_ref, gb_ref, sel_ref, mask_ref, dmat_ref, win_ref, wtail_ref,
     sa0_ref, sb0_ref) = take(12)
    side_in = take(n_side)
    mix_ref, sa_ref, sb_ref = take(3)
    side_out = take(n_side)
    sat_scr, sbt_scr = take(2)
    c = pl.program_id(1)
    last = pl.num_programs(1) - 1
    head_w = IN0_WIDTH // N_HEADS
    G = sa_ref.shape[0]

    if n_side:
        @pl.when((pl.program_id(0) * pl.num_programs(1) + c) % side_every == 0)
        def _():
            for i_ref, o_ref in zip(side_in, side_out):
                o_ref[...] = i_ref[...].astype(BF16)

    @pl.when(c == 0)
    def _():
        for gi in range(G):
            for hh in range(N_HEADS):
                sat_scr[gi * N_HEADS + hh] = sa0_ref[gi, hh].T
                sbt_scr[gi * N_HEADS + hh] = sb0_ref[gi, hh].T

    lbl = lbl_ref[...]
    e = jnp.exp(lbl - jnp.max(lbl, axis=0, keepdims=True))
    lb_all = jnp.sum(e[:layer_slot + 1], axis=0, keepdims=True) / jnp.sum(e, axis=0, keepdims=True)

    if fused:
        x_ref, g_ref, w_ref = src
        hb = _rms(x_ref[...], g_ref[...]).astype(BF16)
    cosf = cos_ref[...]
    sinf = sin_ref[...]

    def project(hh):
        hcols = slice(hh * head_w, (hh + 1) * head_w)
        return jnp.dot(hb, w_ref[:, hcols], preferred_element_type=F32) if fused else src[0][:, hcols]

    hg_units, ret_units = [], []
    z_next = project(0)
    for hh in range(N_HEADS):
        hs = slice(hh * HEAD_DK, (hh + 1) * HEAD_DK)
        z_head = z_next
        if hh + 1 < N_HEADS:
            z_next = project(hh + 1)
        lb = lb_all[:, hs]
        for gi in range(G):
            rs = slice(gi * L, (gi + 1) * L)
            si = gi * N_HEADS + hh
            zh = z_head[rs]
            part = lambda j: zh[:, j * HEAD_DK:(j + 1) * HEAD_DK]
            f = lb + (1.0 - lb) * _sigmoid(part(1))
            b, b_last, expo = _decay_exponents(sel_ref, jnp.log(f), L)
            hg_units.append((rs, hs, si, part(0), 1.0 - f, part(2), part(3), b, b_last, expo))
            rq = part(4)
            rk = part(5)
            rq = rq * cosf + pltpu.roll(rq, HEAD_DK // 2, 1) * sinf
            rk = (rk * cosf + pltpu.roll(rk, HEAD_DK // 2, 1) * sinf) * (HEAD_DK ** -0.5)
            ret_units.append((rs, hh, si, rq, rk, part(6), part(7)))

    states = [sat_scr[u[2]] for u in hg_units]
    inter = [_dot_nt(u[3] * jnp.exp(u[7]), st) for u, st in zip(hg_units, states)]
    intra = _hgrn_intra([(u[3], u[4], u[5], u[9]) for u in hg_units], mask_ref, L)
    for (rs, hs, si, q, k, v, og, b, b_last, _), st, o1, o2 in zip(hg_units, states, inter, intra):
        sat_scr[si] = st * jnp.exp(b_last) + _dot_tn(v, k * jnp.exp(b_last - b))
        o = _rms((o1 + o2) * _sigmoid(og), ga_ref[...])
        mix_ref[rs, hs] = o.astype(mix_ref.dtype)

    states = [sbt_scr[si] for _, _, si, _, _, _, _ in ret_units]
    scores = [_dot_nt(rq, rk) * dmat_ref[hh] for _, hh, _, rq, rk, _, _ in ret_units]
    inter = [_dot_nt(u[3], rt) for u, rt in zip(ret_units, states)]
    outs = [_dot(sc, u[5]) + win_ref[u[1]] * it_ for u, sc, it_ in zip(ret_units, scores, inter)]
    for (rs, hh, si, rq, rk, rv, rg), rt, ro in zip(ret_units, states, outs):
        sbt_scr[si] = g_chunk[hh] * rt + _dot_tn(rv, rk * wtail_ref[hh])
        ro = _rms(ro, gb_ref[...]) * _silu(rg)
        mix_ref[rs, KW + hh * HEAD_DK:KW + (hh + 1) * HEAD_DK] = ro.astype(mix_ref.dtype)

    @pl.when(c == last)
    def _():
        for gi in range(G):
            for hh in range(N_HEADS):
                sa_ref[gi, hh] = sat_scr[gi * N_HEADS + hh].T
                sb_ref[gi, hh] = sbt_scr[gi * N_HEADS + hh].T


def _mixer0(src, n_streams, T, pos0, sa0, sb0, lb_logits, ga, gb, layer_slot, side_cast=()):
    L = min(CHUNK0, T)
    nc = T // L
    G = max(1, min(n_streams, STEP_ROWS0 // L)) if nc == 1 else 1
    fused = len(src) == 3
    sel = jnp.asarray(_hgrn_consts(min(SEL_BLOCK, L))[0], BF16)
    masks = jnp.asarray(_hgrn_consts(min(HGRN_SUB, L))[1])
    dmat, w_in, w_tail, g_chunk = _retention_consts(L)
    cosf, sinf = _rope_tables(pos0, T)
    full = lambda a: pl.BlockSpec(a.shape, lambda s, c: (0,) * a.ndim, pipeline_mode=pl.Buffered(1))
    rows = lambda a: pl.BlockSpec((G * L, a.shape[1]), lambda s, c: (s * nc + c, 0))
    state_spec = pl.BlockSpec((G, N_HEADS, HEAD_DK, HEAD_DK), lambda s, c: (s, 0, 0, 0))
    state_shape = jax.ShapeDtypeStruct((n_streams, N_HEADS, HEAD_DK, HEAD_DK), F32)
    ga2, gb2 = ga.reshape(1, HEAD_DK), gb.reshape(1, HEAD_DK)
    if fused:
        src = (src[0], src[1].reshape(1, -1), src[2])
        src_specs = [rows(src[0]), full(src[1]), full(src[2])]
    else:
        src_specs = [rows(src[0])]
    consts = (lb_logits, ga2, gb2, sel, masks, dmat, w_in, w_tail)
    side = [w.reshape(-1, w.shape[-1]) for w in side_cast]
    per_block = (n_streams // G) * nc // SIDE_BLOCKS if side else 1
    side_spec = lambda w: pl.BlockSpec((w.shape[0] // SIDE_BLOCKS, w.shape[1]),
                                       lambda s, c: ((s * nc + c) // per_block, 0))
    outs = pl.pallas_call(
        functools.partial(_mixer0_body, L=L, layer_slot=layer_slot, g_chunk=g_chunk, fused=fused,
                          n_side=len(side), side_every=per_block),
        grid=(n_streams // G, nc),
        in_specs=src_specs
                 + [pl.BlockSpec((L, HEAD_DK), lambda s, c: (c, 0)), pl.BlockSpec((L, HEAD_DK), lambda s, c: (c, 0))]
                 + [full(a) for a in consts] + [state_spec, state_spec] + [side_spec(w) for w in side],
        out_specs=[pl.BlockSpec((G * L, 2 * KW), lambda s, c: (s * nc + c, 0)), state_spec, state_spec]
                  + [side_spec(w) for w in side],
        out_shape=[jax.ShapeDtypeStruct((n_streams * T, 2 * KW), BF16), state_shape, state_shape]
                  + [jax.ShapeDtypeStruct(w.shape, BF16) for w in side],
        scratch_shapes=[pltpu.VMEM((G * N_HEADS, HEAD_DK, HEAD_DK), F32),
                        pltpu.VMEM((G * N_HEADS, HEAD_DK, HEAD_DK), F32)],
        compiler_params=pltpu.CompilerParams(dimension_semantics=("arbitrary", "arbitrary"),
                                             vmem_limit_bytes=VMEM_LIMIT_BYTES),
        name="mixer0_T%d" % T,
    )(*src, cosf, sinf, *consts, sa0, sb0, *side)
    return outs[0], outs[1], outs[2], [o.reshape(w.shape) for o, w in zip(outs[3:], side_cast)]


def _mixer1_body(*refs, L, fused, zero_init, single_chunk):
    it = iter(refs)
    take = lambda n: [next(it) for _ in range(n)]
    src = take(3 if fused else 1)
    (cw_ref, cb_ref, wq_ref, wk_ref, wv_ref, wgate_ref, bgate_ref, tri_ref, gn_ref, skip_ref) = take(10)
    init = None if zero_init else take(4)
    hg_ref, c_ref, n_ref, m_ref, conv_ref = take(5)
    xbuf = take(1)[0]
    c = pl.program_id(1)
    last = pl.num_programs(1) - 1
    n_taps = C_CONV - 1
    G = c_ref.shape[0]
    kv_state = zero_init

    @pl.when(c == 0)
    def _():
        if zero_init:
            c_ref[...] = jnp.zeros_like(c_ref)
            n_ref[...] = jnp.zeros_like(n_ref)
            m_ref[...] = jnp.zeros_like(m_ref)
            xbuf[:, 0:CONV_HDR, :] = jnp.zeros((G, CONV_HDR, C_INNER), F32)
        else:
            c0_ref, n0_ref, m0_ref, conv0_ref = init
            if not single_chunk:
                c_ref[...] = c0_ref[...]
                n_ref[...] = n0_ref[...]
                m_ref[...] = m0_ref[...]
            xbuf[:, CONV_HDR - n_taps:CONV_HDR, :] = conv0_ref[...]

    c_in, n_in, m_in = init[:3] if single_chunk and not zero_init else (c_ref, n_ref, m_ref)

    if fused:
        x_ref, g_ref, w_ref = src
        hb = _rms(x_ref[...], g_ref[...]).astype(BF16)
        xm = jnp.dot(hb, w_ref[:, :C_INNER], preferred_element_type=F32)
        zg = jnp.dot(hb, w_ref[:, C_INNER:], preferred_element_type=F32)
    else:
        xm = src[0][:, :C_INNER]
        zg = src[0][:, C_INNER:]

    xcs, tails = [], []
    for gi in range(G):
        xg = xm[gi * L:(gi + 1) * L]
        xbuf[gi, CONV_HDR:CONV_HDR + L, :] = xg
        acc = cb_ref[...] + xg * cw_ref[n_taps:n_taps + 1, :]
        for w in range(n_taps):
            acc = acc + xbuf[gi, CONV_HDR - n_taps + w:CONV_HDR - n_taps + w + L, :] * cw_ref[w:w + 1, :]
        tails.append(xbuf[gi, CONV_HDR + L - n_taps:CONV_HDR + L, :])
        xbuf[gi, CONV_HDR - n_taps:CONV_HDR, :] = tails[gi]
        xcs.append(acc)
    xc = _silu(jnp.concatenate(xcs, axis=0) if G > 1 else xcs[0])

    xcb = xc.astype(BF16)
    xmb = xm.astype(BF16)
    qs, ks, vs = [], [], []
    for g in range(C_INNER // BD_GROUP):
        gs = slice(g * BD_GROUP, (g + 1) * BD_GROUP)
        qs.append(jnp.dot(xcb[:, gs], wq_ref[g], preferred_element_type=F32))
        ks.append(jnp.dot(xcb[:, gs], wk_ref[g], preferred_element_type=F32))
        vs.append(jnp.dot(xmb[:, gs], wv_ref[g], preferred_element_type=F32))
    q = jnp.concatenate(qs, axis=1).astype(BF16)
    k = (jnp.concatenate(ks, axis=1) * (C_DH ** -0.5)).astype(BF16)
    v = jnp.concatenate(vs, axis=1).astype(BF16)

    gates = (_dot(q, wgate_ref[0:C_INNER, :]) + _dot(k, wgate_ref[C_INNER:2 * C_INNER, :])
             + _dot(v, wgate_ref[2 * C_INNER:3 * C_INNER, :]) + bgate_ref[...])
    bcum = _exact_dot(tri_ref[...], _log_sigmoid(gates))
    lane = lax.broadcasted_iota(jnp.int32, gates.shape, 1)
    rows = jnp.where(lane < N_HEADS, gates, bcum).T
    ti = lax.broadcasted_iota(jnp.int32, (L, L), 0)
    si = lax.broadcasted_iota(jnp.int32, (L, L), 1)
    causal = si <= ti

    units = []
    for gi, hh in [(gi, hh) for gi in range(G) for hh in range(N_HEADS)]:
        hs = slice(hh * C_DH, (hh + 1) * C_DH)
        rs = slice(gi * L, (gi + 1) * L)
        b_col = bcum[rs, N_HEADS + hh:N_HEADS + hh + 1]
        i_col = gates[rs, hh:hh + 1]
        b_row = rows[N_HEADS + hh:N_HEADS + hh + 1, rs]
        i_row = rows[hh:hh + 1, rs]
        m_prev = m_in[gi, :, hh:hh + 1]
        lw = jnp.where(causal, b_col - b_row + i_row, -jnp.inf)
        lp = b_col + m_prev
        mj = jnp.maximum(lp, jnp.max(lw, axis=-1, keepdims=True))
        m_new = mj[L - 1:L, :]
        b_last = b_col[L - 1:L, :]
        units.append(dict(
            gi=gi, hh=hh, hs=hs, rs=rs, mj=mj, m_new=m_new, wgt=jnp.exp(lw - mj), wp=jnp.exp(lp - mj),
            ws=jnp.exp(b_last - b_col + i_col - m_new),
            wpl=jnp.exp(b_last + m_prev - m_new),
            cm=c_in[gi, hh],
            nv=n_in[gi, hh:hh + 1, :],
            q=q[rs, hs], k=k[rs, hs], v=v[rs, hs]))

    scores = [_dot_nt(u['q'], u['k']) * u['wgt'] for u in units]
    inter = [_dot(u['q'], u['cm']) if kv_state else _dot_nt(u['q'], u['cm']) for u in units]
    cells = []
    for u, s, it_ in zip(units, scores, inter):
        num = _dot(s, u['v']) + u['wp'] * it_
        den = jnp.sum(s, axis=-1, keepdims=True) + u['wp'] * jnp.sum(u['q'] * u['nv'], axis=-1, keepdims=True)
        cells.append(num / jnp.maximum(jnp.abs(den), jnp.exp(-u['mj'])))
    for u in units:
        gi, hh, vw = u['gi'], u['hh'], u['v'] * u['ws']
        c_ref[gi, hh] = u['wpl'] * u['cm'] + (_dot_tn(u['k'], vw) if kv_state else _dot_tn(vw, u['k']))
        n_ref[gi, hh:hh + 1, :] = u['wpl'] * u['nv'] + jnp.sum(u['k'] * u['ws'], axis=0, keepdims=True)
        m_ref[gi, :, hh:hh + 1] = u['m_new']
    for u, hcell in zip(units, cells):
        hs, rs = u['hs'], u['rs']
        hc = hcell - jnp.mean(hcell, axis=-1, keepdims=True)
        hn = hc * lax.rsqrt(jnp.mean(hc * hc, axis=-1, keepdims=True) + EPS) * gn_ref[...]
        out = (hn + skip_ref[:, hs] * xc[rs, hs]) * _silu(zg[rs, hs])
        hg_ref[rs, hs] = out.astype(hg_ref.dtype)

    @pl.when(c == last)
    def _():
        for gi in range(G):
            conv_ref[gi] = tails[gi]
            if kv_state:
                for hh in range(N_HEADS):
                    c_ref[gi, hh] = c_ref[gi, hh].T


def _mixer1(src, n_streams, T, states, cw, cb, wq, wk, wv, wgate, bgate, gn, skip):
    L = min(CHUNK1, T)
    nc = T // L
    G = max(1, min(n_streams, STEP_ROWS1 // L)) if nc == 1 else 1
    fused = len(src) == 3
    zero_init = states is None
    tri = jnp.asarray(np.kron(np.eye(G, dtype=np.float32), np.tril(np.ones((L, L), np.float32))), BF16)
    full = lambda a: pl.BlockSpec(a.shape, lambda s, c: (0,) * a.ndim, pipeline_mode=pl.Buffered(1))
    rows = lambda a: pl.BlockSpec((G * L, a.shape[1]), lambda s, c: (s * nc + c, 0))
    per_stream = lambda shape: pl.BlockSpec((G,) + shape[1:], lambda s, c: (s,) + (0,) * (len(shape) - 1))
    gn2, skip2, cb2 = gn.reshape(1, C_DH), skip.reshape(1, C_INNER), cb.reshape(1, C_INNER)
    state_shapes = ((n_streams, N_HEADS, C_DH, C_DH), (n_streams, N_HEADS, C_DH), (n_streams, 1, N_HEADS),
                    (n_streams, C_CONV - 1, C_INNER))
    scratch = [pltpu.VMEM((G, CONV_HDR + L, C_INNER), F32)]
    if fused:
        src = (src[0], src[1].reshape(1, -1), src[2])
        src_specs = [rows(src[0]), full(src[1]), full(src[2])]
    else:
        src_specs = [rows(src[0])]
    consts = (cw, cb2, wq, wk, wv, wgate, bgate, tri, gn2, skip2)
    if not zero_init:
        states = tuple(a.reshape(shape) for a, shape in zip(states, state_shapes))
    hg, c1, n1, m1, conv1 = pl.pallas_call(
        functools.partial(_mixer1_body, L=L, fused=fused, zero_init=zero_init, single_chunk=nc == 1),
        grid=(n_streams // G, nc),
        in_specs=src_specs + [full(a) for a in consts]
                 + ([] if zero_init else [per_stream(shape) for shape in state_shapes]),
        out_specs=[pl.BlockSpec((G * L, C_INNER), lambda s, c: (s * nc + c, 0))]
                  + [per_stream(shape) for shape in state_shapes],
        out_shape=[jax.ShapeDtypeStruct((n_streams * T, C_INNER), BF16)]
                  + [jax.ShapeDtypeStruct(shape, F32) for shape in state_shapes],
        scratch_shapes=scratch,
        compiler_params=pltpu.CompilerParams(dimension_semantics=("arbitrary", "arbitrary"),
                                             vmem_limit_bytes=VMEM_LIMIT_BYTES),
        name="mixer1_T%d" % T,
    )(*src, *consts, *(() if zero_init else states))
    return hg, c1, n1, m1.reshape(n_streams, N_HEADS), conv1


def _dense_blockdiag(w):
    rows = w.reshape(-1, BD_GROUP, C_BLOCK)
    tiled = jnp.tile(rows, (1, 1, BD_GROUP // C_BLOCK))
    idx = np.arange(BD_GROUP) // C_BLOCK
    same_block = jnp.asarray(idx[:, None] == idx[None, :])
    return jnp.where(same_block, tiled, 0.0).astype(BF16)


def kernel(x_prompt, x_sample, state_hgrn, state_ret, state_mlstm_c, state_mlstm_n, state_mlstm_m, state_conv,
           norm_mix, norm_ffn, norm_final, w_in0, lb_logits, hgrn_norm, ret_norm, w_out0,
           w_up1, conv_w, conv_b, w_q1, w_k1, w_v1, w_ig, b_ig, w_fg, b_fg, mlstm_norm, mlstm_skip, w_down1,
           w_ffn_gate, w_ffn_up, w_ffn_down):
    bp, tp, d = x_prompt.shape
    bs, ts, _ = x_sample.shape
    mp, ms = bp * tp, bs * ts
    past_len = 2048
    zeros = lambda *shape: jnp.zeros(shape, F32)
    w_in, = _cast_weights([w_in0[0]])
    later_weights = [w_out0[0], w_up1[0], w_down1[0], w_ffn_gate, w_ffn_up, w_ffn_down]

    xp = x_prompt.reshape(mp, d)
    xs = x_sample.reshape(ms, d)

    mix_p, hg_p, rt_p, (w_out, w_up, w_down, wf_gate, wf_up, wf_down) = _mixer0(
        (xp, norm_mix[0], w_in), bp, tp, 0, zeros(bp, N_HEADS, HEAD_DK, HEAD_DK),
        zeros(bp, N_HEADS, HEAD_DK, HEAD_DK), lb_logits, hgrn_norm[0], ret_norm[0], 0, side_cast=later_weights)
    ffn0 = (w_out, norm_ffn[0], wf_gate, wf_up, wf_down, 0, norm_final, False)
    mix_s, hg_s, rt_s, _ = _mixer0((_norm_matmul(xs, norm_mix[0], w_in),), bs, ts, past_len, state_hgrn[0],
                                   state_ret[0], lb_logits, hgrn_norm[0], ret_norm[0], 0)
    xp, xs = _proj_ffn(xp, mix_p, xs, mix_s, *ffn0)

    ffn1 = (w_down, norm_ffn[1], wf_gate, wf_up, wf_down, 1, norm_final, True)
    wgate = jnp.pad(jnp.concatenate([w_ig[0], w_fg[0]], axis=1), ((0, 0), (0, GATE_LANES - 2 * N_HEADS))).astype(BF16)
    bgate = jnp.pad(jnp.concatenate([b_ig[0], b_fg[0]]), (0, GATE_LANES - 2 * N_HEADS)).reshape(1, GATE_LANES)
    m1_consts = (conv_w[0], conv_b[0], _dense_blockdiag(w_q1[0]), _dense_blockdiag(w_k1[0]),
                 _dense_blockdiag(w_v1[0]), wgate, bgate, mlstm_norm[0], mlstm_skip[0])
    hg1_p, mc_p, mn_p, mm_p, cv_p = _mixer1((xp, norm_mix[1], w_up), bp, tp, None, *m1_consts)
    hg1_s, mc_s, mn_s, mm_s, cv_s = _mixer1((_norm_matmul(xs, norm_mix[1], w_up),), bs, ts,
                                            (state_mlstm_c[0], state_mlstm_n[0], state_mlstm_m[0], state_conv[0]),
                                            *m1_consts)
    yp, ys = _proj_ffn(xp, hg1_p, xs, hg1_s, *ffn1)

    lead = lambda a: a[None]
    return (yp.reshape(bp, tp, d), ys.reshape(bs, ts, d),
            lead(hg_p), lead(hg_s), lead(rt_p), lead(rt_s),
            lead(mc_p), lead(mc_s), lead(mn_p), lead(mn_s), lead(mm_p), lead(mm_s), lead(cv_p), lead(cv_s))
```

```python
import functools

import numpy as np
import jax
import jax.numpy as jnp
from jax import lax
from jax.experimental import pallas as pl
from jax.experimental.pallas import tpu as pltpu

F32 = jnp.float32
BF16 = jnp.bfloat16

D_MODEL = 1024
CHUNK0 = 256
CHUNK1 = 256
STEP_ROWS0 = 128
STEP_ROWS1 = 32
SEL_BLOCK = 64
HGRN_SUB = 128
EPS = 1e-6
SUBLANES = 8
N_HEADS = 4
HEAD_DK = 128
KW = N_HEADS * HEAD_DK
IN0_WIDTH = 8 * KW
ROPE_BASE = 10000.0
C_INNER = 2 * D_MODEL
C_DH = C_INNER // N_HEADS
C_CONV = 4
C_BLOCK = 4
BD_GROUP = 256
PAST_LEN = 2048
GATE_LANES = 128
CONV_HDR = 8
VMEM_LIMIT_BYTES = 56 * 1024 * 1024
ROW_TILE = 512
CAST_STEPS = 4
SIDE_BLOCKS = 32


def _dot(a, b):
    return jnp.dot(a.astype(BF16), b.astype(BF16), preferred_element_type=F32)


def _dot_nt(a, b):
    return lax.dot_general(a.astype(BF16), b.astype(BF16), (((1,), (1,)), ((), ())),
                           preferred_element_type=F32)


def _dot_tn(a, b):
    return lax.dot_general(a.astype(BF16), b.astype(BF16), (((0,), (0,)), ((), ())),
                           preferred_element_type=F32)


def _exact_dot(sel, x):
    hi = x.astype(BF16)
    lo = (x - hi.astype(F32)).astype(BF16)
    n = x.shape[1]
    both = jnp.dot(sel, jnp.concatenate([hi, lo], axis=1), preferred_element_type=F32)
    return both[:, :n] + both[:, n:]


def _rms(x, g):
    return x * lax.rsqrt(jnp.mean(x * x, axis=-1, keepdims=True) + EPS) * g


def _sigmoid(x):
    return 1.0 / (1.0 + jnp.exp(-x))


def _silu(x):
    return x * _sigmoid(x)


def _log_sigmoid(x):
    return jnp.minimum(x, 0.0) - jnp.log(1.0 + jnp.exp(-jnp.abs(x)))


def _norm_matmul_body(x_ref, g_ref, w_ref, o_ref):
    h = _rms(x_ref[...], g_ref[...])
    o_ref[...] = jnp.dot(h.astype(BF16), w_ref[...], preferred_element_type=F32)


def _norm_matmul(x, g, w):
    m, d = x.shape
    n = w.shape[1]
    return pl.pallas_call(
        _norm_matmul_body,
        grid=(m // ROW_TILE,),
        in_specs=[pl.BlockSpec((ROW_TILE, d), lambda i: (i, 0)),
                  pl.BlockSpec((1, d), lambda i: (0, 0)),
                  pl.BlockSpec((d, n), lambda i: (0, 0), pipeline_mode=pl.Buffered(1))],
        out_specs=pl.BlockSpec((ROW_TILE, n), lambda i: (i, 0)),
        out_shape=jax.ShapeDtypeStruct((m, n), F32),
        compiler_params=pltpu.CompilerParams(dimension_semantics=("arbitrary",),
                                             vmem_limit_bytes=VMEM_LIMIT_BYTES),
        name="norm_matmul",
    )(x, g.reshape(1, d), w)


def _cast_weights_body(*refs, n):
    ins, outs = refs[:n], refs[n:]
    n_slot = IN0_WIDTH // KW
    for hh in range(N_HEADS):
        for j in range(n_slot):
            dst = slice((hh * n_slot + j) * HEAD_DK, (hh * n_slot + j + 1) * HEAD_DK)
            src = slice((j * N_HEADS + hh) * HEAD_DK, (j * N_HEADS + hh + 1) * HEAD_DK)
            outs[0][:, dst] = ins[0][:, src].astype(BF16)
    for i_ref, o_ref in zip(ins[1:], outs[1:]):
        o_ref[...] = i_ref[...].astype(BF16)


def _cast_weights(ws):
    flat = [w.reshape(-1, w.shape[-1]) for w in ws]
    spec = lambda w: pl.BlockSpec((w.shape[0] // CAST_STEPS, w.shape[1]), lambda i: (i, 0))
    out = pl.pallas_call(
        functools.partial(_cast_weights_body, n=len(flat)),
        grid=(CAST_STEPS,),
        in_specs=[spec(w) for w in flat],
        out_specs=[spec(w) for w in flat],
        out_shape=[jax.ShapeDtypeStruct(w.shape, BF16) for w in flat],
        compiler_params=pltpu.CompilerParams(dimension_semantics=("arbitrary",),
                                             vmem_limit_bytes=VMEM_LIMIT_BYTES),
        name="cast_weights",
    )(*flat)
    return [o.reshape(w.shape) for o, w in zip(out, ws)]


def _proj_ffn_body(x_ref, a_ref, wo_ref, g_ref, wg_ref, wu_ref, wd_ref, gf_ref, o_ref, *, final_norm):
    x1 = x_ref[...] + jnp.dot(a_ref[...], wo_ref[...], preferred_element_type=F32)
    h = _rms(x1, g_ref[...]).astype(BF16)
    gate = jnp.dot(h, wg_ref[...], preferred_element_type=F32)
    up = jnp.dot(h, wu_ref[...], preferred_element_type=F32)
    t = (_silu(gate) * up).astype(BF16)
    x2 = x1 + jnp.dot(t, wd_ref[...], preferred_element_type=F32)
    if final_norm:
        x2 = _rms(x2, gf_ref[...])
    o_ref[...] = x2


def _proj_ffn_pair_body(xa_ref, aa_ref, xb_ref, ab_ref, wo_ref, g_ref, wg_ref, wu_ref, wd_ref, gf_ref,
                        oa_ref, ob_ref, *, final_norm, n_a):
    i = pl.program_id(0)
    shared = (wo_ref, g_ref, wg_ref, wu_ref, wd_ref, gf_ref)

    @pl.when(i < n_a)
    def _():
        _proj_ffn_body(xa_ref, aa_ref, *shared, oa_ref, final_norm=final_norm)

    @pl.when(i >= n_a)
    def _():
        _proj_ffn_body(xb_ref, ab_ref, *shared, ob_ref, final_norm=final_norm)


def _proj_ffn(xa, aa, xb, ab, wo, g, wg, wu, wd, layer, gf, final_norm):
    d = xa.shape[1]
    ka = aa.shape[1]
    ff = wg.shape[2]
    n_a, n_b = xa.shape[0] // ROW_TILE, xb.shape[0] // ROW_TILE
    const = lambda shape: pl.BlockSpec(shape, lambda i: (0, 0), pipeline_mode=pl.Buffered(1))
    of_layer = lambda shape: pl.BlockSpec((None,) + shape, lambda i: (layer, 0, 0), pipeline_mode=pl.Buffered(1))
    rows_a = lambda w: pl.BlockSpec((ROW_TILE, w), lambda i: (jnp.minimum(i, n_a - 1), 0))
    rows_b = lambda w: pl.BlockSpec((ROW_TILE, w), lambda i: (jnp.maximum(i - n_a, 0), 0))
    return pl.pallas_call(
        functools.partial(_proj_ffn_pair_body, final_norm=final_norm, n_a=n_a),
        grid=(n_a + n_b,),
        in_specs=[rows_a(d), rows_a(ka), rows_b(d), rows_b(ka),
                  const((ka, d)), const((1, d)), of_layer((d, ff)), of_layer((d, ff)), of_layer((ff, d)),
                  const((1, d))],
        out_specs=[rows_a(d), rows_b(d)],
        out_shape=[jax.ShapeDtypeStruct(xa.shape, F32), jax.ShapeDtypeStruct(xb.shape, F32)],
        compiler_params=pltpu.CompilerParams(dimension_semantics=("arbitrary",),
                                             vmem_limit_bytes=VMEM_LIMIT_BYTES),
        name="proj_ffn",
    )(xa, aa, xb, ab, wo, g.reshape(1, d), wg, wu, wd, gf.reshape(1, d))


def _hgrn_consts(L):
    r = np.arange(L)[:, None]
    t = np.arange(L)[None, :]
    sels = [t <= r]
    masks = [r == t]
    h = L // 2
    while h >= 1:
        base = (r // (2 * h)) * (2 * h)
        upper = (r % (2 * h)) >= h
        if h < SUBLANES:
            sels.append(np.where(upper, (t >= base + h) & (t <= r), (t > r) & (t <= base + h - 1)))
        masks.append((r // (2 * h) == t // (2 * h)) & upper & ((t % (2 * h)) < h))
        h //= 2
    return np.concatenate(sels, axis=0).astype(np.float32), np.stack(masks).astype(np.float32)


def _decay_exponents(sel_ref, logf, L):
    blk = min(SEL_BLOCK, L)
    nb = L // blk
    res = [_exact_dot(sel_ref[...], logf[i * blk:(i + 1) * blk]) for i in range(nb)]
    cum = [r[0:blk] for r in res]
    pre = [jnp.zeros_like(cum[0][0:1])]
    for i in range(nb):
        pre.append(pre[i] + cum[i][blk - 1:blk])
    cat = lambda parts: jnp.concatenate(parts, axis=0) if len(parts) > 1 else parts[0]
    b = cat([cum[i] + pre[i] for i in range(nb)])
    levels = []
    nbh = nb // 2
    while nbh >= 1:
        parts = []
        for i in range(nb):
            ref = pre[(i // (2 * nbh)) * 2 * nbh + nbh]
            if (i % (2 * nbh)) >= nbh:
                parts.append(cum[i] + (pre[i] - ref))
            else:
                parts.append((ref - pre[i]) - cum[i])
        levels.append(cat(parts))
        nbh //= 2
    h = blk // 2
    while h >= SUBLANES:
        parts = []
        for i in range(nb):
            for base in range(0, blk, 2 * h):
                ref = cum[i][base + h - 1:base + h]
                parts += [ref - cum[i][base:base + h], cum[i][base + h:base + 2 * h] - ref]
        levels.append(cat(parts))
        h //= 2
    for lev in range(sel_ref.shape[0] // blk - 1):
        levels.append(cat([r[(lev + 1) * blk:(lev + 2) * blk] for r in res]))
    return b, pre[nb], levels


def _retention_consts(L):
    lg = np.log1p(-np.exp2(-5.0 - np.arange(N_HEADS, dtype=np.float32))).astype(np.float32)
    idx = np.arange(L, dtype=np.float32)
    rel = idx[:, None] - idx[None, :]
    dmat = np.where(rel >= 0, np.exp(lg[:, None, None] * np.maximum(rel, 0.0)), 0.0)
    w_in = np.exp(lg[:, None] * (idx + 1.0))
    w_tail = np.exp(lg[:, None] * (L - 1.0 - idx))
    bcast = lambda v: jnp.asarray(np.broadcast_to(v[:, :, None], (N_HEADS, L, HEAD_DK)).astype(np.float32))
    g_chunk = [float(v) for v in np.exp(lg * L).astype(np.float32)]
    return jnp.asarray(dmat.astype(np.float32)), bcast(w_in), bcast(w_tail), g_chunk


def _rope_tables(pos0, T):
    half = HEAD_DK // 2
    inv = ROPE_BASE ** (-jnp.arange(half, dtype=F32) / half)
    ang = (jnp.arange(T) + pos0).astype(F32)[:, None] * inv[None, :]
    cos, sin = jnp.cos(ang), jnp.sin(ang)
    return jnp.concatenate([cos, cos], axis=-1), jnp.concatenate([-sin, sin], axis=-1)


def _hgrn_intra(units, mask_ref, L):
    sub = min(L, HGRN_SUB)
    n_sub = L // sub
    n_lev = mask_ref.shape[0] - 1
    top = len(units[0][3]) - n_lev
    assert n_sub in (1, 2) and top == n_sub - 1
    pieces = [(u, slice(i * sub, (i + 1) * sub)) for u in range(len(units)) for i in range(n_sub)]
    atts = [mask_ref[0] * _dot_nt(units[u][0][r], units[u][1][r]) for u, r in pieces]
    for lev in range(n_lev):
        for p, (u, r) in enumerate(pieces):
            q, k, _, expo = units[u]
            sc = jnp.exp(expo[top + lev][r])
            atts[p] = atts[p] + mask_ref[lev + 1] * _dot_nt(q[r] * sc, k[r] * sc)
    if n_sub == 1:
        return [_dot(att, units[u][2]) for att, (u, _) in zip(atts, pieces)]
    outs = []
    for u, (q, k, v, expo) in enumerate(units):
        cross = _dot_nt(q[sub:] * jnp.exp(expo[0][sub:]), k[:sub] * jnp.exp(expo[0][:sub]))
        outs.append(jnp.concatenate([_dot(atts[2 * u], v[:sub]),
                                     _dot(jnp.concatenate([cross, atts[2 * u + 1]], axis=1), v)], axis=0))
    return outs


def _mixer0_body(*refs, L, layer_slot, g_chunk, fused, n_side, side_every):
    it = iter(refs)
    take = lambda n: [next(it) for _ in range(n)]
    src = take(3 if fused else 1)
    (cos_ref, sin_ref, lbl_ref, ga_ref, gb_ref, sel_ref, mask_ref, dmat_ref, win_ref, wtail_ref,
     sa0_ref, sb0_ref) = take(12)
    side_in = take(n_side)
    mix_ref, sa_ref, sb_ref = take(3)
    side_out = take(n_side)
    sat_scr, sbt_scr = take(2)
    c = pl.program_id(1)
    last = pl.num_programs(1) - 1
    head_w = IN0_WIDTH // N_HEADS
    G = sa_ref.shape[0]

    if n_side:
        @pl.when((pl.program_id(0) * pl.num_programs(1) + c) % side_every == 0)
        def _():
            for i_ref, o_ref in zip(side_in, side_out):
                o_ref[...] = i_ref[...].astype(BF16)

    @pl.when(c == 0)
    def _():
        for gi in range(G):
            for hh in range(N_HEADS):
                sat_scr[gi * N_HEADS + hh] = sa0_ref[gi, hh].T
                sbt_scr[gi * N_HEADS + hh] = sb0_ref[gi, hh].T

    lbl = lbl_ref[...]
    e = jnp.exp(lbl - jnp.max(lbl, axis=0, keepdims=True))
    lb_all = jnp.sum(e[:layer_slot + 1], axis=0, keepdims=True) / jnp.sum(e, axis=0, keepdims=True)

    if fused:
        x_ref, g_ref, w_ref = src
        hb = _rms(x_ref[...], g_ref[...]).astype(BF16)
    cosf = cos_ref[...]
    sinf = sin_ref[...]

    def project(hh):
        hcols = slice(hh * head_w, (hh + 1) * head_w)
        return jnp.dot(hb, w_ref[:, hcols], preferred_element_type=F32) if fused else src[0][:, hcols]

    hg_units, ret_units = [], []
    z_next = project(0)
    for hh in range(N_HEADS):
        hs = slice(hh * HEAD_DK, (hh + 1) * HEAD_DK)
        z_head = z_next
        if hh + 1 < N_HEADS:
            z_next = project(hh + 1)
        lb = lb_all[:, hs]
        for gi in range(G):
            rs = slice(gi * L, (gi + 1) * L)
            si = gi * N_HEADS + hh
            zh = z_head[rs]
            part = lambda j: zh[:, j * HEAD_DK:(j + 1) * HEAD_DK]
            f = lb + (1.0 - lb) * _sigmoid(part(1))
            b, b_last, expo = _decay_exponents(sel_ref, jnp.log(f), L)
            hg_units.append((rs, hs, si, part(0), 1.0 - f, part(2), part(3), b, b_last, expo))
            rq = part(4)
            rk = part(5)
            rq = rq * cosf + pltpu.roll(rq, HEAD_DK // 2, 1) * sinf
            rk = (rk * cosf + pltpu.roll(rk, HEAD_DK // 2, 1) * sinf) * (HEAD_DK ** -0.5)
            ret_units.append((rs, hh, si, rq, rk, part(6), part(7)))

    states = [sat_scr[u[2]] for u in hg_units]
    inter = [_dot_nt(u[3] * jnp.exp(u[7]), st) for u, st in zip(hg_units, states)]
    intra = _hgrn_intra([(u[3], u[4], u[5], u[9]) for u in hg_units], mask_ref, L)
    for (rs, hs, si, q, k, v, og, b, b_last, _), st, o1, o2 in zip(hg_units, states, inter, intra):
        sat_scr[si] = st * jnp.exp(b_last) + _dot_tn(v, k * jnp.exp(b_last - b))
        o = _rms((o1 + o2) * _sigmoid(og), ga_ref[...])
        mix_ref[rs, hs] = o.astype(mix_ref.dtype)

    states = [sbt_scr[si] for _, _, si, _, _, _, _ in ret_units]
    scores = [_dot_nt(rq, rk) * dmat_ref[hh] for _, hh, _, rq, rk, _, _ in ret_units]
    inter = [_dot_nt(u[3], rt) for u, rt in zip(ret_units, states)]
    outs = [_dot(sc, u[5]) + win_ref[u[1]] * it_ for u, sc, it_ in zip(ret_units, scores, inter)]
    for (rs, hh, si, rq, rk, rv, rg), rt, ro in zip(ret_units, states, outs):
        sbt_scr[si] = g_chunk[hh] * rt + _dot_tn(rv, rk * wtail_ref[hh])
        ro = _rms(ro, gb_ref[...]) * _silu(rg)
        mix_ref[rs, KW + hh * HEAD_DK:KW + (hh + 1) * HEAD_DK] = ro.astype(mix_ref.dtype)

    @pl.when(c == last)
    def _():
        for gi in range(G):
            for hh in range(N_HEADS):
                sa_ref[gi, hh] = sat_scr[gi * N_HEADS + hh].T
                sb_ref[gi, hh] = sbt_scr[gi * N_HEADS + hh].T


def _mixer0(src, n_streams, T, pos0, sa0, sb0, lb_logits, ga, gb, layer_slot, side_cast=()):
    L = min(CHUNK0, T)
    nc = T // L
    G = max(1, min(n_streams, STEP_ROWS0 // L)) if nc == 1 else 1
    fused = len(src) == 3
    sel = jnp.asarray(_hgrn_consts(min(SEL_BLOCK, L))[0], BF16)
    masks = jnp.asarray(_hgrn_consts(min(HGRN_SUB, L))[1])
    dmat, w_in, w_tail, g_chunk = _retention_consts(L)
    cosf, sinf = _rope_tables(pos0, T)
    full = lambda a: pl.BlockSpec(a.shape, lambda s, c: (0,) * a.ndim, pipeline_mode=pl.Buffered(1))
    rows = lambda a: pl.BlockSpec((G * L, a.shape[1]), lambda s, c: (s * nc + c, 0))
    state_spec = pl.BlockSpec((G, N_HEADS, HEAD_DK, HEAD_DK), lambda s, c: (s, 0, 0, 0))
    state_shape = jax.ShapeDtypeStruct((n_streams, N_HEADS, HEAD_DK, HEAD_DK), F32)
    ga2, gb2 = ga.reshape(1, HEAD_DK), gb.reshape(1, HEAD_DK)
    if fused:
        src = (src[0], src[1].reshape(1, -1), src[2])
        src_specs = [rows(src[0]), full(src[1]), full(src[2])]
    else:
        src_specs = [rows(src[0])]
    consts = (lb_logits, ga2, gb2, sel, masks, dmat, w_in, w_tail)
    side = [w.reshape(-1, w.shape[-1]) for w in side_cast]
    per_block = (n_streams // G) * nc // SIDE_BLOCKS if side else 1
    side_spec = lambda w: pl.BlockSpec((w.shape[0] // SIDE_BLOCKS, w.shape[1]),
                                       lambda s, c: ((s * nc + c) // per_block, 0))
    outs = pl.pallas_call(
        functools.partial(_mixer0_body, L=L, layer_slot=layer_slot, g_chunk=g_chunk, fused=fused,
                          n_side=len(side), side_every=per_block),
        grid=(n_streams // G, nc),
        in_specs=src_specs
                 + [pl.BlockSpec((L, HEAD_DK), lambda s, c: (c, 0)), pl.BlockSpec((L, HEAD_DK), lambda s, c: (c, 0))]
                 + [full(a) for a in consts] + [state_spec, state_spec] + [side_spec(w) for w in side],
        out_specs=[pl.BlockSpec((G * L, 2 * KW), lambda s, c: (s * nc + c, 0)), state_spec, state_spec]
                  + [side_spec(w) for w in side],
        out_shape=[jax.ShapeDtypeStruct((n_streams * T, 2 * KW), BF16), state_shape, state_shape]
                  + [jax.ShapeDtypeStruct(w.shape, BF16) for w in side],
        scratch_shapes=[pltpu.VMEM((G * N_HEADS, HEAD_DK, HEAD_DK), F32),
                        pltpu.VMEM((G * N_HEADS, HEAD_DK, HEAD_DK), F32)],
        compiler_params=pltpu.CompilerParams(dimension_semantics=("arbitrary", "arbitrary"),
                                             vmem_limit_bytes=VMEM_LIMIT_BYTES),
        name="mixer0_T%d" % T,
    )(*src, cosf, sinf, *consts, sa0, sb0, *side)
    return outs[0], outs[1], outs[2], [o.reshape(w.shape) for o, w in zip(outs[3:], side_cast)]


def _mixer1_body(*refs, L, fused, zero_init, single_chunk):
    it = iter(refs)
    take = lambda n: [next(it) for _ in range(n)]
    src = take(3 if fused else 1)
    (cw_ref, cb_ref, wq_ref, wk_ref, wv_ref, wgate_ref, bgate_ref, tri_ref, gn_ref, skip_ref) = take(10)
    init = None if zero_init else take(4)
    hg_ref, c_ref, n_ref, m_ref, conv_ref = take(5)
    xbuf = take(1)[0]
    c = pl.program_id(1)
    last = pl.num_programs(1) - 1
    n_taps = C_CONV - 1
    G = c_ref.shape[0]
    kv_state = zero_init

    @pl.when(c == 0)
    def _():
        if zero_init:
            c_ref[...] = jnp.zeros_like(c_ref)
            n_ref[...] = jnp.zeros_like(n_ref)
            m_ref[...] = jnp.zeros_like(m_ref)
            xbuf[:, 0:CONV_HDR, :] = jnp.zeros((G, CONV_HDR, C_INNER), F32)
        else:
            c0_ref, n0_ref, m0_ref, conv0_ref = init
            if not single_chunk:
                c_ref[...] = c0_ref[...]
                n_ref[...] = n0_ref[...]
                m_ref[...] = m0_ref[...]
            xbuf[:, CONV_HDR - n_taps:CONV_HDR, :] = conv0_ref[...]

    c_in, n_in, m_in = init[:3] if single_chunk and not zero_init else (c_ref, n_ref, m_ref)

    if fused:
        x_ref, g_ref, w_ref = src
        hb = _rms(x_ref[...], g_ref[...]).astype(BF16)
        xm = jnp.dot(hb, w_ref[:, :C_INNER], preferred_element_type=F32)
        zg = jnp.dot(hb, w_ref[:, C_INNER:], preferred_element_type=F32)
    else:
        xm = src[0][:, :C_INNER]
        zg = src[0][:, C_INNER:]

    xcs, tails = [], []
    for gi in range(G):
        xg = xm[gi * L:(gi + 1) * L]
        xbuf[gi, CONV_HDR:CONV_HDR + L, :] = xg
        acc = cb_ref[...] + xg * cw_ref[n_taps:n_taps + 1, :]
        for w in range(n_taps):
            acc = acc + xbuf[gi, CONV_HDR - n_taps + w:CONV_HDR - n_taps + w + L, :] * cw_ref[w:w + 1, :]
        tails.append(xbuf[gi, CONV_HDR + L - n_taps:CONV_HDR + L, :])
        xbuf[gi, CONV_HDR - n_taps:CONV_HDR, :] = tails[gi]
        xcs.append(acc)
    xc = _silu(jnp.concatenate(xcs, axis=0) if G > 1 else xcs[0])

    xcb = xc.astype(BF16)
    xmb = xm.astype(BF16)
    qs, ks, vs = [], [], []
    for g in range(C_INNER // BD_GROUP):
        gs = slice(g * BD_GROUP, (g + 1) * BD_GROUP)
        qs.append(jnp.dot(xcb[:, gs], wq_ref[g], preferred_element_type=F32))
        ks.append(jnp.dot(xcb[:, gs], wk_ref[g], preferred_element_type=F32))
        vs.append(jnp.dot(xmb[:, gs], wv_ref[g], preferred_element_type=F32))
    q = jnp.concatenate(qs, axis=1).astype(BF16)
    k = (jnp.concatenate(ks, axis=1) * (C_DH ** -0.5)).astype(BF16)
    v = jnp.concatenate(vs, axis=1).astype(BF16)

    gates = (_dot(q, wgate_ref[0:C_INNER, :]) + _dot(k, wgate_ref[C_INNER:2 * C_INNER, :])
             + _dot(v, wgate_ref[2 * C_INNER:3 * C_INNER, :]) + bgate_ref[...])
    bcum = _exact_dot(tri_ref[...], _log_sigmoid(gates))
    lane = lax.broadcasted_iota(jnp.int32, gates.shape, 1)
    rows = jnp.where(lane < N_HEADS, gates, bcum).T
    ti = lax.broadcasted_iota(jnp.int32, (L, L), 0)
    si = lax.broadcasted_iota(jnp.int32, (L, L), 1)
    causal = si <= ti

    units = []
    for gi, hh in [(gi, hh) for gi in range(G) for hh in range(N_HEADS)]:
        hs = slice(hh * C_DH, (hh + 1) * C_DH)
        rs = slice(gi * L, (gi + 1) * L)
        b_col = bcum[rs, N_HEADS + hh:N_HEADS + hh + 1]
        i_col = gates[rs, hh:hh + 1]
        b_row = rows[N_HEADS + hh:N_HEADS + hh + 1, rs]
        i_row = rows[hh:hh + 1, rs]
        m_prev = m_in[gi, :, hh:hh + 1]
        lw = jnp.where(causal, b_col - b_row + i_row, -jnp.inf)
        lp = b_col + m_prev
        mj = jnp.maximum(lp, jnp.max(lw, axis=-1, keepdims=True))
        m_new = mj[L - 1:L, :]
        b_last = b_col[L - 1:L, :]
        units.append(dict(
            gi=gi, hh=hh, hs=hs, rs=rs, mj=mj, m_new=m_new, wgt=jnp.exp(lw - mj), wp=jnp.exp(lp - mj),
            ws=jnp.exp(b_last - b_col + i_col - m_new),
            wpl=jnp.exp(b_last + m_prev - m_new),
            cm=c_in[gi, hh],
            nv=n_in[gi, hh:hh + 1, :],
            q=q[rs, hs], k=k[rs, hs], v=v[rs, hs]))

    scores = [_dot_nt(u['q'], u['k']) * u['wgt'] for u in units]
    inter = [_dot(u['q'], u['cm']) if kv_state else _dot_nt(u['q'], u['cm']) for u in units]
    cells = []
    for u, s, it_ in zip(units, scores, inter):
        num = _dot(s, u['v']) + u['wp'] * it_
        den = jnp.sum(s, axis=-1, keepdims=True) + u['wp'] * jnp.sum(u['q'] * u['nv'], axis=-1, keepdims=True)
        cells.append(num / jnp.maximum(jnp.abs(den), jnp.exp(-u['mj'])))
    for u in units:
        gi, hh, vw = u['gi'], u['hh'], u['v'] * u['ws']
        c_ref[gi, hh] = u['wpl'] * u['cm'] + (_dot_tn(u['k'], vw) if kv_state else _dot_tn(vw, u['k']))
        n_ref[gi, hh:hh + 1, :] = u['wpl'] * u['nv'] + jnp.sum(u['k'] * u['ws'], axis=0, keepdims=True)
        m_ref[gi, :, hh:hh + 1] = u['m_new']
    for u, hcell in zip(units, cells):
        hs, rs = u['hs'], u['rs']
        hc = hcell - jnp.mean(hcell, axis=-1, keepdims=True)
        hn = hc * lax.rsqrt(jnp.mean(hc * hc, axis=-1, keepdims=True) + EPS) * gn_ref[...]
        out = (hn + skip_ref[:, hs] * xc[rs, hs]) * _silu(zg[rs, hs])
        hg_ref[rs, hs] = out.astype(hg_ref.dtype)

    @pl.when(c == last)
    def _():
        for gi in range(G):
            conv_ref[gi] = tails[gi]
            if kv_state:
                for hh in range(N_HEADS):
                    c_ref[gi, hh] = c_ref[gi, hh].T


def _mixer1(src, n_streams, T, states, cw, cb, wq, wk, wv, wgate, bgate, gn, skip):
    L = min(CHUNK1, T)
    nc = T // L
    G = max(1, min(n_streams, STEP_ROWS1 // L)) if nc == 1 else 1
    fused = len(src) == 3
    zero_init = states is None
    tri = jnp.asarray(np.kron(np.eye(G, dtype=np.float32), np.tril(np.ones((L, L), np.float32))), BF16)
    full = lambda a: pl.BlockSpec(a.shape, lambda s, c: (0,) * a.ndim, pipeline_mode=pl.Buffered(1))
    rows = lambda a: pl.BlockSpec((G * L, a.shape[1]), lambda s, c: (s * nc + c, 0))
    per_stream = lambda shape: pl.BlockSpec((G,) + shape[1:], lambda s, c: (s,) + (0,) * (len(shape) - 1))
    gn2, skip2, cb2 = gn.reshape(1, C_DH), skip.reshape(1, C_INNER), cb.reshape(1, C_INNER)
    state_shapes = ((n_streams, N_HEADS, C_DH, C_DH), (n_streams, N_HEADS, C_DH), (n_streams, 1, N_HEADS),
                    (n_streams, C_CONV - 1, C_INNER))
    scratch = [pltpu.VMEM((G, CONV_HDR + L, C_INNER), F32)]
    if fused:
        src = (src[0], src[1].reshape(1, -1), src[2])
        src_specs = [rows(src[0]), full(src[1]), full(src[2])]
    else:
        src_specs = [rows(src[0])]
    consts = (cw, cb2, wq, wk, wv, wgate, bgate, tri, gn2, skip2)
    if not zero_init:
        states = tuple(a.reshape(shape) for a, shape in zip(states, state_shapes))
    hg, c1, n1, m1, conv1 = pl.pallas_call(
        functools.partial(_mixer1_body, L=L, fused=fused, zero_init=zero_init, single_chunk=nc == 1),
        grid=(n_streams // G, nc),
        in_specs=src_specs + [full(a) for a in consts]
                 + ([] if zero_init else [per_stream(shape) for shape in state_shapes]),
        out_specs=[pl.BlockSpec((G * L, C_INNER), lambda s, c: (s * nc + c, 0))]
                  + [per_stream(shape) for shape in state_shapes],
        out_shape=[jax.ShapeDtypeStruct((n_streams * T, C_INNER), BF16)]
                  + [jax.ShapeDtypeStruct(shape, F32) for shape in state_shapes],
        scratch_shapes=scratch,
        compiler_params=pltpu.CompilerParams(dimension_semantics=("arbitrary", "arbitrary"),
                                             vmem_limit_bytes=VMEM_LIMIT_BYTES),
        name="mixer1_T%d" % T,
    )(*src, *consts, *(() if zero_init else states))
    return hg, c1, n1, m1.reshape(n_streams, N_HEADS), conv1


def _dense_blockdiag(w):
    rows = w.reshape(-1, BD_GROUP, C_BLOCK)
    tiled = jnp.tile(rows, (1, 1, BD_GROUP // C_BLOCK))
    idx = np.arange(BD_GROUP) // C_BLOCK
    same_block = jnp.asarray(idx[:, None] == idx[None, :])
    return jnp.where(same_block, tiled, 0.0).astype(BF16)


def kernel(x_prompt, x_sample, state_hgrn, state_ret, state_mlstm_c, state_mlstm_n, state_mlstm_m, state_conv,
           norm_mix, norm_ffn, norm_final, w_in0, lb_logits, hgrn_norm, ret_norm, w_out0,
           w_up1, conv_w, conv_b, w_q1, w_k1, w_v1, w_ig, b_ig, w_fg, b_fg, mlstm_norm, mlstm_skip, w_down1,
           w_ffn_gate, w_ffn_up, w_ffn_down):
    bp, tp, d = x_prompt.shape
    bs, ts, _ = x_sample.shape
    mp, ms = bp * tp, bs * ts
    zeros = lambda *shape: jnp.zeros(shape, F32)
    w_in, = _cast_weights([w_in0[0]])
    later_weights = [w_out0[0], w_up1[0], w_down1[0], w_ffn_gate, w_ffn_up, w_ffn_down]

    xp = x_prompt.reshape(mp, d)
    xs = x_sample.reshape(ms, d)

    mix_p, hg_p, rt_p, (w_out, w_up, w_down, wf_gate, wf_up, wf_down) = _mixer0(
        (xp, norm_mix[0], w_in), bp, tp, 0, zeros(bp, N_HEADS, HEAD_DK, HEAD_DK),
        zeros(bp, N_HEADS, HEAD_DK, HEAD_DK), lb_logits, hgrn_norm[0], ret_norm[0], 0, side_cast=later_weights)
    ffn0 = (w_out, norm_ffn[0], wf_gate, wf_up, wf_down, 0, norm_final, False)
    mix_s, hg_s, rt_s, _ = _mixer0((_norm_matmul(xs, norm_mix[0], w_in),), bs, ts, PAST_LEN, state_hgrn[0],
                                   state_ret[0], lb_logits, hgrn_norm[0], ret_norm[0], 0)
    xp, xs = _proj_ffn(xp, mix_p, xs, mix_s, *ffn0)

    ffn1 = (w_down, norm_ffn[1], wf_gate, wf_up, wf_down, 1, norm_final, True)
    wgate = jnp.pad(jnp.concatenate([w_ig[0], w_fg[0]], axis=1), ((0, 0), (0, GATE_LANES - 2 * N_HEADS))).astype(BF16)
    bgate = jnp.pad(jnp.concatenate([b_ig[0], b_fg[0]]), (0, GATE_LANES - 2 * N_HEADS)).reshape(1, GATE_LANES)
    m1_consts = (conv_w[0], conv_b[0], _dense_blockdiag(w_q1[0]), _dense_blockdiag(w_k1[0]),
                 _dense_blockdiag(w_v1[0]), wgate, bgate, mlstm_norm[0], mlstm_skip[0])
    hg1_p, mc_p, mn_p, mm_p, cv_p = _mixer1((xp, norm_mix[1], w_up), bp, tp, None, *m1_consts)
    hg1_s, mc_s, mn_s, mm_s, cv_s = _mixer1((_norm_matmul(xs, norm_mix[1], w_up),), bs, ts,
                                            (state_mlstm_c[0], state_mlstm_n[0], state_mlstm_m[0], state_conv[0]),
                                            *m1_consts)
    yp, ys = _proj_ffn(xp, hg1_p, xs, hg1_s, *ffn1)

    lead = lambda a: a[None]
    return (yp.reshape(bp, tp, d), ys.reshape(bs, ts, d),
            lead(hg_p), lead(hg_s), lead(rt_p), lead(rt_s),
            lead(mc_p), lead(mc_s), lead(mn_p), lead(mn_s), lead(mm_p), lead(mm_s), lead(cv_p), lead(cv_s))
```

```python
import functools

import numpy as np
import jax
import jax.numpy as jnp
from jax import lax
from jax.experimental import pallas as pl
from jax.experimental.pallas import tpu as pltpu

F32 = jnp.float32
BF16 = jnp.bfloat16

D_MODEL = 1024
CHUNK0 = 256
CHUNK1 = 256
STEP_ROWS0 = 128
STEP_ROWS1 = 32
SEL_BLOCK = 64
HGRN_SUB = 128
PHASE_HEADS = 2
EPS = 1e-6
SUBLANES = 8
N_HEADS = 4
HEAD_DK = 128
KW = N_HEADS * HEAD_DK
IN0_WIDTH = 8 * KW
ROPE_BASE = 10000.0
C_INNER = 2 * D_MODEL
C_DH = C_INNER // N_HEADS
C_CONV = 4
C_BLOCK = 4
BD_GROUP = 256
PAST_LEN = 2048
GATE_LANES = 128
CONV_HDR = 8
VMEM_LIMIT_BYTES = 56 * 1024 * 1024
ROW_TILE = 512
CAST_STEPS = 4
SIDE_BLOCKS = 32


def _dot(a, b):
    return jnp.dot(a.astype(BF16), b.astype(BF16), preferred_element_type=F32)


def _dot_nt(a, b):
    return lax.dot_general(a.astype(BF16), b.astype(BF16), (((1,), (1,)), ((), ())),
                           preferred_element_type=F32)


def _dot_tn(a, b):
    return lax.dot_general(a.astype(BF16), b.astype(BF16), (((0,), (0,)), ((), ())),
                           preferred_element_type=F32)


def _exact_dot(sel, x):
    hi = x.astype(BF16)
    lo = (x - hi.astype(F32)).astype(BF16)
    n = x.shape[1]
    both = jnp.dot(sel, jnp.concatenate([hi, lo], axis=1), preferred_element_type=F32)
    return both[:, :n] + both[:, n:]


def _rms(x, g):
    return x * lax.rsqrt(jnp.mean(x * x, axis=-1, keepdims=True) + EPS) * g


def _sigmoid(x):
    return 1.0 / (1.0 + jnp.exp(-x))


def _silu(x):
    return x * _sigmoid(x)


def _log_sigmoid(x):
    return jnp.minimum(x, 0.0) - jnp.log(1.0 + jnp.exp(-jnp.abs(x)))


def _norm_matmul_body(x_ref, g_ref, w_ref, o_ref):
    h = _rms(x_ref[...], g_ref[...])
    o_ref[...] = jnp.dot(h.astype(BF16), w_ref[...], preferred_element_type=F32)


def _norm_matmul(x, g, w):
    m, d = x.shape
    n = w.shape[1]
    return pl.pallas_call(
        _norm_matmul_body,
        grid=(m // ROW_TILE,),
        in_specs=[pl.BlockSpec((ROW_TILE, d), lambda i: (i, 0)),
                  pl.BlockSpec((1, d), lambda i: (0, 0)),
                  pl.BlockSpec((d, n), lambda i: (0, 0), pipeline_mode=pl.Buffered(1))],
        out_specs=pl.BlockSpec((ROW_TILE, n), lambda i: (i, 0)),
        out_shape=jax.ShapeDtypeStruct((m, n), F32),
        compiler_params=pltpu.CompilerParams(dimension_semantics=("arbitrary",),
                                             vmem_limit_bytes=VMEM_LIMIT_BYTES),
        name="norm_matmul",
    )(x, g.reshape(1, d), w)


def _cast_weights_body(*refs, n):
    ins, outs = refs[:n], refs[n:]
    n_slot = IN0_WIDTH // KW
    for hh in range(N_HEADS):
        for j in range(n_slot):
            dst = slice((hh * n_slot + j) * HEAD_DK, (hh * n_slot + j + 1) * HEAD_DK)
            src = slice((j * N_HEADS + hh) * HEAD_DK, (j * N_HEADS + hh + 1) * HEAD_DK)
            outs[0][:, dst] = ins[0][:, src].astype(BF16)
    for i_ref, o_ref in zip(ins[1:], outs[1:]):
        o_ref[...] = i_ref[...].astype(BF16)


def _cast_weights(ws):
    flat = [w.reshape(-1, w.shape[-1]) for w in ws]
    spec = lambda w: pl.BlockSpec((w.shape[0] // CAST_STEPS, w.shape[1]), lambda i: (i, 0))
    out = pl.pallas_call(
        functools.partial(_cast_weights_body, n=len(flat)),
        grid=(CAST_STEPS,),
        in_specs=[spec(w) for w in flat],
        out_specs=[spec(w) for w in flat],
        out_shape=[jax.ShapeDtypeStruct(w.shape, BF16) for w in flat],
        compiler_params=pltpu.CompilerParams(dimension_semantics=("arbitrary",),
                                             vmem_limit_bytes=VMEM_LIMIT_BYTES),
        name="cast_weights",
    )(*flat)
    return [o.reshape(w.shape) for o, w in zip(out, ws)]


def _proj_ffn_body(x_ref, a_ref, wo_ref, g_ref, wg_ref, wu_ref, wd_ref, gf_ref, o_ref, *, final_norm):
    x1 = x_ref[...] + jnp.dot(a_ref[...], wo_ref[...], preferred_element_type=F32)
    h = _rms(x1, g_ref[...]).astype(BF16)
    gate = jnp.dot(h, wg_ref[...], preferred_element_type=F32)
    up = jnp.dot(h, wu_ref[...], preferred_element_type=F32)
    t = (_silu(gate) * up).astype(BF16)
    x2 = x1 + jnp.dot(t, wd_ref[...], preferred_element_type=F32)
    if final_norm:
        x2 = _rms(x2, gf_ref[...])
    o_ref[...] = x2


def _proj_ffn_pair_body(xa_ref, aa_ref, xb_ref, ab_ref, wo_ref, g_ref, wg_ref, wu_ref, wd_ref, gf_ref,
                        oa_ref, ob_ref, *, final_norm, n_a):
    i = pl.program_id(0)
    shared = (wo_ref, g_ref, wg_ref, wu_ref, wd_ref, gf_ref)

    @pl.when(i < n_a)
    def _():
        _proj_ffn_body(xa_ref, aa_ref, *shared, oa_ref, final_norm=final_norm)

    @pl.when(i >= n_a)
    def _():
        _proj_ffn_body(xb_ref, ab_ref, *shared, ob_ref, final_norm=final_norm)


def _proj_ffn(xa, aa, xb, ab, wo, g, wg, wu, wd, layer, gf, final_norm):
    d = xa.shape[1]
    ka = aa.shape[1]
    ff = wg.shape[2]
    n_a, n_b = xa.shape[0] // ROW_TILE, xb.shape[0] // ROW_TILE
    const = lambda shape: pl.BlockSpec(shape, lambda i: (0, 0), pipeline_mode=pl.Buffered(1))
    of_layer = lambda shape: pl.BlockSpec((None,) + shape, lambda i: (layer, 0, 0), pipeline_mode=pl.Buffered(1))
    rows_a = lambda w: pl.BlockSpec((ROW_TILE, w), lambda i: (jnp.minimum(i, n_a - 1), 0))
    rows_b = lambda w: pl.BlockSpec((ROW_TILE, w), lambda i: (jnp.maximum(i - n_a, 0), 0))
    return pl.pallas_call(
        functools.partial(_proj_ffn_pair_body, final_norm=final_norm, n_a=n_a),
        grid=(n_a + n_b,),
        in_specs=[rows_a(d), rows_a(ka), rows_b(d), rows_b(ka),
                  const((ka, d)), const((1, d)), of_layer((d, ff)), of_layer((d, ff)), of_layer((ff, d)),
                  const((1, d))],
        out_specs=[rows_a(d), rows_b(d)],
        out_shape=[jax.ShapeDtypeStruct(xa.shape, F32), jax.ShapeDtypeStruct(xb.shape, F32)],
        compiler_params=pltpu.CompilerParams(dimension_semantics=("arbitrary",),
                                             vmem_limit_bytes=VMEM_LIMIT_BYTES),
        name="proj_ffn",
    )(xa, aa, xb, ab, wo, g.reshape(1, d), wg, wu, wd, gf.reshape(1, d))


def _hgrn_consts(L):
    r = np.arange(L)[:, None]
    t = np.arange(L)[None, :]
    sels = [t <= r]
    masks = [r == t]
    h = L // 2
    while h >= 1:
        base = (r // (2 * h)) * (2 * h)
        upper = (r % (2 * h)) >= h
        if h < SUBLANES:
            sels.append(np.where(upper, (t >= base + h) & (t <= r), (t > r) & (t <= base + h - 1)))
        masks.append((r // (2 * h) == t // (2 * h)) & upper & ((t % (2 * h)) < h))
        h //= 2
    return np.concatenate(sels, axis=0).astype(np.float32), np.stack(masks).astype(np.float32)


def _decay_exponents(sel_ref, logf, L):
    blk = min(SEL_BLOCK, L)
    nb = L // blk
    res = [_exact_dot(sel_ref[...], logf[i * blk:(i + 1) * blk]) for i in range(nb)]
    cum = [r[0:blk] for r in res]
    pre = [jnp.zeros_like(cum[0][0:1])]
    for i in range(nb):
        pre.append(pre[i] + cum[i][blk - 1:blk])
    cat = lambda parts: jnp.concatenate(parts, axis=0) if len(parts) > 1 else parts[0]
    b = cat([cum[i] + pre[i] for i in range(nb)])
    levels = []
    nbh = nb // 2
    while nbh >= 1:
        parts = []
        for i in range(nb):
            ref = pre[(i // (2 * nbh)) * 2 * nbh + nbh]
            if (i % (2 * nbh)) >= nbh:
                parts.append(cum[i] + (pre[i] - ref))
            else:
                parts.append((ref - pre[i]) - cum[i])
        levels.append(cat(parts))
        nbh //= 2
    h = blk // 2
    while h >= SUBLANES:
        parts = []
        for i in range(nb):
            for base in range(0, blk, 2 * h):
                ref = cum[i][base + h - 1:base + h]
                parts += [ref - cum[i][base:base + h], cum[i][base + h:base + 2 * h] - ref]
        levels.append(cat(parts))
        h //= 2
    for lev in range(sel_ref.shape[0] // blk - 1):
        levels.append(cat([r[(lev + 1) * blk:(lev + 2) * blk] for r in res]))
    return b, pre[nb], levels


def _retention_consts(L):
    lg = np.log1p(-np.exp2(-5.0 - np.arange(N_HEADS, dtype=np.float32))).astype(np.float32)
    idx = np.arange(L, dtype=np.float32)
    rel = idx[:, None] - idx[None, :]
    dmat = np.where(rel >= 0, np.exp(lg[:, None, None] * np.maximum(rel, 0.0)), 0.0)
    w_in = np.exp(lg[:, None] * (idx + 1.0))
    w_tail = np.exp(lg[:, None] * (L - 1.0 - idx))
    bcast = lambda v: jnp.asarray(np.broadcast_to(v[:, :, None], (N_HEADS, L, HEAD_DK)).astype(np.float32))
    g_chunk = [float(v) for v in np.exp(lg * L).astype(np.float32)]
    return jnp.asarray(dmat.astype(np.float32)), bcast(w_in), bcast(w_tail), g_chunk


def _rope_tables(pos0, T):
    half = HEAD_DK // 2
    inv = ROPE_BASE ** (-jnp.arange(half, dtype=F32) / half)
    ang = (jnp.arange(T) + pos0).astype(F32)[:, None] * inv[None, :]
    cos, sin = jnp.cos(ang), jnp.sin(ang)
    return jnp.concatenate([cos, cos], axis=-1), jnp.concatenate([-sin, sin], axis=-1)


def _hgrn_intra(units, mask_ref, L, fillers=()):
    sub = min(L, HGRN_SUB)
    n_sub = L // sub
    n_lev = mask_ref.shape[0] - 1
    top = len(units[0][3]) - n_lev
    assert n_sub in (1, 2) and top == n_sub - 1
    pieces = [(u, slice(i * sub, (i + 1) * sub)) for u in range(len(units)) for i in range(n_sub)]
    atts = [mask_ref[0] * _dot_nt(units[u][0][r], units[u][1][r]) for u, r in pieces]
    fillers = list(fillers)
    every = max(1, n_lev // (len(fillers) + 1))
    for lev in range(n_lev):
        for p, (u, r) in enumerate(pieces):
            q, k, _, expo = units[u]
            sc = jnp.exp(expo[top + lev][r])
            atts[p] = atts[p] + mask_ref[lev + 1] * _dot_nt(q[r] * sc, k[r] * sc)
        if fillers and (lev + 1) % every == 0:
            fillers.pop(0)()
    while fillers:
        fillers.pop(0)()
    if n_sub == 1:
        return [_dot(att, units[u][2]) for att, (u, _) in zip(atts, pieces)]
    outs = []
    for u, (q, k, v, expo) in enumerate(units):
        cross = _dot_nt(q[sub:] * jnp.exp(expo[0][sub:]), k[:sub] * jnp.exp(expo[0][:sub]))
        outs.append(jnp.concatenate([_dot(atts[2 * u], v[:sub]),
                                     _dot(jnp.concatenate([cross, atts[2 * u + 1]], axis=1), v)], axis=0))
    return outs


def _mixer0_body(*refs, L, layer_slot, g_chunk, fused, n_side, side_every):
    it = iter(refs)
    take = lambda n: [next(it) for _ in range(n)]
    src = take(3 if fused else 1)
    (cos_ref, sin_ref, lbl_ref, ga_ref, gb_ref, sel_ref, mask_ref, dmat_ref, win_ref, wtail_ref,
     sa0_ref, sb0_ref) = take(12)
    side_in = take(n_side)
    mix_ref, sa_ref, sb_ref = take(3)
    side_out = take(n_side)
    sat_scr, sbt_scr = take(2)
    c = pl.program_id(1)
    last = pl.num_programs(1) - 1
    head_w = IN0_WIDTH // N_HEADS
    G = sa_ref.shape[0]

    if n_side:
        @pl.when((pl.program_id(0) * pl.num_programs(1) + c) % side_every == 0)
        def _():
            for i_ref, o_ref in zip(side_in, side_out):
                o_ref[...] = i_ref[...].astype(BF16)

    @pl.when(c == 0)
    def _():
        for gi in range(G):
            for hh in range(N_HEADS):
                sat_scr[gi * N_HEADS + hh] = sa0_ref[gi, hh].T
                sbt_scr[gi * N_HEADS + hh] = sb0_ref[gi, hh].T

    lbl = lbl_ref[...]
    e = jnp.exp(lbl - jnp.max(lbl, axis=0, keepdims=True))
    lb_all = jnp.sum(e[:layer_slot + 1], axis=0, keepdims=True) / jnp.sum(e, axis=0, keepdims=True)

    if fused:
        x_ref, g_ref, w_ref = src
        hb = _rms(x_ref[...], g_ref[...]).astype(BF16)
    cosf = cos_ref[...]
    sinf = sin_ref[...]

    def project(hh):
        hcols = slice(hh * head_w, (hh + 1) * head_w)
        return jnp.dot(hb, w_ref[:, hcols], preferred_element_type=F32) if fused else src[0][:, hcols]

    def run_phase(heads, z_of, fillers):
        hg_units, ret_units = [], []
        for hh in heads:
            hs = slice(hh * HEAD_DK, (hh + 1) * HEAD_DK)
            lb = lb_all[:, hs]
            for gi in range(G):
                rs = slice(gi * L, (gi + 1) * L)
                si = gi * N_HEADS + hh
                zh = z_of[hh][rs]
                part = lambda j, zh=zh: zh[:, j * HEAD_DK:(j + 1) * HEAD_DK]
                f = lb + (1.0 - lb) * _sigmoid(part(1))
                b, b_last, expo = _decay_exponents(sel_ref, jnp.log(f), L)
                hg_units.append((rs, hs, si, part(0), 1.0 - f, part(2), part(3), b, b_last, expo))
                rq = part(4)
                rk = part(5)
                rq = rq * cosf + pltpu.roll(rq, HEAD_DK // 2, 1) * sinf
                rk = (rk * cosf + pltpu.roll(rk, HEAD_DK // 2, 1) * sinf) * (HEAD_DK ** -0.5)
                ret_units.append((rs, hh, si, rq, rk, part(6), part(7)))

        states = [sat_scr[u[2]] for u in hg_units]
        inter = [_dot_nt(u[3] * jnp.exp(u[7]), st) for u, st in zip(hg_units, states)]
        intra = _hgrn_intra([(u[3], u[4], u[5], u[9]) for u in hg_units], mask_ref, L, fillers)
        for (rs, hs, si, q, k, v, og, b, b_last, _), st, o1, o2 in zip(hg_units, states, inter, intra):
            sat_scr[si] = st * jnp.exp(b_last) + _dot_tn(v, k * jnp.exp(b_last - b))
            o = _rms((o1 + o2) * _sigmoid(og), ga_ref[...])
            mix_ref[rs, hs] = o.astype(mix_ref.dtype)

        states = [sbt_scr[si] for _, _, si, _, _, _, _ in ret_units]
        scores = [_dot_nt(rq, rk) * dmat_ref[hh] for _, hh, _, rq, rk, _, _ in ret_units]
        inter = [_dot_nt(u[3], rt) for u, rt in zip(ret_units, states)]
        outs = [_dot(sc, u[5]) + win_ref[u[1]] * it_ for u, sc, it_ in zip(ret_units, scores, inter)]
        for (rs, hh, si, rq, rk, rv, rg), rt, ro in zip(ret_units, states, outs):
            sbt_scr[si] = g_chunk[hh] * rt + _dot_tn(rv, rk * wtail_ref[hh])
            ro = _rms(ro, gb_ref[...]) * _silu(rg)
            mix_ref[rs, KW + hh * HEAD_DK:KW + (hh + 1) * HEAD_DK] = ro.astype(mix_ref.dtype)

    if fused:
        per = PHASE_HEADS
        z_of = {hh: project(hh) for hh in range(per)}
        for h0 in range(0, N_HEADS, per):
            nxt = range(h0 + per, min(h0 + 2 * per, N_HEADS))
            run_phase(range(h0, h0 + per), z_of, [lambda hh=hh: z_of.__setitem__(hh, project(hh)) for hh in nxt])
    else:
        run_phase(range(N_HEADS), {hh: project(hh) for hh in range(N_HEADS)}, [])

    @pl.when(c == last)
    def _():
        for gi in range(G):
            for hh in range(N_HEADS):
                sa_ref[gi, hh] = sat_scr[gi * N_HEADS + hh].T
                sb_ref[gi, hh] = sbt_scr[gi * N_HEADS + hh].T


def _mixer0(src, n_streams, T, pos0, sa0, sb0, lb_logits, ga, gb, layer_slot, side_cast=()):
    L = min(CHUNK0, T)
    nc = T // L
    G = max(1, min(n_streams, STEP_ROWS0 // L)) if nc == 1 else 1
    fused = len(src) == 3
    sel = jnp.asarray(_hgrn_consts(min(SEL_BLOCK, L))[0], BF16)
    masks = jnp.asarray(_hgrn_consts(min(HGRN_SUB, L))[1])
    dmat, w_in, w_tail, g_chunk = _retention_consts(L)
    cosf, sinf = _rope_tables(pos0, T)
    full = lambda a: pl.BlockSpec(a.shape, lambda s, c: (0,) * a.ndim, pipeline_mode=pl.Buffered(1))
    rows = lambda a: pl.BlockSpec((G * L, a.shape[1]), lambda s, c: (s * nc + c, 0))
    state_spec = pl.BlockSpec((G, N_HEADS, HEAD_DK, HEAD_DK), lambda s, c: (s, 0, 0, 0))
    state_shape = jax.ShapeDtypeStruct((n_streams, N_HEADS, HEAD_DK, HEAD_DK), F32)
    ga2, gb2 = ga.reshape(1, HEAD_DK), gb.reshape(1, HEAD_DK)
    if fused:
        src = (src[0], src[1].reshape(1, -1), src[2])
        src_specs = [rows(src[0]), full(src[1]), full(src[2])]
    else:
        src_specs = [rows(src[0])]
    consts = (lb_logits, ga2, gb2, sel, masks, dmat, w_in, w_tail)
    side = [w.reshape(-1, w.shape[-1]) for w in side_cast]
    per_block = (n_streams // G) * nc // SIDE_BLOCKS if side else 1
    side_spec = lambda w: pl.BlockSpec((w.shape[0] // SIDE_BLOCKS, w.shape[1]),
                                       lambda s, c: ((s * nc + c) // per_block, 0))
    outs = pl.pallas_call(
        functools.partial(_mixer0_body, L=L, layer_slot=layer_slot, g_chunk=g_chunk, fused=fused,
                          n_side=len(side), side_every=per_block),
        grid=(n_streams // G, nc),
        in_specs=src_specs
                 + [pl.BlockSpec((L, HEAD_DK), lambda s, c: (c, 0)), pl.BlockSpec((L, HEAD_DK), lambda s, c: (c, 0))]
                 + [full(a) for a in consts] + [state_spec, state_spec] + [side_spec(w) for w in side],
        out_specs=[pl.BlockSpec((G * L, 2 * KW), lambda s, c: (s * nc + c, 0)), state_spec, state_spec]
                  + [side_spec(w) for w in side],
        out_shape=[jax.ShapeDtypeStruct((n_streams * T, 2 * KW), BF16), state_shape, state_shape]
                  + [jax.ShapeDtypeStruct(w.shape, BF16) for w in side],
        scratch_shapes=[pltpu.VMEM((G * N_HEADS, HEAD_DK, HEAD_DK), F32),
                        pltpu.VMEM((G * N_HEADS, HEAD_DK, HEAD_DK), F32)],
        compiler_params=pltpu.CompilerParams(dimension_semantics=("arbitrary", "arbitrary"),
                                             vmem_limit_bytes=VMEM_LIMIT_BYTES),
        name="mixer0_T%d" % T,
    )(*src, cosf, sinf, *consts, sa0, sb0, *side)
    return outs[0], outs[1], outs[2], [o.reshape(w.shape) for o, w in zip(outs[3:], side_cast)]


def _mixer1_body(*refs, L, fused, zero_init, single_chunk):
    it = iter(refs)
    take = lambda n: [next(it) for _ in range(n)]
    src = take(3 if fused else 1)
    (cw_ref, cb_ref, wq_ref, wk_ref, wv_ref, wgate_ref, bgate_ref, tri_ref, gn_ref, skip_ref) = take(10)
    init = None if zero_init else take(4)
    hg_ref, c_ref, n_ref, m_ref, conv_ref = take(5)
    xbuf = take(1)[0]
    c = pl.program_id(1)
    last = pl.num_programs(1) - 1
    n_taps = C_CONV - 1
    G = c_ref.shape[0]
    kv_state = zero_init

    @pl.when(c == 0)
    def _():
        if zero_init:
            c_ref[...] = jnp.zeros_like(c_ref)
            n_ref[...] = jnp.zeros_like(n_ref)
            m_ref[...] = jnp.zeros_like(m_ref)
            xbuf[:, 0:CONV_HDR, :] = jnp.zeros((G, CONV_HDR, C_INNER), F32)
        else:
            c0_ref, n0_ref, m0_ref, conv0_ref = init
            if not single_chunk:
                c_ref[...] = c0_ref[...]
                n_ref[...] = n0_ref[...]
                m_ref[...] = m0_ref[...]
            xbuf[:, CONV_HDR - n_taps:CONV_HDR, :] = conv0_ref[...]

    c_in, n_in, m_in = init[:3] if single_chunk and not zero_init else (c_ref, n_ref, m_ref)

    if fused:
        x_ref, g_ref, w_ref = src
        hb = _rms(x_ref[...], g_ref[...]).astype(BF16)
        xm = jnp.dot(hb, w_ref[:, :C_INNER], preferred_element_type=F32)
        zg = jnp.dot(hb, w_ref[:, C_INNER:], preferred_element_type=F32)
    else:
        xm = src[0][:, :C_INNER]
        zg = src[0][:, C_INNER:]

    xcs, tails = [], []
    for gi in range(G):
        xg = xm[gi * L:(gi + 1) * L]
        xbuf[gi, CONV_HDR:CONV_HDR + L, :] = xg
        acc = cb_ref[...] + xg * cw_ref[n_taps:n_taps + 1, :]
        for w in range(n_taps):
            acc = acc + xbuf[gi, CONV_HDR - n_taps + w:CONV_HDR - n_taps + w + L, :] * cw_ref[w:w + 1, :]
        tails.append(xbuf[gi, CONV_HDR + L - n_taps:CONV_HDR + L, :])
        xbuf[gi, CONV_HDR - n_taps:CONV_HDR, :] = tails[gi]
        xcs.append(acc)
    xc = _silu(jnp.concatenate(xcs, axis=0) if G > 1 else xcs[0])

    xcb = xc.astype(BF16)
    xmb = xm.astype(BF16)
    qs, ks, vs = [], [], []
    for g in range(C_INNER // BD_GROUP):
        gs = slice(g * BD_GROUP, (g + 1) * BD_GROUP)
        qs.append(jnp.dot(xcb[:, gs], wq_ref[g], preferred_element_type=F32))
        ks.append(jnp.dot(xcb[:, gs], wk_ref[g], preferred_element_type=F32))
        vs.append(jnp.dot(xmb[:, gs], wv_ref[g], preferred_element_type=F32))
    q = jnp.concatenate(qs, axis=1).astype(BF16)
    k = (jnp.concatenate(ks, axis=1) * (C_DH ** -0.5)).astype(BF16)
    v = jnp.concatenate(vs, axis=1).astype(BF16)

    gates = (_dot(q, wgate_ref[0:C_INNER, :]) + _dot(k, wgate_ref[C_INNER:2 * C_INNER, :])
             + _dot(v, wgate_ref[2 * C_INNER:3 * C_INNER, :]) + bgate_ref[...])
    bcum = _exact_dot(tri_ref[...], _log_sigmoid(gates))
    lane = lax.broadcasted_iota(jnp.int32, gates.shape, 1)
    rows = jnp.where(lane < N_HEADS, gates, bcum).T
    ti = lax.broadcasted_iota(jnp.int32, (L, L), 0)
    si = lax.broadcasted_iota(jnp.int32, (L, L), 1)
    causal = si <= ti

    units = []
    for gi, hh in [(gi, hh) for gi in range(G) for hh in range(N_HEADS)]:
        hs = slice(hh * C_DH, (hh + 1) * C_DH)
        rs = slice(gi * L, (gi + 1) * L)
        b_col = bcum[rs, N_HEADS + hh:N_HEADS + hh + 1]
        i_col = gates[rs, hh:hh + 1]
        b_row = rows[N_HEADS + hh:N_HEADS + hh + 1, rs]
        i_row = rows[hh:hh + 1, rs]
        m_prev = m_in[gi, :, hh:hh + 1]
        lw = jnp.where(causal, b_col - b_row + i_row, -jnp.inf)
        lp = b_col + m_prev
        mj = jnp.maximum(lp, jnp.max(lw, axis=-1, keepdims=True))
        m_new = mj[L - 1:L, :]
        b_last = b_col[L - 1:L, :]
        units.append(dict(
            gi=gi, hh=hh, hs=hs, rs=rs, mj=mj, m_new=m_new, wgt=jnp.exp(lw - mj), wp=jnp.exp(lp - mj),
            ws=jnp.exp(b_last - b_col + i_col - m_new),
            wpl=jnp.exp(b_last + m_prev - m_new),
            cm=c_in[gi, hh],
            nv=n_in[gi, hh:hh + 1, :],
            q=q[rs, hs], k=k[rs, hs], v=v[rs, hs]))

    scores = [_dot_nt(u['q'], u['k']) * u['wgt'] for u in units]
    inter = [_dot(u['q'], u['cm']) if kv_state else _dot_nt(u['q'], u['cm']) for u in units]
    cells = []
    for u, s, it_ in zip(units, scores, inter):
        num = _dot(s, u['v']) + u['wp'] * it_
        den = jnp.sum(s, axis=-1, keepdims=True) + u['wp'] * jnp.sum(u['q'] * u['nv'], axis=-1, keepdims=True)
        cells.append(num / jnp.maximum(jnp.abs(den), jnp.exp(-u['mj'])))
    for u in units:
        gi, hh, vw = u['gi'], u['hh'], u['v'] * u['ws']
        c_ref[gi, hh] = u['wpl'] * u['cm'] + (_dot_tn(u['k'], vw) if kv_state else _dot_tn(vw, u['k']))
        n_ref[gi, hh:hh + 1, :] = u['wpl'] * u['nv'] + jnp.sum(u['k'] * u['ws'], axis=0, keepdims=True)
        m_ref[gi, :, hh:hh + 1] = u['m_new']
    for u, hcell in zip(units, cells):
        hs, rs = u['hs'], u['rs']
        hc = hcell - jnp.mean(hcell, axis=-1, keepdims=True)
        hn = hc * lax.rsqrt(jnp.mean(hc * hc, axis=-1, keepdims=True) + EPS) * gn_ref[...]
        out = (hn + skip_ref[:, hs] * xc[rs, hs]) * _silu(zg[rs, hs])
        hg_ref[rs, hs] = out.astype(hg_ref.dtype)

    @pl.when(c == last)
    def _():
        for gi in range(G):
            conv_ref[gi] = tails[gi]
            if kv_state:
                for hh in range(N_HEADS):
                    c_ref[gi, hh] = c_ref[gi, hh].T


def _mixer1(src, n_streams, T, states, cw, cb, wq, wk, wv, wgate, bgate, gn, skip):
    L = min(CHUNK1, T)
    nc = T // L
    G = max(1, min(n_streams, STEP_ROWS1 // L)) if nc == 1 else 1
    fused = len(src) == 3
    zero_init = states is None
    tri = jnp.asarray(np.kron(np.eye(G, dtype=np.float32), np.tril(np.ones((L, L), np.float32))), BF16)
    full = lambda a: pl.BlockSpec(a.shape, lambda s, c: (0,) * a.ndim, pipeline_mode=pl.Buffered(1))
    rows = lambda a: pl.BlockSpec((G * L, a.shape[1]), lambda s, c: (s * nc + c, 0))
    per_stream = lambda shape: pl.BlockSpec((G,) + shape[1:], lambda s, c: (s,) + (0,) * (len(shape) - 1))
    gn2, skip2, cb2 = gn.reshape(1, C_DH), skip.reshape(1, C_INNER), cb.reshape(1, C_INNER)
    state_shapes = ((n_streams, N_HEADS, C_DH, C_DH), (n_streams, N_HEADS, C_DH), (n_streams, 1, N_HEADS),
                    (n_streams, C_CONV - 1, C_INNER))
    scratch = [pltpu.VMEM((G, CONV_HDR + L, C_INNER), F32)]
    if fused:
        src = (src[0], src[1].reshape(1, -1), src[2])
        src_specs = [rows(src[0]), full(src[1]), full(src[2])]
    else:
        src_specs = [rows(src[0])]
    consts = (cw, cb2, wq, wk, wv, wgate, bgate, tri, gn2, skip2)
    if not zero_init:
        states = tuple(a.reshape(shape) for a, shape in zip(states, state_shapes))
    hg, c1, n1, m1, conv1 = pl.pallas_call(
        functools.partial(_mixer1_body, L=L, fused=fused, zero_init=zero_init, single_chunk=nc == 1),
        grid=(n_streams // G, nc),
        in_specs=src_specs + [full(a) for a in consts]
                 + ([] if zero_init else [per_stream(shape) for shape in state_shapes]),
        out_specs=[pl.BlockSpec((G * L, C_INNER), lambda s, c: (s * nc + c, 0))]
                  + [per_stream(shape) for shape in state_shapes],
        out_shape=[jax.ShapeDtypeStruct((n_streams * T, C_INNER), BF16)]
                  + [jax.ShapeDtypeStruct(shape, F32) for shape in state_shapes],
        scratch_shapes=scratch,
        compiler_params=pltpu.CompilerParams(dimension_semantics=("arbitrary", "arbitrary"),
                                             vmem_limit_bytes=VMEM_LIMIT_BYTES),
        name="mixer1_T%d" % T,
    )(*src, *consts, *(() if zero_init else states))
    return hg, c1, n1, m1.reshape(n_streams, N_HEADS), conv1


def _dense_blockdiag(w):
    rows = w.reshape(-1, BD_GROUP, C_BLOCK)
    tiled = jnp.tile(rows, (1, 1, BD_GROUP // C_BLOCK))
    idx = np.arange(BD_GROUP) // C_BLOCK
    same_block = jnp.asarray(idx[:, None] == idx[None, :])
    return jnp.where(same_block, tiled, 0.0).astype(BF16)


def kernel(x_prompt, x_sample, state_hgrn, state_ret, state_mlstm_c, state_mlstm_n, state_mlstm_m, state_conv,
           norm_mix, norm_ffn, norm_final, w_in0, lb_logits, hgrn_norm, ret_norm, w_out0,
           w_up1, conv_w, conv_b, w_q1, w_k1, w_v1, w_ig, b_ig, w_fg, b_fg, mlstm_norm, mlstm_skip, w_down1,
           w_ffn_gate, w_ffn_up, w_ffn_down):
    bp, tp, d = x_prompt.shape
    bs, ts, _ = x_sample.shape
    mp, ms = bp * tp, bs * ts
    zeros = lambda *shape: jnp.zeros(shape, F32)
    w_in, = _cast_weights([w_in0[0]])
    later_weights = [w_out0[0], w_up1[0], w_down1[0], w_ffn_gate, w_ffn_up, w_ffn_down]

    xp = x_prompt.reshape(mp, d)
    xs = x_sample.reshape(ms, d)

    mix_p, hg_p, rt_p, (w_out, w_up, w_down, wf_gate, wf_up, wf_down) = _mixer0(
        (xp, norm_mix[0], w_in), bp, tp, 0, zeros(bp, N_HEADS, HEAD_DK, HEAD_DK),
        zeros(bp, N_HEADS, HEAD_DK, HEAD_DK), lb_logits, hgrn_norm[0], ret_norm[0], 0, side_cast=later_weights)
    ffn0 = (w_out, norm_ffn[0], wf_gate, wf_up, wf_down, 0, norm_final, False)
    mix_s, hg_s, rt_s, _ = _mixer0((_norm_matmul(xs, norm_mix[0], w_in),), bs, ts, PAST_LEN, state_hgrn[0],
                                   state_ret[0], lb_logits, hgrn_norm[0], ret_norm[0], 0)
    xp, xs = _proj_ffn(xp, mix_p, xs, mix_s, *ffn0)

    ffn1 = (w_down, norm_ffn[1], wf_gate, wf_up, wf_down, 1, norm_final, True)
    wgate = jnp.pad(jnp.concatenate([w_ig[0], w_fg[0]], axis=1), ((0, 0), (0, GATE_LANES - 2 * N_HEADS))).astype(BF16)
    bgate = jnp.pad(jnp.concatenate([b_ig[0], b_fg[0]]), (0, GATE_LANES - 2 * N_HEADS)).reshape(1, GATE_LANES)
    m1_consts = (conv_w[0], conv_b[0], _dense_blockdiag(w_q1[0]), _dense_blockdiag(w_k1[0]),
                 _dense_blockdiag(w_v1[0]), wgate, bgate, mlstm_norm[0], mlstm_skip[0])
    hg1_p, mc_p, mn_p, mm_p, cv_p = _mixer1((xp, norm_mix[1], w_up), bp, tp, None, *m1_consts)
    hg1_s, mc_s, mn_s, mm_s, cv_s = _mixer1((_norm_matmul(xs, norm_mix[1], w_up),), bs, ts,
                                            (state_mlstm_c[0], state_mlstm_n[0], state_mlstm_m[0], state_conv[0]),
                                            *m1_consts)
    yp, ys = _proj_ffn(xp, hg1_p, xs, hg1_s, *ffn1)

    lead = lambda a: a[None]
    return (yp.reshape(bp, tp, d), ys.reshape(bs, ts, d),
            lead(hg_p), lead(hg_s), lead(rt_p), lead(rt_s),
            lead(mc_p), lead(mc_s), lead(mn_p), lead(mn_s), lead(mm_p), lead(mm_s), lead(cv_p), lead(cv_s))
```

```python
import functools

import numpy as np
import jax
import jax.numpy as jnp
from jax import lax
from jax.experimental import pallas as pl
from jax.experimental.pallas import tpu as pltpu

F32 = jnp.float32
BF16 = jnp.bfloat16

D_MODEL = 1024
CHUNK0 = 256
CHUNK1 = 256
STEP_ROWS0 = 128
STEP_ROWS1 = 32
SEL_BLOCK = 64
HGRN_SUB = 128
PHASE_HEADS = 2
EPS = 1e-6
SUBLANES = 8
N_HEADS = 4
HEAD_DK = 128
KW = N_HEADS * HEAD_DK
IN0_WIDTH = 8 * KW
ROPE_BASE = 10000.0
C_INNER = 2 * D_MODEL
C_DH = C_INNER // N_HEADS
C_CONV = 4
C_BLOCK = 4
BD_GROUP = 256
PAST_LEN = 2048
GATE_LANES = 128
CONV_HDR = 8
VMEM_LIMIT_BYTES = 56 * 1024 * 1024
ROW_TILE = 512
CAST_STEPS = 4
SIDE_BLOCKS = 32


def _dot(a, b):
    return jnp.dot(a.astype(BF16), b.astype(BF16), preferred_element_type=F32)


def _dot_nt(a, b):
    return lax.dot_general(a.astype(BF16), b.astype(BF16), (((1,), (1,)), ((), ())),
                           preferred_element_type=F32)


def _dot_tn(a, b):
    return lax.dot_general(a.astype(BF16), b.astype(BF16), (((0,), (0,)), ((), ())),
                           preferred_element_type=F32)


def _exact_dot(sel, x):
    hi = x.astype(BF16)
    lo = (x - hi.astype(F32)).astype(BF16)
    n = x.shape[1]
    both = jnp.dot(sel, jnp.concatenate([hi, lo], axis=1), preferred_element_type=F32)
    return both[:, :n] + both[:, n:]


def _rms(x, g):
    return x * lax.rsqrt(jnp.mean(x * x, axis=-1, keepdims=True) + EPS) * g


def _sigmoid(x):
    return 1.0 / (1.0 + jnp.exp(-x))


def _silu(x):
    return x * _sigmoid(x)


def _log_sigmoid(x):
    return jnp.minimum(x, 0.0) - jnp.log(1.0 + jnp.exp(-jnp.abs(x)))


def _norm_matmul_body(x_ref, g_ref, w_ref, o_ref):
    h = _rms(x_ref[...], g_ref[...])
    o_ref[...] = jnp.dot(h.astype(BF16), w_ref[...], preferred_element_type=F32)


def _norm_matmul(x, g, w):
    m, d = x.shape
    n = w.shape[1]
    return pl.pallas_call(
        _norm_matmul_body,
        grid=(m // ROW_TILE,),
        in_specs=[pl.BlockSpec((ROW_TILE, d), lambda i: (i, 0)),
                  pl.BlockSpec((1, d), lambda i: (0, 0)),
                  pl.BlockSpec((d, n), lambda i: (0, 0), pipeline_mode=pl.Buffered(1))],
        out_specs=pl.BlockSpec((ROW_TILE, n), lambda i: (i, 0)),
        out_shape=jax.ShapeDtypeStruct((m, n), F32),
        compiler_params=pltpu.CompilerParams(dimension_semantics=("arbitrary",),
                                             vmem_limit_bytes=VMEM_LIMIT_BYTES),
        name="norm_matmul",
    )(x, g.reshape(1, d), w)


def _cast_weights_body(*refs, n):
    ins, outs = refs[:n], refs[n:]
    n_slot = IN0_WIDTH // KW
    for hh in range(N_HEADS):
        for j in range(n_slot):
            dst = slice((hh * n_slot + j) * HEAD_DK, (hh * n_slot + j + 1) * HEAD_DK)
            src = slice((j * N_HEADS + hh) * HEAD_DK, (j * N_HEADS + hh + 1) * HEAD_DK)
            outs[0][:, dst] = ins[0][:, src].astype(BF16)
    for i_ref, o_ref in zip(ins[1:], outs[1:]):
        o_ref[...] = i_ref[...].astype(BF16)


def _cast_weights(ws):
    flat = [w.reshape(-1, w.shape[-1]) for w in ws]
    spec = lambda w: pl.BlockSpec((w.shape[0] // CAST_STEPS, w.shape[1]), lambda i: (i, 0))
    out = pl.pallas_call(
        functools.partial(_cast_weights_body, n=len(flat)),
        grid=(CAST_STEPS,),
        in_specs=[spec(w) for w in flat],
        out_specs=[spec(w) for w in flat],
        out_shape=[jax.ShapeDtypeStruct(w.shape, BF16) for w in flat],
        compiler_params=pltpu.CompilerParams(dimension_semantics=("arbitrary",),
                                             vmem_limit_bytes=VMEM_LIMIT_BYTES),
        name="cast_weights",
    )(*flat)
    return [o.reshape(w.shape) for o, w in zip(out, ws)]


def _proj_ffn_body(x_ref, a_ref, wo_ref, g_ref, wg_ref, wu_ref, wd_ref, gf_ref, o_ref, *, final_norm):
    x1 = x_ref[...] + jnp.dot(a_ref[...], wo_ref[...], preferred_element_type=F32)
    h = _rms(x1, g_ref[...]).astype(BF16)
    gate = jnp.dot(h, wg_ref[...], preferred_element_type=F32)
    up = jnp.dot(h, wu_ref[...], preferred_element_type=F32)
    t = (_silu(gate) * up).astype(BF16)
    x2 = x1 + jnp.dot(t, wd_ref[...], preferred_element_type=F32)
    if final_norm:
        x2 = _rms(x2, gf_ref[...])
    o_ref[...] = x2


def _proj_ffn_pair_body(xa_ref, aa_ref, xb_ref, ab_ref, wo_ref, g_ref, wg_ref, wu_ref, wd_ref, gf_ref,
                        oa_ref, ob_ref, *, final_norm, n_a):
    i = pl.program_id(0)
    shared = (wo_ref, g_ref, wg_ref, wu_ref, wd_ref, gf_ref)

    @pl.when(i < n_a)
    def _():
        _proj_ffn_body(xa_ref, aa_ref, *shared, oa_ref, final_norm=final_norm)

    @pl.when(i >= n_a)
    def _():
        _proj_ffn_body(xb_ref, ab_ref, *shared, ob_ref, final_norm=final_norm)


def _proj_ffn(xa, aa, xb, ab, wo, g, wg, wu, wd, layer, gf, final_norm):
    d = xa.shape[1]
    ka = aa.shape[1]
    ff = wg.shape[2]
    n_a, n_b = xa.shape[0] // ROW_TILE, xb.shape[0] // ROW_TILE
    const = lambda shape: pl.BlockSpec(shape, lambda i: (0, 0), pipeline_mode=pl.Buffered(1))
    of_layer = lambda shape: pl.BlockSpec((None,) + shape, lambda i: (layer, 0, 0), pipeline_mode=pl.Buffered(1))
    rows_a = lambda w: pl.BlockSpec((ROW_TILE, w), lambda i: (jnp.minimum(i, n_a - 1), 0))
    rows_b = lambda w: pl.BlockSpec((ROW_TILE, w), lambda i: (jnp.maximum(i - n_a, 0), 0))
    return pl.pallas_call(
        functools.partial(_proj_ffn_pair_body, final_norm=final_norm, n_a=n_a),
        grid=(n_a + n_b,),
        in_specs=[rows_a(d), rows_a(ka), rows_b(d), rows_b(ka),
                  const((ka, d)), const((1, d)), of_layer((d, ff)), of_layer((d, ff)), of_layer((ff, d)),
                  const((1, d))],
        out_specs=[rows_a(d), rows_b(d)],
        out_shape=[jax.ShapeDtypeStruct(xa.shape, F32), jax.ShapeDtypeStruct(xb.shape, F32)],
        compiler_params=pltpu.CompilerParams(dimension_semantics=("arbitrary",),
                                             vmem_limit_bytes=VMEM_LIMIT_BYTES),
        name="proj_ffn",
    )(xa, aa, xb, ab, wo, g.reshape(1, d), wg, wu, wd, gf.reshape(1, d))


def _hgrn_consts(L):
    r = np.arange(L)[:, None]
    t = np.arange(L)[None, :]
    sels = [t <= r]
    masks = [r == t]
    h = L // 2
    while h >= 1:
        base = (r // (2 * h)) * (2 * h)
        upper = (r % (2 * h)) >= h
        if h < SUBLANES:
            sels.append(np.where(upper, (t >= base + h) & (t <= r), (t > r) & (t <= base + h - 1)))
        masks.append((r // (2 * h) == t // (2 * h)) & upper & ((t % (2 * h)) < h))
        h //= 2
    return np.concatenate(sels, axis=0).astype(np.float32), np.stack(masks).astype(np.float32)


def _decay_exponents(sel_ref, logf, L):
    blk = min(SEL_BLOCK, L)
    nb = L // blk
    res = [_exact_dot(sel_ref[...], logf[i * blk:(i + 1) * blk]) for i in range(nb)]
    cum = [r[0:blk] for r in res]
    pre = [jnp.zeros_like(cum[0][0:1])]
    for i in range(nb):
        pre.append(pre[i] + cum[i][blk - 1:blk])
    cat = lambda parts: jnp.concatenate(parts, axis=0) if len(parts) > 1 else parts[0]
    b = cat([cum[i] + pre[i] for i in range(nb)])
    levels = []
    nbh = nb // 2
    while nbh >= 1:
        parts = []
        for i in range(nb):
            ref = pre[(i // (2 * nbh)) * 2 * nbh + nbh]
            if (i % (2 * nbh)) >= nbh:
                parts.append(cum[i] + (pre[i] - ref))
            else:
                parts.append((ref - pre[i]) - cum[i])
        levels.append(cat(parts))
        nbh //= 2
    h = blk // 2
    while h >= SUBLANES:
        parts = []
        for i in range(nb):
            for base in range(0, blk, 2 * h):
                ref = cum[i][base + h - 1:base + h]
                parts += [ref - cum[i][base:base + h], cum[i][base + h:base + 2 * h] - ref]
        levels.append(cat(parts))
        h //= 2
    for lev in range(sel_ref.shape[0] // blk - 1):
        levels.append(cat([r[(lev + 1) * blk:(lev + 2) * blk] for r in res]))
    return b, pre[nb], levels


def _retention_consts(L):
    lg = np.log1p(-np.exp2(-5.0 - np.arange(N_HEADS, dtype=np.float32))).astype(np.float32)
    idx = np.arange(L, dtype=np.float32)
    rel = idx[:, None] - idx[None, :]
    dmat = np.where(rel >= 0, np.exp(lg[:, None, None] * np.maximum(rel, 0.0)), 0.0)
    w_in = np.exp(lg[:, None] * (idx + 1.0))
    w_tail = np.exp(lg[:, None] * (L - 1.0 - idx))
    bcast = lambda v: jnp.asarray(np.broadcast_to(v[:, :, None], (N_HEADS, L, HEAD_DK)).astype(np.float32))
    g_chunk = [float(v) for v in np.exp(lg * L).astype(np.float32)]
    return jnp.asarray(dmat.astype(np.float32)), bcast(w_in), bcast(w_tail), g_chunk


def _rope_tables(pos0, T):
    half = HEAD_DK // 2
    inv = ROPE_BASE ** (-jnp.arange(half, dtype=F32) / half)
    ang = (jnp.arange(T) + pos0).astype(F32)[:, None] * inv[None, :]
    cos, sin = jnp.cos(ang), jnp.sin(ang)
    return jnp.concatenate([cos, cos], axis=-1), jnp.concatenate([-sin, sin], axis=-1)


def _hgrn_intra(units, mask_ref, L, fillers=()):
    sub = min(L, HGRN_SUB)
    n_sub = L // sub
    n_lev = mask_ref.shape[0] - 1
    top = len(units[0][3]) - n_lev
    assert n_sub in (1, 2) and top == n_sub - 1
    pieces = [(u, slice(i * sub, (i + 1) * sub)) for u in range(len(units)) for i in range(n_sub)]
    atts = [mask_ref[0] * _dot_nt(units[u][0][r], units[u][1][r]) for u, r in pieces]
    fillers = list(fillers)
    every = max(1, n_lev // (len(fillers) + 1))
    for lev in range(n_lev):
        for p, (u, r) in enumerate(pieces):
            q, k, _, expo = units[u]
            sc = jnp.exp(expo[top + lev][r])
            atts[p] = atts[p] + mask_ref[lev + 1] * _dot_nt(q[r] * sc, k[r] * sc)
        if fillers and (lev + 1) % every == 0:
            fillers.pop(0)()
    while fillers:
        fillers.pop(0)()
    if n_sub == 1:
        return [_dot(att, units[u][2]) for att, (u, _) in zip(atts, pieces)]
    outs = []
    for u, (q, k, v, expo) in enumerate(units):
        cross = _dot_nt(q[sub:] * jnp.exp(expo[0][sub:]), k[:sub] * jnp.exp(expo[0][:sub]))
        outs.append(jnp.concatenate([_dot(atts[2 * u], v[:sub]),
                                     _dot(jnp.concatenate([cross, atts[2 * u + 1]], axis=1), v)], axis=0))
    return outs


def _mixer0_body(*refs, L, layer_slot, g_chunk, fused, n_side, side_every):
    it = iter(refs)
    take = lambda n: [next(it) for _ in range(n)]
    src = take(3 if fused else 1)
    (cos_ref, sin_ref, lbl_ref, ga_ref, gb_ref, sel_ref, mask_ref, dmat_ref, win_ref, wtail_ref,
     sa0_ref, sb0_ref) = take(12)
    side_in = take(n_side)
    mix_ref, sa_ref, sb_ref = take(3)
    side_out = take(n_side)
    sat_scr, sbt_scr = take(2)
    c = pl.program_id(1)
    last = pl.num_programs(1) - 1
    head_w = IN0_WIDTH // N_HEADS
    G = sa_ref.shape[0]

    if n_side:
        @pl.when((pl.program_id(0) * pl.num_programs(1) + c) % side_every == 0)
        def _():
            for i_ref, o_ref in zip(side_in, side_out):
                o_ref[...] = i_ref[...].astype(BF16)

    @pl.when(c == 0)
    def _():
        for gi in range(G):
            for hh in range(N_HEADS):
                sat_scr[gi * N_HEADS + hh] = sa0_ref[gi, hh].T
                sbt_scr[gi * N_HEADS + hh] = sb0_ref[gi, hh].T

    lbl = lbl_ref[...]
    e = jnp.exp(lbl - jnp.max(lbl, axis=0, keepdims=True))
    lb_all = jnp.sum(e[:layer_slot + 1], axis=0, keepdims=True) / jnp.sum(e, axis=0, keepdims=True)

    if fused:
        x_ref, g_ref, w_ref = src
        hb = _rms(x_ref[...], g_ref[...]).astype(BF16)
    cosf = cos_ref[...]
    sinf = sin_ref[...]

    def project(hh):
        hcols = slice(hh * head_w, (hh + 1) * head_w)
        return jnp.dot(hb, w_ref[:, hcols], preferred_element_type=F32) if fused else src[0][:, hcols]

    def run_phase(heads, z_of, fillers):
        hg_units, ret_units = [], []
        for hh in heads:
            hs = slice(hh * HEAD_DK, (hh + 1) * HEAD_DK)
            lb = lb_all[:, hs]
            for gi in range(G):
                rs = slice(gi * L, (gi + 1) * L)
                si = gi * N_HEADS + hh
                zh = z_of[hh][rs]
                part = lambda j, zh=zh: zh[:, j * HEAD_DK:(j + 1) * HEAD_DK]
                f = lb + (1.0 - lb) * _sigmoid(part(1))
                b, b_last, expo = _decay_exponents(sel_ref, jnp.log(f), L)
                hg_units.append((rs, hs, si, part(0), 1.0 - f, part(2), part(3), b, b_last, expo))
                rq = part(4)
                rk = part(5)
                rq = rq * cosf + pltpu.roll(rq, HEAD_DK // 2, 1) * sinf
                rk = (rk * cosf + pltpu.roll(rk, HEAD_DK // 2, 1) * sinf) * (HEAD_DK ** -0.5)
                ret_units.append((rs, hh, si, rq, rk, part(6), part(7)))

        states = [sat_scr[u[2]] for u in hg_units]
        inter = [_dot_nt(u[3] * jnp.exp(u[7]), st) for u, st in zip(hg_units, states)]
        intra = _hgrn_intra([(u[3], u[4], u[5], u[9]) for u in hg_units], mask_ref, L, fillers)
        for (rs, hs, si, q, k, v, og, b, b_last, _), st, o1, o2 in zip(hg_units, states, inter, intra):
            sat_scr[si] = st * jnp.exp(b_last) + _dot_tn(v, k * jnp.exp(b_last - b))
            o = _rms((o1 + o2) * _sigmoid(og), ga_ref[...])
            mix_ref[rs, hs] = o.astype(mix_ref.dtype)

        states = [sbt_scr[si] for _, _, si, _, _, _, _ in ret_units]
        scores = [_dot_nt(rq, rk) * dmat_ref[hh] for _, hh, _, rq, rk, _, _ in ret_units]
        inter = [_dot_nt(u[3], rt) for u, rt in zip(ret_units, states)]
        outs = [_dot(sc, u[5]) + win_ref[u[1]] * it_ for u, sc, it_ in zip(ret_units, scores, inter)]
        for (rs, hh, si, rq, rk, rv, rg), rt, ro in zip(ret_units, states, outs):
            sbt_scr[si] = g_chunk[hh] * rt + _dot_tn(rv, rk * wtail_ref[hh])
            ro = _rms(ro, gb_ref[...]) * _silu(rg)
            mix_ref[rs, KW + hh * HEAD_DK:KW + (hh + 1) * HEAD_DK] = ro.astype(mix_ref.dtype)

    if fused:
        per = PHASE_HEADS
        z_of = {hh: project(hh) for hh in range(per)}
        for h0 in range(0, N_HEADS, per):
            nxt = range(h0 + per, min(h0 + 2 * per, N_HEADS))
            run_phase(range(h0, h0 + per), z_of, [lambda hh=hh: z_of.__setitem__(hh, project(hh)) for hh in nxt])
    else:
        run_phase(range(N_HEADS), {hh: project(hh) for hh in range(N_HEADS)}, [])

    @pl.when(c == last)
    def _():
        for gi in range(G):
            for hh in range(N_HEADS):
                sa_ref[gi, hh] = sat_scr[gi * N_HEADS + hh].T
                sb_ref[gi, hh] = sbt_scr[gi * N_HEADS + hh].T


def _mixer0(src, n_streams, T, pos0, sa0, sb0, lb_logits, ga, gb, layer_slot, side_cast=()):
    L = min(CHUNK0, T)
    nc = T // L
    G = max(1, min(n_streams, STEP_ROWS0 // L)) if nc == 1 else 1
    fused = len(src) == 3
    sel = jnp.asarray(_hgrn_consts(min(SEL_BLOCK, L))[0], BF16)
    masks = jnp.asarray(_hgrn_consts(min(HGRN_SUB, L))[1])
    dmat, w_in, w_tail, g_chunk = _retention_consts(L)
    cosf, sinf = _rope_tables(pos0, T)
    full = lambda a: pl.BlockSpec(a.shape, lambda s, c: (0,) * a.ndim, pipeline_mode=pl.Buffered(1))
    rows = lambda a: pl.BlockSpec((G * L, a.shape[1]), lambda s, c: (s * nc + c, 0))
    state_spec = pl.BlockSpec((G, N_HEADS, HEAD_DK, HEAD_DK), lambda s, c: (s, 0, 0, 0))
    state_shape = jax.ShapeDtypeStruct((n_streams, N_HEADS, HEAD_DK, HEAD_DK), F32)
    ga2, gb2 = ga.reshape(1, HEAD_DK), gb.reshape(1, HEAD_DK)
    if fused:
        src = (src[0], src[1].reshape(1, -1), src[2])
        src_specs = [rows(src[0]), full(src[1]), full(src[2])]
    else:
        src_specs = [rows(src[0])]
    consts = (lb_logits, ga2, gb2, sel, masks, dmat, w_in, w_tail)
    side = [w.reshape(-1, w.shape[-1]) for w in side_cast]
    per_block = (n_streams // G) * nc // SIDE_BLOCKS if side else 1
    side_spec = lambda w: pl.BlockSpec((w.shape[0] // SIDE_BLOCKS, w.shape[1]),
                                       lambda s, c: ((s * nc + c) // per_block, 0))
    outs = pl.pallas_call(
        functools.partial(_mixer0_body, L=L, layer_slot=layer_slot, g_chunk=g_chunk, fused=fused,
                          n_side=len(side), side_every=per_block),
        grid=(n_streams // G, nc),
        in_specs=src_specs
                 + [pl.BlockSpec((L, HEAD_DK), lambda s, c: (c, 0)), pl.BlockSpec((L, HEAD_DK), lambda s, c: (c, 0))]
                 + [full(a) for a in consts] + [state_spec, state_spec] + [side_spec(w) for w in side],
        out_specs=[pl.BlockSpec((G * L, 2 * KW), lambda s, c: (s * nc + c, 0)), state_spec, state_spec]
                  + [side_spec(w) for w in side],
        out_shape=[jax.ShapeDtypeStruct((n_streams * T, 2 * KW), BF16), state_shape, state_shape]
                  + [jax.ShapeDtypeStruct(w.shape, BF16) for w in side],
        scratch_shapes=[pltpu.VMEM((G * N_HEADS, HEAD_DK, HEAD_DK), F32),
                        pltpu.VMEM((G * N_HEADS, HEAD_DK, HEAD_DK), F32)],
        compiler_params=pltpu.CompilerParams(dimension_semantics=("arbitrary", "arbitrary"),
                                             vmem_limit_bytes=VMEM_LIMIT_BYTES),
        name="mixer0_T%d" % T,
    )(*src, cosf, sinf, *consts, sa0, sb0, *side)
    return outs[0], outs[1], outs[2], [o.reshape(w.shape) for o, w in zip(outs[3:], side_cast)]


def _mixer1_body(*refs, L, fused, zero_init, single_chunk):
    it = iter(refs)
    take = lambda n: [next(it) for _ in range(n)]
    src = take(3 if fused else 1)
    (cw_ref, cb_ref, wq_ref, wk_ref, wv_ref, wgate_ref, bgate_ref, tri_ref, gn_ref, skip_ref) = take(10)
    init = None if zero_init else take(4)
    hg_ref, c_ref, n_ref, m_ref, conv_ref = take(5)
    xbuf = take(1)[0]
    c = pl.program_id(1)
    last = pl.num_programs(1) - 1
    n_taps = C_CONV - 1
    G = c_ref.shape[0]
    kv_state = zero_init

    @pl.when(c == 0)
    def _():
        if zero_init:
            c_ref[...] = jnp.zeros_like(c_ref)
            n_ref[...] = jnp.zeros_like(n_ref)
            m_ref[...] = jnp.zeros_like(m_ref)
            xbuf[:, 0:CONV_HDR, :] = jnp.zeros((G, CONV_HDR, C_INNER), F32)
        else:
            c0_ref, n0_ref, m0_ref, conv0_ref = init
            if not single_chunk:
                c_ref[...] = c0_ref[...]
                n_ref[...] = n0_ref[...]
                m_ref[...] = m0_ref[...]
            xbuf[:, CONV_HDR - n_taps:CONV_HDR, :] = conv0_ref[...]

    c_in, n_in, m_in = init[:3] if single_chunk and not zero_init else (c_ref, n_ref, m_ref)

    if fused:
        x_ref, g_ref, w_ref = src
        hb = _rms(x_ref[...], g_ref[...]).astype(BF16)
        xm = jnp.dot(hb, w_ref[:, :C_INNER], preferred_element_type=F32)
    else:
        xm = src[0][:, :C_INNER]

    xcs, tails = [], []
    for gi in range(G):
        xg = xm[gi * L:(gi + 1) * L]
        xbuf[gi, CONV_HDR:CONV_HDR + L, :] = xg
        acc = cb_ref[...] + xg * cw_ref[n_taps:n_taps + 1, :]
        for w in range(n_taps):
            acc = acc + xbuf[gi, CONV_HDR - n_taps + w:CONV_HDR - n_taps + w + L, :] * cw_ref[w:w + 1, :]
        tails.append(xbuf[gi, CONV_HDR + L - n_taps:CONV_HDR + L, :])
        xbuf[gi, CONV_HDR - n_taps:CONV_HDR, :] = tails[gi]
        xcs.append(acc)
    xc = _silu(jnp.concatenate(xcs, axis=0) if G > 1 else xcs[0])

    xcb = xc.astype(BF16)
    xmb = xm.astype(BF16)
    qs, ks, vs = [], [], []
    for g in range(C_INNER // BD_GROUP):
        gs = slice(g * BD_GROUP, (g + 1) * BD_GROUP)
        qs.append(jnp.dot(xcb[:, gs], wq_ref[g], preferred_element_type=F32))
        ks.append(jnp.dot(xcb[:, gs], wk_ref[g], preferred_element_type=F32))
        vs.append(jnp.dot(xmb[:, gs], wv_ref[g], preferred_element_type=F32))
    q = jnp.concatenate(qs, axis=1).astype(BF16)
    k = (jnp.concatenate(ks, axis=1) * (C_DH ** -0.5)).astype(BF16)
    v = jnp.concatenate(vs, axis=1).astype(BF16)

    gates = (_dot(q, wgate_ref[0:C_INNER, :]) + _dot(k, wgate_ref[C_INNER:2 * C_INNER, :])
             + _dot(v, wgate_ref[2 * C_INNER:3 * C_INNER, :]) + bgate_ref[...])
    bcum = _exact_dot(tri_ref[...], _log_sigmoid(gates))
    lane = lax.broadcasted_iota(jnp.int32, gates.shape, 1)
    rows = jnp.where(lane < N_HEADS, gates, bcum).T
    ti = lax.broadcasted_iota(jnp.int32, (L, L), 0)
    si = lax.broadcasted_iota(jnp.int32, (L, L), 1)
    causal = si <= ti

    units = []
    for gi, hh in [(gi, hh) for gi in range(G) for hh in range(N_HEADS)]:
        hs = slice(hh * C_DH, (hh + 1) * C_DH)
        rs = slice(gi * L, (gi + 1) * L)
        b_col = bcum[rs, N_HEADS + hh:N_HEADS + hh + 1]
        i_col = gates[rs, hh:hh + 1]
        b_row = rows[N_HEADS + hh:N_HEADS + hh + 1, rs]
        i_row = rows[hh:hh + 1, rs]
        m_prev = m_in[gi, :, hh:hh + 1]
        lw = jnp.where(causal, b_col - b_row + i_row, -jnp.inf)
        lp = b_col + m_prev
        mj = jnp.maximum(lp, jnp.max(lw, axis=-1, keepdims=True))
        m_new = mj[L - 1:L, :]
        b_last = b_col[L - 1:L, :]
        units.append(dict(
            gi=gi, hh=hh, hs=hs, rs=rs, mj=mj, m_new=m_new, wgt=jnp.exp(lw - mj), wp=jnp.exp(lp - mj),
            ws=jnp.exp(b_last - b_col + i_col - m_new),
            wpl=jnp.exp(b_last + m_prev - m_new),
            nv=n_in[gi, hh:hh + 1, :],
            q=q[rs, hs], k=k[rs, hs], v=v[rs, hs]))

    def run_phase(units, fillers):
        fillers = list(fillers)
        fill = lambda: fillers.pop(0)() if fillers else None
        for u in units:
            u['cm'] = c_in[u['gi'], u['hh']]
        scores = [_dot_nt(u['q'], u['k']) * u['wgt'] for u in units]
        fill()
        inter = [_dot(u['q'], u['cm']) if kv_state else _dot_nt(u['q'], u['cm']) for u in units]
        fill()
        cells = []
        for u, s, it_ in zip(units, scores, inter):
            num = _dot(s, u['v']) + u['wp'] * it_
            den = (jnp.sum(s, axis=-1, keepdims=True)
                   + u['wp'] * jnp.sum(u['q'] * u['nv'], axis=-1, keepdims=True))
            cells.append(num / jnp.maximum(jnp.abs(den), jnp.exp(-u['mj'])))
        while fillers:
            fill()
        for u in units:
            gi, hh, vw = u['gi'], u['hh'], u['v'] * u['ws']
            c_ref[gi, hh] = u['wpl'] * u['cm'] + (_dot_tn(u['k'], vw) if kv_state else _dot_tn(vw, u['k']))
            n_ref[gi, hh:hh + 1, :] = u['wpl'] * u['nv'] + jnp.sum(u['k'] * u['ws'], axis=0, keepdims=True)
            m_ref[gi, :, hh:hh + 1] = u['m_new']
        for u, hcell in zip(units, cells):
            hs, rs = u['hs'], u['rs']
            hc = hcell - jnp.mean(hcell, axis=-1, keepdims=True)
            hn = hc * lax.rsqrt(jnp.mean(hc * hc, axis=-1, keepdims=True) + EPS) * gn_ref[...]
            out = (hn + skip_ref[:, hs] * xc[rs, hs]) * _silu(out_gate(rs, hs))
            hg_ref[rs, hs] = out.astype(hg_ref.dtype)

    if fused:
        zg_parts = {}
        out_gate = lambda rs, hs: zg_parts[hs.start // C_DH][rs]
        slices = [lambda hh=hh: zg_parts.__setitem__(hh, jnp.dot(
            hb, w_ref[:, C_INNER + hh * C_DH:C_INNER + (hh + 1) * C_DH], preferred_element_type=F32))
            for hh in range(N_HEADS)]
        for p0 in range(0, len(units), PHASE_HEADS):
            run_phase(units[p0:p0 + PHASE_HEADS], slices if p0 == 0 else [])
    else:
        out_gate = lambda rs, hs: src[0][rs, C_INNER + hs.start:C_INNER + hs.stop]
        run_phase(units, [])

    @pl.when(c == last)
    def _():
        for gi in range(G):
            conv_ref[gi] = tails[gi]
            if kv_state:
                for hh in range(N_HEADS):
                    c_ref[gi, hh] = c_ref[gi, hh].T


def _mixer1(src, n_streams, T, states, cw, cb, wq, wk, wv, wgate, bgate, gn, skip):
    L = min(CHUNK1, T)
    nc = T // L
    G = max(1, min(n_streams, STEP_ROWS1 // L)) if nc == 1 else 1
    fused = len(src) == 3
    zero_init = states is None
    tri = jnp.asarray(np.kron(np.eye(G, dtype=np.float32), np.tril(np.ones((L, L), np.float32))), BF16)
    full = lambda a: pl.BlockSpec(a.shape, lambda s, c: (0,) * a.ndim, pipeline_mode=pl.Buffered(1))
    rows = lambda a: pl.BlockSpec((G * L, a.shape[1]), lambda s, c: (s * nc + c, 0))
    per_stream = lambda shape: pl.BlockSpec((G,) + shape[1:], lambda s, c: (s,) + (0,) * (len(shape) - 1))
    gn2, skip2, cb2 = gn.reshape(1, C_DH), skip.reshape(1, C_INNER), cb.reshape(1, C_INNER)
    state_shapes = ((n_streams, N_HEADS, C_DH, C_DH), (n_streams, N_HEADS, C_DH), (n_streams, 1, N_HEADS),
                    (n_streams, C_CONV - 1, C_INNER))
    scratch = [pltpu.VMEM((G, CONV_HDR + L, C_INNER), F32)]
    if fused:
        src = (src[0], src[1].reshape(1, -1), src[2])
        src_specs = [rows(src[0]), full(src[1]), full(src[2])]
    else:
        src_specs = [rows(src[0])]
    consts = (cw, cb2, wq, wk, wv, wgate, bgate, tri, gn2, skip2)
    if not zero_init:
        states = tuple(a.reshape(shape) for a, shape in zip(states, state_shapes))
    hg, c1, n1, m1, conv1 = pl.pallas_call(
        functools.partial(_mixer1_body, L=L, fused=fused, zero_init=zero_init, single_chunk=nc == 1),
        grid=(n_streams // G, nc),
        in_specs=src_specs + [full(a) for a in consts]
                 + ([] if zero_init else [per_stream(shape) for shape in state_shapes]),
        out_specs=[pl.BlockSpec((G * L, C_INNER), lambda s, c: (s * nc + c, 0))]
                  + [per_stream(shape) for shape in state_shapes],
        out_shape=[jax.ShapeDtypeStruct((n_streams * T, C_INNER), BF16)]
                  + [jax.ShapeDtypeStruct(shape, F32) for shape in state_shapes],
        scratch_shapes=scratch,
        compiler_params=pltpu.CompilerParams(dimension_semantics=("arbitrary", "arbitrary"),
                                             vmem_limit_bytes=VMEM_LIMIT_BYTES),
        name="mixer1_T%d" % T,
    )(*src, *consts, *(() if zero_init else states))
    return hg, c1, n1, m1.reshape(n_streams, N_HEADS), conv1


def _dense_blockdiag(w):
    rows = w.reshape(-1, BD_GROUP, C_BLOCK)
    tiled = jnp.tile(rows, (1, 1, BD_GROUP // C_BLOCK))
    idx = np.arange(BD_GROUP) // C_BLOCK
    same_block = jnp.asarray(idx[:, None] == idx[None, :])
    return jnp.where(same_block, tiled, 0.0).astype(BF16)


def kernel(x_prompt, x_sample, state_hgrn, state_ret, state_mlstm_c, state_mlstm_n, state_mlstm_m, state_conv,
           norm_mix, norm_ffn, norm_final, w_in0, lb_logits, hgrn_norm, ret_norm, w_out0,
           w_up1, conv_w, conv_b, w_q1, w_k1, w_v1, w_ig, b_ig, w_fg, b_fg, mlstm_norm, mlstm_skip, w_down1,
           w_ffn_gate, w_ffn_up, w_ffn_down):
    bp, tp, d = x_prompt.shape
    bs, ts, _ = x_sample.shape
    mp, ms = bp * tp, bs * ts
    zeros = lambda *shape: jnp.zeros(shape, F32)
    w_in, = _cast_weights([w_in0[0]])
    later_weights = [w_out0[0], w_up1[0], w_down1[0], w_ffn_gate, w_ffn_up, w_ffn_down]

    xp = x_prompt.reshape(mp, d)
    xs = x_sample.reshape(ms, d)

    mix_p, hg_p, rt_p, (w_out, w_up, w_down, wf_gate, wf_up, wf_down) = _mixer0(
        (xp, norm_mix[0], w_in), bp, tp, 0, zeros(bp, N_HEADS, HEAD_DK, HEAD_DK),
        zeros(bp, N_HEADS, HEAD_DK, HEAD_DK), lb_logits, hgrn_norm[0], ret_norm[0], 0, side_cast=later_weights)
    ffn0 = (w_out, norm_ffn[0], wf_gate, wf_up, wf_down, 0, norm_final, False)
    mix_s, hg_s, rt_s, _ = _mixer0((_norm_matmul(xs, norm_mix[0], w_in),), bs, ts, PAST_LEN, state_hgrn[0],
                                   state_ret[0], lb_logits, hgrn_norm[0], ret_norm[0], 0)
    xp, xs = _proj_ffn(xp, mix_p, xs, mix_s, *ffn0)

    ffn1 = (w_down, norm_ffn[1], wf_gate, wf_up, wf_down, 1, norm_final, True)
    wgate = jnp.pad(jnp.concatenate([w_ig[0], w_fg[0]], axis=1), ((0, 0), (0, GATE_LANES - 2 * N_HEADS))).astype(BF16)
    bgate = jnp.pad(jnp.concatenate([b_ig[0], b_fg[0]]), (0, GATE_LANES - 2 * N_HEADS)).reshape(1, GATE_LANES)
    m1_consts = (conv_w[0], conv_b[0], _dense_blockdiag(w_q1[0]), _dense_blockdiag(w_k1[0]),
                 _dense_blockdiag(w_v1[0]), wgate, bgate, mlstm_norm[0], mlstm_skip[0])
    hg1_p, mc_p, mn_p, mm_p, cv_p = _mixer1((xp, norm_mix[1], w_up), bp, tp, None, *m1_consts)
    hg1_s, mc_s, mn_s, mm_s, cv_s = _mixer1((_norm_matmul(xs, norm_mix[1], w_up),), bs, ts,
                                            (state_mlstm_c[0], state_mlstm_n[0], state_mlstm_m[0], state_conv[0]),
                                            *m1_consts)
    yp, ys = _proj_ffn(xp, hg1_p, xs, hg1_s, *ffn1)

    lead = lambda a: a[None]
    return (yp.reshape(bp, tp, d), ys.reshape(bs, ts, d),
            lead(hg_p), lead(hg_s), lead(rt_p), lead(rt_s),
            lead(mc_p), lead(mc_s), lead(mn_p), lead(mn_s), lead(mm_p), lead(mm_s), lead(cv_p), lead(cv_s))
```

```python
import functools

import numpy as np
import jax
import jax.numpy as jnp
from jax import lax
from jax.experimental import pallas as pl
from jax.experimental.pallas import tpu as pltpu

F32 = jnp.float32
BF16 = jnp.bfloat16

D_MODEL = 1024
CHUNK0 = 256
CHUNK1 = 256
STEP_ROWS0 = 128
STEP_ROWS1 = 32
SEL_BLOCK = 64
HGRN_SUB = 128
PHASE_HEADS = 2
EPS = 1e-6
SUBLANES = 8
N_HEADS = 4
HEAD_DK = 128
KW = N_HEADS * HEAD_DK
IN0_WIDTH = 8 * KW
ROPE_BASE = 10000.0
C_INNER = 2 * D_MODEL
C_DH = C_INNER // N_HEADS
C_CONV = 4
C_BLOCK = 4
BD_GROUP = 256
PAST_LEN = 2048
GATE_LANES = 128
CONV_HDR = 8
VMEM_LIMIT_BYTES = 56 * 1024 * 1024
ROW_TILE = 512
CAST_STEPS = 4
SIDE_BLOCKS = 32


def _dot(a, b):
    return jnp.dot(a.astype(BF16), b.astype(BF16), preferred_element_type=F32)


def _dot_nt(a, b):
    return lax.dot_general(a.astype(BF16), b.astype(BF16), (((1,), (1,)), ((), ())),
                           preferred_element_type=F32)


def _dot_tn(a, b):
    return lax.dot_general(a.astype(BF16), b.astype(BF16), (((0,), (0,)), ((), ())),
                           preferred_element_type=F32)


def _exact_dot(sel, x):
    hi = x.astype(BF16)
    lo = (x - hi.astype(F32)).astype(BF16)
    n = x.shape[1]
    both = jnp.dot(sel, jnp.concatenate([hi, lo], axis=1), preferred_element_type=F32)
    return both[:, :n] + both[:, n:]


def _rms(x, g):
    return x * lax.rsqrt(jnp.mean(x * x, axis=-1, keepdims=True) + EPS) * g


def _sigmoid(x):
    return 1.0 / (1.0 + jnp.exp(-x))


def _silu(x):
    return x * _sigmoid(x)


def _log_sigmoid(x):
    return jnp.minimum(x, 0.0) - jnp.log(1.0 + jnp.exp(-jnp.abs(x)))


def _norm_matmul_body(x_ref, g_ref, w_ref, o_ref):
    h = _rms(x_ref[...], g_ref[...])
    o_ref[...] = jnp.dot(h.astype(BF16), w_ref[...], preferred_element_type=F32)


def _norm_matmul(x, g, w):
    m, d = x.shape
    n = w.shape[1]
    return pl.pallas_call(
        _norm_matmul_body,
        grid=(m // ROW_TILE,),
        in_specs=[pl.BlockSpec((ROW_TILE, d), lambda i: (i, 0)),
                  pl.BlockSpec((1, d), lambda i: (0, 0)),
                  pl.BlockSpec((d, n), lambda i: (0, 0), pipeline_mode=pl.Buffered(1))],
        out_specs=pl.BlockSpec((ROW_TILE, n), lambda i: (i, 0)),
        out_shape=jax.ShapeDtypeStruct((m, n), F32),
        compiler_params=pltpu.CompilerParams(dimension_semantics=("arbitrary",),
                                             vmem_limit_bytes=VMEM_LIMIT_BYTES),
        name="norm_matmul",
    )(x, g.reshape(1, d), w)


def _cast_weights_body(*refs, n):
    ins, outs = refs[:n], refs[n:]
    n_slot = IN0_WIDTH // KW
    for hh in range(N_HEADS):
        for j in range(n_slot):
            dst = slice((hh * n_slot + j) * HEAD_DK, (hh * n_slot + j + 1) * HEAD_DK)
            src = slice((j * N_HEADS + hh) * HEAD_DK, (j * N_HEADS + hh + 1) * HEAD_DK)
            outs[0][:, dst] = ins[0][:, src].astype(BF16)
    for i_ref, o_ref in zip(ins[1:], outs[1:]):
        o_ref[...] = i_ref[...].astype(BF16)


def _cast_weights(ws):
    flat = [w.reshape(-1, w.shape[-1]) for w in ws]
    spec = lambda w: pl.BlockSpec((w.shape[0] // CAST_STEPS, w.shape[1]), lambda i: (i, 0))
    out = pl.pallas_call(
        functools.partial(_cast_weights_body, n=len(flat)),
        grid=(CAST_STEPS,),
        in_specs=[spec(w) for w in flat],
        out_specs=[spec(w) for w in flat],
        out_shape=[jax.ShapeDtypeStruct(w.shape, BF16) for w in flat],
        compiler_params=pltpu.CompilerParams(dimension_semantics=("arbitrary",),
                                             vmem_limit_bytes=VMEM_LIMIT_BYTES),
        name="cast_weights",
    )(*flat)
    return [o.reshape(w.shape) for o, w in zip(out, ws)]


def _proj_ffn_body(x_ref, a_ref, wo_ref, g_ref, wg_ref, wu_ref, wd_ref, gf_ref, o_ref, *, final_norm):
    x1 = x_ref[...] + jnp.dot(a_ref[...], wo_ref[...], preferred_element_type=F32)
    h = _rms(x1, g_ref[...]).astype(BF16)
    gate = jnp.dot(h, wg_ref[...], preferred_element_type=F32)
    up = jnp.dot(h, wu_ref[...], preferred_element_type=F32)
    t = (_silu(gate) * up).astype(BF16)
    x2 = x1 + jnp.dot(t, wd_ref[...], preferred_element_type=F32)
    if final_norm:
        x2 = _rms(x2, gf_ref[...])
    o_ref[...] = x2


def _proj_ffn_pair_body(xa_ref, aa_ref, xb_ref, ab_ref, wo_ref, g_ref, wg_ref, wu_ref, wd_ref, gf_ref,
                        oa_ref, ob_ref, *, final_norm, n_a):
    i = pl.program_id(0)
    shared = (wo_ref, g_ref, wg_ref, wu_ref, wd_ref, gf_ref)

    @pl.when(i < n_a)
    def _():
        _proj_ffn_body(xa_ref, aa_ref, *shared, oa_ref, final_norm=final_norm)

    @pl.when(i >= n_a)
    def _():
        _proj_ffn_body(xb_ref, ab_ref, *shared, ob_ref, final_norm=final_norm)


def _proj_ffn(xa, aa, xb, ab, wo, g, wg, wu, wd, layer, gf, final_norm):
    d = xa.shape[1]
    ka = aa.shape[1]
    ff = wg.shape[2]
    n_a, n_b = xa.shape[0] // ROW_TILE, xb.shape[0] // ROW_TILE
    const = lambda shape: pl.BlockSpec(shape, lambda i: (0, 0), pipeline_mode=pl.Buffered(1))
    of_layer = lambda shape: pl.BlockSpec((None,) + shape, lambda i: (layer, 0, 0), pipeline_mode=pl.Buffered(1))
    rows_a = lambda w: pl.BlockSpec((ROW_TILE, w), lambda i: (jnp.minimum(i, n_a - 1), 0))
    rows_b = lambda w: pl.BlockSpec((ROW_TILE, w), lambda i: (jnp.maximum(i - n_a, 0), 0))
    return pl.pallas_call(
        functools.partial(_proj_ffn_pair_body, final_norm=final_norm, n_a=n_a),
        grid=(n_a + n_b,),
        in_specs=[rows_a(d), rows_a(ka), rows_b(d), rows_b(ka),
                  const((ka, d)), const((1, d)), of_layer((d, ff)), of_layer((d, ff)), of_layer((ff, d)),
                  const((1, d))],
        out_specs=[rows_a(d), rows_b(d)],
        out_shape=[jax.ShapeDtypeStruct(xa.shape, F32), jax.ShapeDtypeStruct(xb.shape, F32)],
        compiler_params=pltpu.CompilerParams(dimension_semantics=("arbitrary",),
                                             vmem_limit_bytes=VMEM_LIMIT_BYTES),
        name="proj_ffn",
    )(xa, aa, xb, ab, wo, g.reshape(1, d), wg, wu, wd, gf.reshape(1, d))


def _hgrn_consts(L):
    r = np.arange(L)[:, None]
    t = np.arange(L)[None, :]
    sels = [t <= r]
    masks = [r == t]
    h = L // 2
    while h >= 1:
        base = (r // (2 * h)) * (2 * h)
        upper = (r % (2 * h)) >= h
        if h < SUBLANES:
            sels.append(np.where(upper, (t >= base + h) & (t <= r), (t > r) & (t <= base + h - 1)))
        masks.append((r // (2 * h) == t // (2 * h)) & upper & ((t % (2 * h)) < h))
        h //= 2
    return np.concatenate(sels, axis=0).astype(np.float32), np.stack(masks).astype(np.float32)


def _decay_exponents(sel_ref, logf, L):
    blk = min(SEL_BLOCK, L)
    nb = L // blk
    res = [_exact_dot(sel_ref[...], logf[i * blk:(i + 1) * blk]) for i in range(nb)]
    cum = [r[0:blk] for r in res]
    pre = [jnp.zeros_like(cum[0][0:1])]
    for i in range(nb):
        pre.append(pre[i] + cum[i][blk - 1:blk])
    cat = lambda parts: jnp.concatenate(parts, axis=0) if len(parts) > 1 else parts[0]
    b = cat([cum[i] + pre[i] for i in range(nb)])
    levels = []
    nbh = nb // 2
    while nbh >= 1:
        parts = []
        for i in range(nb):
            ref = pre[(i // (2 * nbh)) * 2 * nbh + nbh]
            if (i % (2 * nbh)) >= nbh:
                parts.append(cum[i] + (pre[i] - ref))
            else:
                parts.append((ref - pre[i]) - cum[i])
        levels.append(cat(parts))
        nbh //= 2
    h = blk // 2
    while h >= SUBLANES:
        parts = []
        for i in range(nb):
            for base in range(0, blk, 2 * h):
                ref = cum[i][base + h - 1:base + h]
                parts += [ref - cum[i][base:base + h], cum[i][base + h:base + 2 * h] - ref]
        levels.append(cat(parts))
        h //= 2
    for lev in range(sel_ref.shape[0] // blk - 1):
        levels.append(cat([r[(lev + 1) * blk:(lev + 2) * blk] for r in res]))
    return b, pre[nb], levels


def _retention_consts(L):
    lg = np.log1p(-np.exp2(-5.0 - np.arange(N_HEADS, dtype=np.float32))).astype(np.float32)
    idx = np.arange(L, dtype=np.float32)
    rel = idx[:, None] - idx[None, :]
    dmat = np.where(rel >= 0, np.exp(lg[:, None, None] * np.maximum(rel, 0.0)), 0.0)
    w_in = np.exp(lg[:, None] * (idx + 1.0))
    w_tail = np.exp(lg[:, None] * (L - 1.0 - idx))
    bcast = lambda v: jnp.asarray(np.broadcast_to(v[:, :, None], (N_HEADS, L, HEAD_DK)).astype(np.float32))
    g_chunk = [float(v) for v in np.exp(lg * L).astype(np.float32)]
    return jnp.asarray(dmat.astype(np.float32)), bcast(w_in), bcast(w_tail), g_chunk


def _rope_tables(pos0, T):
    half = HEAD_DK // 2
    inv = ROPE_BASE ** (-jnp.arange(half, dtype=F32) / half)
    ang = (jnp.arange(T) + pos0).astype(F32)[:, None] * inv[None, :]
    cos, sin = jnp.cos(ang), jnp.sin(ang)
    return jnp.concatenate([cos, cos], axis=-1), jnp.concatenate([-sin, sin], axis=-1)


def _hgrn_intra(units, mask_ref, L, fillers=()):
    sub = min(L, HGRN_SUB)
    n_sub = L // sub
    n_lev = mask_ref.shape[0] - 1
    top = len(units[0][3]) - n_lev
    assert n_sub in (1, 2) and top == n_sub - 1
    pieces = [(u, slice(i * sub, (i + 1) * sub)) for u in range(len(units)) for i in range(n_sub)]
    atts = [mask_ref[0] * _dot_nt(units[u][0][r], units[u][1][r]) for u, r in pieces]
    fillers = list(fillers)
    every = max(1, n_lev // (len(fillers) + 1))
    for lev in range(n_lev):
        for p, (u, r) in enumerate(pieces):
            q, k, _, expo = units[u]
            sc = jnp.exp(expo[top + lev][r])
            atts[p] = atts[p] + mask_ref[lev + 1] * _dot_nt(q[r] * sc, k[r] * sc)
        if fillers and (lev + 1) % every == 0:
            fillers.pop(0)()
    while fillers:
        fillers.pop(0)()
    if n_sub == 1:
        return [_dot(att, units[u][2]) for att, (u, _) in zip(atts, pieces)]
    outs = []
    for u, (q, k, v, expo) in enumerate(units):
        cross = _dot_nt(q[sub:] * jnp.exp(expo[0][sub:]), k[:sub] * jnp.exp(expo[0][:sub]))
        outs.append(jnp.concatenate([_dot(atts[2 * u], v[:sub]),
                                     _dot(jnp.concatenate([cross, atts[2 * u + 1]], axis=1), v)], axis=0))
    return outs


def _mixer0_body(*refs, L, layer_slot, g_chunk, fused, n_side, side_every):
    it = iter(refs)
    take = lambda n: [next(it) for _ in range(n)]
    src = take(3 if fused else 1)
    (cos_ref, sin_ref, lbl_ref, ga_ref, gb_ref, sel_ref, mask_ref, dmat_ref, win_ref, wtail_ref,
     sa0_ref, sb0_ref) = take(12)
    side_in = take(n_side)
    mix_ref, sa_ref, sb_ref = take(3)
    side_out = take(n_side)
    sat_scr, sbt_scr = take(2)
    c = pl.program_id(1)
    last = pl.num_programs(1) - 1
    head_w = IN0_WIDTH // N_HEADS
    G = sa_ref.shape[0]

    if n_side:
        @pl.when((pl.program_id(0) * pl.num_programs(1) + c) % side_every == 0)
        def _():
            for i_ref, o_ref in zip(side_in, side_out):
                o_ref[...] = i_ref[...].astype(BF16)

    @pl.when(c == 0)
    def _():
        for gi in range(G):
            for hh in range(N_HEADS):
                sat_scr[gi * N_HEADS + hh] = sa0_ref[gi, hh].T
                sbt_scr[gi * N_HEADS + hh] = sb0_ref[gi, hh].T

    lbl = lbl_ref[...]
    e = jnp.exp(lbl - jnp.max(lbl, axis=0, keepdims=True))
    lb_all = jnp.sum(e[:layer_slot + 1], axis=0, keepdims=True) / jnp.sum(e, axis=0, keepdims=True)

    if fused:
        x_ref, g_ref, w_ref = src
        hb = _rms(x_ref[...], g_ref[...]).astype(BF16)
    cosf = cos_ref[...]
    sinf = sin_ref[...]

    def project(hh, half):
        lo = hh * head_w + half * (head_w // 2)
        cols = slice(lo, lo + head_w // 2)
        return jnp.dot(hb, w_ref[:, cols], preferred_element_type=F32) if fused else src[0][:, cols]

    def run_phase(heads, z_of, fillers):
        hg_units = []
        for hh in heads:
            hs = slice(hh * HEAD_DK, (hh + 1) * HEAD_DK)
            lb = lb_all[:, hs]
            for gi in range(G):
                rs = slice(gi * L, (gi + 1) * L)
                zh = z_of[hh, 0][rs]
                part = lambda j, zh=zh: zh[:, j * HEAD_DK:(j + 1) * HEAD_DK]
                f = lb + (1.0 - lb) * _sigmoid(part(1))
                b, b_last, expo = _decay_exponents(sel_ref, jnp.log(f), L)
                hg_units.append((rs, hs, gi * N_HEADS + hh, part(0), 1.0 - f, part(2), part(3), b, b_last, expo))

        states = [sat_scr[u[2]] for u in hg_units]
        inter = [_dot_nt(u[3] * jnp.exp(u[7]), st) for u, st in zip(hg_units, states)]
        intra = _hgrn_intra([(u[3], u[4], u[5], u[9]) for u in hg_units], mask_ref, L, fillers)
        for (rs, hs, si, q, k, v, og, b, b_last, _), st, o1, o2 in zip(hg_units, states, inter, intra):
            sat_scr[si] = st * jnp.exp(b_last) + _dot_tn(v, k * jnp.exp(b_last - b))
            o = _rms((o1 + o2) * _sigmoid(og), ga_ref[...])
            mix_ref[rs, hs] = o.astype(mix_ref.dtype)

        ret_units = []
        for hh in heads:
            for gi in range(G):
                rs = slice(gi * L, (gi + 1) * L)
                zh = z_of[hh, 1][rs]
                part = lambda j, zh=zh: zh[:, j * HEAD_DK:(j + 1) * HEAD_DK]
                rq = part(0)
                rk = part(1)
                rq = rq * cosf + pltpu.roll(rq, HEAD_DK // 2, 1) * sinf
                rk = (rk * cosf + pltpu.roll(rk, HEAD_DK // 2, 1) * sinf) * (HEAD_DK ** -0.5)
                ret_units.append((rs, hh, gi * N_HEADS + hh, rq, rk, part(2), part(3)))
        states = [sbt_scr[si] for _, _, si, _, _, _, _ in ret_units]
        scores = [_dot_nt(rq, rk) * dmat_ref[hh] for _, hh, _, rq, rk, _, _ in ret_units]
        inter = [_dot_nt(u[3], rt) for u, rt in zip(ret_units, states)]
        outs = [_dot(sc, u[5]) + win_ref[u[1]] * it_ for u, sc, it_ in zip(ret_units, scores, inter)]
        for (rs, hh, si, rq, rk, rv, rg), rt, ro in zip(ret_units, states, outs):
            sbt_scr[si] = g_chunk[hh] * rt + _dot_tn(rv, rk * wtail_ref[hh])
            ro = _rms(ro, gb_ref[...]) * _silu(rg)
            mix_ref[rs, KW + hh * HEAD_DK:KW + (hh + 1) * HEAD_DK] = ro.astype(mix_ref.dtype)

    if fused:
        per = PHASE_HEADS
        z_of = {(hh, 0): project(hh, 0) for hh in range(per)}
        put = lambda hh, half: (lambda: z_of.__setitem__((hh, half), project(hh, half)))
        for h0 in range(0, N_HEADS, per):
            nxt = range(h0 + per, min(h0 + 2 * per, N_HEADS))
            run_phase(range(h0, h0 + per), z_of,
                      [put(hh, 1) for hh in range(h0, h0 + per)] + [put(hh, 0) for hh in nxt])
    else:
        run_phase(range(N_HEADS), {(hh, half): project(hh, half) for hh in range(N_HEADS) for half in (0, 1)}, [])

    @pl.when(c == last)
    def _():
        for gi in range(G):
            for hh in range(N_HEADS):
                sa_ref[gi, hh] = sat_scr[gi * N_HEADS + hh].T
                sb_ref[gi, hh] = sbt_scr[gi * N_HEADS + hh].T


def _mixer0(src, n_streams, T, pos0, sa0, sb0, lb_logits, ga, gb, layer_slot, side_cast=()):
    L = min(CHUNK0, T)
    nc = T // L
    G = max(1, min(n_streams, STEP_ROWS0 // L)) if nc == 1 else 1
    fused = len(src) == 3
    sel = jnp.asarray(_hgrn_consts(min(SEL_BLOCK, L))[0], BF16)
    masks = jnp.asarray(_hgrn_consts(min(HGRN_SUB, L))[1])
    dmat, w_in, w_tail, g_chunk = _retention_consts(L)
    cosf, sinf = _rope_tables(pos0, T)
    full = lambda a: pl.BlockSpec(a.shape, lambda s, c: (0,) * a.ndim, pipeline_mode=pl.Buffered(1))
    rows = lambda a: pl.BlockSpec((G * L, a.shape[1]), lambda s, c: (s * nc + c, 0))
    state_spec = pl.BlockSpec((G, N_HEADS, HEAD_DK, HEAD_DK), lambda s, c: (s, 0, 0, 0))
    state_shape = jax.ShapeDtypeStruct((n_streams, N_HEADS, HEAD_DK, HEAD_DK), F32)
    ga2, gb2 = ga.reshape(1, HEAD_DK), gb.reshape(1, HEAD_DK)
    if fused:
        src = (src[0], src[1].reshape(1, -1), src[2])
        src_specs = [rows(src[0]), full(src[1]), full(src[2])]
    else:
        src_specs = [rows(src[0])]
    consts = (lb_logits, ga2, gb2, sel, masks, dmat, w_in, w_tail)
    side = [w.reshape(-1, w.shape[-1]) for w in side_cast]
    per_block = (n_streams // G) * nc // SIDE_BLOCKS if side else 1
    side_spec = lambda w: pl.BlockSpec((w.shape[0] // SIDE_BLOCKS, w.shape[1]),
                                       lambda s, c: ((s * nc + c) // per_block, 0))
    outs = pl.pallas_call(
        functools.partial(_mixer0_body, L=L, layer_slot=layer_slot, g_chunk=g_chunk, fused=fused,
                          n_side=len(side), side_every=per_block),
        grid=(n_streams // G, nc),
        in_specs=src_specs
                 + [pl.BlockSpec((L, HEAD_DK), lambda s, c: (c, 0)), pl.BlockSpec((L, HEAD_DK), lambda s, c: (c, 0))]
                 + [full(a) for a in consts] + [state_spec, state_spec] + [side_spec(w) for w in side],
        out_specs=[pl.BlockSpec((G * L, 2 * KW), lambda s, c: (s * nc + c, 0)), state_spec, state_spec]
                  + [side_spec(w) for w in side],
        out_shape=[jax.ShapeDtypeStruct((n_streams * T, 2 * KW), BF16), state_shape, state_shape]
                  + [jax.ShapeDtypeStruct(w.shape, BF16) for w in side],
        scratch_shapes=[pltpu.VMEM((G * N_HEADS, HEAD_DK, HEAD_DK), F32),
                        pltpu.VMEM((G * N_HEADS, HEAD_DK, HEAD_DK), F32)],
        compiler_params=pltpu.CompilerParams(dimension_semantics=("arbitrary", "arbitrary"),
                                             vmem_limit_bytes=VMEM_LIMIT_BYTES),
        name="mixer0_T%d" % T,
    )(*src, cosf, sinf, *consts, sa0, sb0, *side)
    return outs[0], outs[1], outs[2], [o.reshape(w.shape) for o, w in zip(outs[3:], side_cast)]


def _mixer1_body(*refs, L, fused, zero_init, single_chunk):
    it = iter(refs)
    take = lambda n: [next(it) for _ in range(n)]
    src = take(3 if fused else 1)
    (cw_ref, cb_ref, wq_ref, wk_ref, wv_ref, wgate_ref, bgate_ref, tri_ref, gn_ref, skip_ref) = take(10)
    init = None if zero_init else take(4)
    hg_ref, c_ref, n_ref, m_ref, conv_ref = take(5)
    xbuf = take(1)[0]
    c = pl.program_id(1)
    last = pl.num_programs(1) - 1
    n_taps = C_CONV - 1
    G = c_ref.shape[0]
    kv_state = zero_init

    @pl.when(c == 0)
    def _():
        if zero_init:
            c_ref[...] = jnp.zeros_like(c_ref)
            n_ref[...] = jnp.zeros_like(n_ref)
            m_ref[...] = jnp.zeros_like(m_ref)
            xbuf[:, 0:CONV_HDR, :] = jnp.zeros((G, CONV_HDR, C_INNER), F32)
        else:
            c0_ref, n0_ref, m0_ref, conv0_ref = init
            if not single_chunk:
                c_ref[...] = c0_ref[...]
                n_ref[...] = n0_ref[...]
                m_ref[...] = m0_ref[...]
            xbuf[:, CONV_HDR - n_taps:CONV_HDR, :] = conv0_ref[...]

    c_in, n_in, m_in = init[:3] if single_chunk and not zero_init else (c_ref, n_ref, m_ref)

    if fused:
        x_ref, g_ref, w_ref = src
        hb = _rms(x_ref[...], g_ref[...]).astype(BF16)
        xm = jnp.dot(hb, w_ref[:, :C_INNER], preferred_element_type=F32)
        zg = jnp.dot(hb, w_ref[:, C_INNER:], preferred_element_type=F32)
    else:
        xm = src[0][:, :C_INNER]
        zg = src[0][:, C_INNER:]

    xcs, tails = [], []
    for gi in range(G):
        xg = xm[gi * L:(gi + 1) * L]
        xbuf[gi, CONV_HDR:CONV_HDR + L, :] = xg
        acc = cb_ref[...] + xg * cw_ref[n_taps:n_taps + 1, :]
        for w in range(n_taps):
            acc = acc + xbuf[gi, CONV_HDR - n_taps + w:CONV_HDR - n_taps + w + L, :] * cw_ref[w:w + 1, :]
        tails.append(xbuf[gi, CONV_HDR + L - n_taps:CONV_HDR + L, :])
        xbuf[gi, CONV_HDR - n_taps:CONV_HDR, :] = tails[gi]
        xcs.append(acc)
    xc = _silu(jnp.concatenate(xcs, axis=0) if G > 1 else xcs[0])

    xcb = xc.astype(BF16)
    xmb = xm.astype(BF16)
    qs, ks, vs = [], [], []
    for g in range(C_INNER // BD_GROUP):
        gs = slice(g * BD_GROUP, (g + 1) * BD_GROUP)
        qs.append(jnp.dot(xcb[:, gs], wq_ref[g], preferred_element_type=F32))
        ks.append(jnp.dot(xcb[:, gs], wk_ref[g], preferred_element_type=F32))
        vs.append(jnp.dot(xmb[:, gs], wv_ref[g], preferred_element_type=F32))
    q = jnp.concatenate(qs, axis=1).astype(BF16)
    k = (jnp.concatenate(ks, axis=1) * (C_DH ** -0.5)).astype(BF16)
    v = jnp.concatenate(vs, axis=1).astype(BF16)

    gates = (_dot(q, wgate_ref[0:C_INNER, :]) + _dot(k, wgate_ref[C_INNER:2 * C_INNER, :])
             + _dot(v, wgate_ref[2 * C_INNER:3 * C_INNER, :]) + bgate_ref[...])
    bcum = _exact_dot(tri_ref[...], _log_sigmoid(gates))
    lane = lax.broadcasted_iota(jnp.int32, gates.shape, 1)
    rows = jnp.where(lane < N_HEADS, gates, bcum).T
    ti = lax.broadcasted_iota(jnp.int32, (L, L), 0)
    si = lax.broadcasted_iota(jnp.int32, (L, L), 1)
    causal = si <= ti

    units = []
    for gi, hh in [(gi, hh) for gi in range(G) for hh in range(N_HEADS)]:
        hs = slice(hh * C_DH, (hh + 1) * C_DH)
        rs = slice(gi * L, (gi + 1) * L)
        b_col = bcum[rs, N_HEADS + hh:N_HEADS + hh + 1]
        i_col = gates[rs, hh:hh + 1]
        b_row = rows[N_HEADS + hh:N_HEADS + hh + 1, rs]
        i_row = rows[hh:hh + 1, rs]
        m_prev = m_in[gi, :, hh:hh + 1]
        lw = jnp.where(causal, b_col - b_row + i_row, -jnp.inf)
        lp = b_col + m_prev
        mj = jnp.maximum(lp, jnp.max(lw, axis=-1, keepdims=True))
        m_new = mj[L - 1:L, :]
        b_last = b_col[L - 1:L, :]
        units.append(dict(
            gi=gi, hh=hh, hs=hs, rs=rs, mj=mj, m_new=m_new, wgt=jnp.exp(lw - mj), wp=jnp.exp(lp - mj),
            ws=jnp.exp(b_last - b_col + i_col - m_new),
            wpl=jnp.exp(b_last + m_prev - m_new),
            cm=c_in[gi, hh],
            nv=n_in[gi, hh:hh + 1, :],
            q=q[rs, hs], k=k[rs, hs], v=v[rs, hs]))

    scores = [_dot_nt(u['q'], u['k']) * u['wgt'] for u in units]
    inter = [_dot(u['q'], u['cm']) if kv_state else _dot_nt(u['q'], u['cm']) for u in units]
    cells = []
    for u, s, it_ in zip(units, scores, inter):
        num = _dot(s, u['v']) + u['wp'] * it_
        den = jnp.sum(s, axis=-1, keepdims=True) + u['wp'] * jnp.sum(u['q'] * u['nv'], axis=-1, keepdims=True)
        cells.append(num / jnp.maximum(jnp.abs(den), jnp.exp(-u['mj'])))
    for u in units:
        gi, hh, vw = u['gi'], u['hh'], u['v'] * u['ws']
        c_ref[gi, hh] = u['wpl'] * u['cm'] + (_dot_tn(u['k'], vw) if kv_state else _dot_tn(vw, u['k']))
        n_ref[gi, hh:hh + 1, :] = u['wpl'] * u['nv'] + jnp.sum(u['k'] * u['ws'], axis=0, keepdims=True)
        m_ref[gi, :, hh:hh + 1] = u['m_new']
    for u, hcell in zip(units, cells):
        hs, rs = u['hs'], u['rs']
        hc = hcell - jnp.mean(hcell, axis=-1, keepdims=True)
        hn = hc * lax.rsqrt(jnp.mean(hc * hc, axis=-1, keepdims=True) + EPS) * gn_ref[...]
        out = (hn + skip_ref[:, hs] * xc[rs, hs]) * _silu(zg[rs, hs])
        hg_ref[rs, hs] = out.astype(hg_ref.dtype)

    @pl.when(c == last)
    def _():
        for gi in range(G):
            conv_ref[gi] = tails[gi]
            if kv_state:
                for hh in range(N_HEADS):
                    c_ref[gi, hh] = c_ref[gi, hh].T


def _mixer1(src, n_streams, T, states, cw, cb, wq, wk, wv, wgate, bgate, gn, skip):
    L = min(CHUNK1, T)
    nc = T // L
    G = max(1, min(n_streams, STEP_ROWS1 // L)) if nc == 1 else 1
    fused = len(src) == 3
    zero_init = states is None
    tri = jnp.asarray(np.kron(np.eye(G, dtype=np.float32), np.tril(np.ones((L, L), np.float32))), BF16)
    full = lambda a: pl.BlockSpec(a.shape, lambda s, c: (0,) * a.ndim, pipeline_mode=pl.Buffered(1))
    rows = lambda a: pl.BlockSpec((G * L, a.shape[1]), lambda s, c: (s * nc + c, 0))
    per_stream = lambda shape: pl.BlockSpec((G,) + shape[1:], lambda s, c: (s,) + (0,) * (len(shape) - 1))
    gn2, skip2, cb2 = gn.reshape(1, C_DH), skip.reshape(1, C_INNER), cb.reshape(1, C_INNER)
    state_shapes = ((n_streams, N_HEADS, C_DH, C_DH), (n_streams, N_HEADS, C_DH), (n_streams, 1, N_HEADS),
                    (n_streams, C_CONV - 1, C_INNER))
    scratch = [pltpu.VMEM((G, CONV_HDR + L, C_INNER), F32)]
    if fused:
        src = (src[0], src[1].reshape(1, -1), src[2])
        src_specs = [rows(src[0]), full(src[1]), full(src[2])]
    else:
        src_specs = [rows(src[0])]
    consts = (cw, cb2, wq, wk, wv, wgate, bgate, tri, gn2, skip2)
    if not zero_init:
        states = tuple(a.reshape(shape) for a, shape in zip(states, state_shapes))
    hg, c1, n1, m1, conv1 = pl.pallas_call(
        functools.partial(_mixer1_body, L=L, fused=fused, zero_init=zero_init, single_chunk=nc == 1),
        grid=(n_streams // G, nc),
        in_specs=src_specs + [full(a) for a in consts]
                 + ([] if zero_init else [per_stream(shape) for shape in state_shapes]),
        out_specs=[pl.BlockSpec((G * L, C_INNER), lambda s, c: (s * nc + c, 0))]
                  + [per_stream(shape) for shape in state_shapes],
        out_shape=[jax.ShapeDtypeStruct((n_streams * T, C_INNER), BF16)]
                  + [jax.ShapeDtypeStruct(shape, F32) for shape in state_shapes],
        scratch_shapes=scratch,
        compiler_params=pltpu.CompilerParams(dimension_semantics=("arbitrary", "arbitrary"),
                                             vmem_limit_bytes=VMEM_LIMIT_BYTES),
        name="mixer1_T%d" % T,
    )(*src, *consts, *(() if zero_init else states))
    return hg, c1, n1, m1.reshape(n_streams, N_HEADS), conv1


def _dense_blockdiag(w):
    rows = w.reshape(-1, BD_GROUP, C_BLOCK)
    tiled = jnp.tile(rows, (1, 1, BD_GROUP // C_BLOCK))
    idx = np.arange(BD_GROUP) // C_BLOCK
    same_block = jnp.asarray(idx[:, None] == idx[None, :])
    return jnp.where(same_block, tiled, 0.0).astype(BF16)


def kernel(x_prompt, x_sample, state_hgrn, state_ret, state_mlstm_c, state_mlstm_n, state_mlstm_m, state_conv,
           norm_mix, norm_ffn, norm_final, w_in0, lb_logits, hgrn_norm, ret_norm, w_out0,
           w_up1, conv_w, conv_b, w_q1, w_k1, w_v1, w_ig, b_ig, w_fg, b_fg, mlstm_norm, mlstm_skip, w_down1,
           w_ffn_gate, w_ffn_up, w_ffn_down):
    bp, tp, d = x_prompt.shape
    bs, ts, _ = x_sample.shape
    mp, ms = bp * tp, bs * ts
    zeros = lambda *shape: jnp.zeros(shape, F32)
    w_in, = _cast_weights([w_in0[0]])
    later_weights = [w_out0[0], w_up1[0], w_down1[0], w_ffn_gate, w_ffn_up, w_ffn_down]

    xp = x_prompt.reshape(mp, d)
    xs = x_sample.reshape(ms, d)

    mix_p, hg_p, rt_p, (w_out, w_up, w_down, wf_gate, wf_up, wf_down) = _mixer0(
        (xp, norm_mix[0], w_in), bp, tp, 0, zeros(bp, N_HEADS, HEAD_DK, HEAD_DK),
        zeros(bp, N_HEADS, HEAD_DK, HEAD_DK), lb_logits, hgrn_norm[0], ret_norm[0], 0, side_cast=later_weights)
    ffn0 = (w_out, norm_ffn[0], wf_gate, wf_up, wf_down, 0, norm_final, False)
    mix_s, hg_s, rt_s, _ = _mixer0((_norm_matmul(xs, norm_mix[0], w_in),), bs, ts, PAST_LEN, state_hgrn[0],
                                   state_ret[0], lb_logits, hgrn_norm[0], ret_norm[0], 0)
    xp, xs = _proj_ffn(xp, mix_p, xs, mix_s, *ffn0)

    ffn1 = (w_down, norm_ffn[1], wf_gate, wf_up, wf_down, 1, norm_final, True)
    wgate = jnp.pad(jnp.concatenate([w_ig[0], w_fg[0]], axis=1), ((0, 0), (0, GATE_LANES - 2 * N_HEADS))).astype(BF16)
    bgate = jnp.pad(jnp.concatenate([b_ig[0], b_fg[0]]), (0, GATE_LANES - 2 * N_HEADS)).reshape(1, GATE_LANES)
    m1_consts = (conv_w[0], conv_b[0], _dense_blockdiag(w_q1[0]), _dense_blockdiag(w_k1[0]),
                 _dense_blockdiag(w_v1[0]), wgate, bgate, mlstm_norm[0], mlstm_skip[0])
    hg1_p, mc_p, mn_p, mm_p, cv_p = _mixer1((xp, norm_mix[1], w_up), bp, tp, None, *m1_consts)
    hg1_s, mc_s, mn_s, mm_s, cv_s = _mixer1((_norm_matmul(xs, norm_mix[1], w_up),), bs, ts,
                                            (state_mlstm_c[0], state_mlstm_n[0], state_mlstm_m[0], state_conv[0]),
                                            *m1_consts)
    yp, ys = _proj_ffn(xp, hg1_p, xs, hg1_s, *ffn1)

    lead = lambda a: a[None]
    return (yp.reshape(bp, tp, d), ys.reshape(bs, ts, d),
            lead(hg_p), lead(hg_s), lead(rt_p), lead(rt_s),
            lead(mc_p), lead(mc_s), lead(mn_p), lead(mn_s), lead(mm_p), lead(mm_s), lead(cv_p), lead(cv_s))
```

```python
import functools

import numpy as np
import jax
import jax.numpy as jnp
from jax import lax
from jax.experimental import pallas as pl
from jax.experimental.pallas import tpu as pltpu

F32 = jnp.float32
BF16 = jnp.bfloat16

D_MODEL = 1024
CHUNK0 = 256
CHUNK1 = 256
STEP_ROWS0 = 128
STEP_ROWS1 = 32
SEL_BLOCK = 64
HGRN_SUB = 128
PHASE_HEADS = 2
EPS = 1e-6
SUBLANES = 8
N_HEADS = 4
HEAD_DK = 128
KW = N_HEADS * HEAD_DK
IN0_WIDTH = 8 * KW
ROPE_BASE = 10000.0
C_INNER = 2 * D_MODEL
C_DH = C_INNER // N_HEADS
C_CONV = 4
C_BLOCK = 4
BD_GROUP = 256
PAST_LEN = 2048
GATE_LANES = 128
CONV_HDR = 8
VMEM_LIMIT_BYTES = 56 * 1024 * 1024
ROW_TILE = 512
CAST_STEPS = 4
SIDE_BLOCKS = 32


def _dot(a, b):
    return jnp.dot(a.astype(BF16), b.astype(BF16), preferred_element_type=F32)


def _dot_nt(a, b):
    return lax.dot_general(a.astype(BF16), b.astype(BF16), (((1,), (1,)), ((), ())),
                           preferred_element_type=F32)


def _dot_tn(a, b):
    return lax.dot_general(a.astype(BF16), b.astype(BF16), (((0,), (0,)), ((), ())),
                           preferred_element_type=F32)


def _exact_dot(sel, x):
    hi = x.astype(BF16)
    lo = (x - hi.astype(F32)).astype(BF16)
    n = x.shape[1]
    both = jnp.dot(sel, jnp.concatenate([hi, lo], axis=1), preferred_element_type=F32)
    return both[:, :n] + both[:, n:]


def _rms(x, g):
    return x * lax.rsqrt(jnp.mean(x * x, axis=-1, keepdims=True) + EPS) * g


def _sigmoid(x):
    return 0.5 * jnp.tanh(0.5 * x) + 0.5


def _silu(x):
    return x * _sigmoid(x)


def _log_sigmoid(x):
    return jnp.minimum(x, 0.0) - jnp.log(1.0 + jnp.exp(-jnp.abs(x)))


def _norm_matmul_body(x_ref, g_ref, w_ref, o_ref):
    h = _rms(x_ref[...], g_ref[...])
    o_ref[...] = jnp.dot(h.astype(BF16), w_ref[...], preferred_element_type=F32)


def _norm_matmul(x, g, w):
    m, d = x.shape
    n = w.shape[1]
    return pl.pallas_call(
        _norm_matmul_body,
        grid=(m // ROW_TILE,),
        in_specs=[pl.BlockSpec((ROW_TILE, d), lambda i: (i, 0)),
                  pl.BlockSpec((1, d), lambda i: (0, 0)),
                  pl.BlockSpec((d, n), lambda i: (0, 0), pipeline_mode=pl.Buffered(1))],
        out_specs=pl.BlockSpec((ROW_TILE, n), lambda i: (i, 0)),
        out_shape=jax.ShapeDtypeStruct((m, n), F32),
        compiler_params=pltpu.CompilerParams(dimension_semantics=("arbitrary",),
                                             vmem_limit_bytes=VMEM_LIMIT_BYTES),
        name="norm_matmul",
    )(x, g.reshape(1, d), w)


def _cast_weights_body(*refs, n):
    ins, outs = refs[:n], refs[n:]
    n_slot = IN0_WIDTH // KW
    for hh in range(N_HEADS):
        for j in range(n_slot):
            dst = slice((hh * n_slot + j) * HEAD_DK, (hh * n_slot + j + 1) * HEAD_DK)
            src = slice((j * N_HEADS + hh) * HEAD_DK, (j * N_HEADS + hh + 1) * HEAD_DK)
            outs[0][:, dst] = ins[0][:, src].astype(BF16)
    for i_ref, o_ref in zip(ins[1:], outs[1:]):
        o_ref[...] = i_ref[...].astype(BF16)


def _cast_weights(ws):
    flat = [w.reshape(-1, w.shape[-1]) for w in ws]
    spec = lambda w: pl.BlockSpec((w.shape[0] // CAST_STEPS, w.shape[1]), lambda i: (i, 0))
    out = pl.pallas_call(
        functools.partial(_cast_weights_body, n=len(flat)),
        grid=(CAST_STEPS,),
        in_specs=[spec(w) for w in flat],
        out_specs=[spec(w) for w in flat],
        out_shape=[jax.ShapeDtypeStruct(w.shape, BF16) for w in flat],
        compiler_params=pltpu.CompilerParams(dimension_semantics=("arbitrary",),
                                             vmem_limit_bytes=VMEM_LIMIT_BYTES),
        name="cast_weights",
    )(*flat)
    return [o.reshape(w.shape) for o, w in zip(out, ws)]


def _proj_ffn_body(x_ref, a_ref, wo_ref, g_ref, wg_ref, wu_ref, wd_ref, gf_ref, o_ref, *, final_norm):
    x1 = x_ref[...] + jnp.dot(a_ref[...], wo_ref[...], preferred_element_type=F32)
    h = _rms(x1, g_ref[...]).astype(BF16)
    gate = jnp.dot(h, wg_ref[...], preferred_element_type=F32)
    up = jnp.dot(h, wu_ref[...], preferred_element_type=F32)
    t = (_silu(gate) * up).astype(BF16)
    x2 = x1 + jnp.dot(t, wd_ref[...], preferred_element_type=F32)
    if final_norm:
        x2 = _rms(x2, gf_ref[...])
    o_ref[...] = x2


def _proj_ffn_pair_body(xa_ref, aa_ref, xb_ref, ab_ref, wo_ref, g_ref, wg_ref, wu_ref, wd_ref, gf_ref,
                        oa_ref, ob_ref, *, final_norm, n_a):
    i = pl.program_id(0)
    shared = (wo_ref, g_ref, wg_ref, wu_ref, wd_ref, gf_ref)

    @pl.when(i < n_a)
    def _():
        _proj_ffn_body(xa_ref, aa_ref, *shared, oa_ref, final_norm=final_norm)

    @pl.when(i >= n_a)
    def _():
        _proj_ffn_body(xb_ref, ab_ref, *shared, ob_ref, final_norm=final_norm)


def _proj_ffn(xa, aa, xb, ab, wo, g, wg, wu, wd, layer, gf, final_norm):
    d = xa.shape[1]
    ka = aa.shape[1]
    ff = wg.shape[2]
    n_a, n_b = xa.shape[0] // ROW_TILE, xb.shape[0] // ROW_TILE
    const = lambda shape: pl.BlockSpec(shape, lambda i: (0, 0), pipeline_mode=pl.Buffered(1))
    of_layer = lambda shape: pl.BlockSpec((None,) + shape, lambda i: (layer, 0, 0), pipeline_mode=pl.Buffered(1))
    rows_a = lambda w: pl.BlockSpec((ROW_TILE, w), lambda i: (jnp.minimum(i, n_a - 1), 0))
    rows_b = lambda w: pl.BlockSpec((ROW_TILE, w), lambda i: (jnp.maximum(i - n_a, 0), 0))
    return pl.pallas_call(
        functools.partial(_proj_ffn_pair_body, final_norm=final_norm, n_a=n_a),
        grid=(n_a + n_b,),
        in_specs=[rows_a(d), rows_a(ka), rows_b(d), rows_b(ka),
                  const((ka, d)), const((1, d)), of_layer((d, ff)), of_layer((d, ff)), of_layer((ff, d)),
                  const((1, d))],
        out_specs=[rows_a(d), rows_b(d)],
        out_shape=[jax.ShapeDtypeStruct(xa.shape, F32), jax.ShapeDtypeStruct(xb.shape, F32)],
        compiler_params=pltpu.CompilerParams(dimension_semantics=("arbitrary",),
                                             vmem_limit_bytes=VMEM_LIMIT_BYTES),
        name="proj_ffn",
    )(xa, aa, xb, ab, wo, g.reshape(1, d), wg, wu, wd, gf.reshape(1, d))


def _hgrn_consts(L):
    r = np.arange(L)[:, None]
    t = np.arange(L)[None, :]
    sels = [t <= r]
    masks = [r == t]
    h = L // 2
    while h >= 1:
        base = (r // (2 * h)) * (2 * h)
        upper = (r % (2 * h)) >= h
        if h < SUBLANES:
            sels.append(np.where(upper, (t >= base + h) & (t <= r), (t > r) & (t <= base + h - 1)))
        masks.append((r // (2 * h) == t // (2 * h)) & upper & ((t % (2 * h)) < h))
        h //= 2
    return np.concatenate(sels, axis=0).astype(np.float32), np.stack(masks).astype(np.float32)


def _decay_exponents(sel_ref, logf, L):
    blk = min(SEL_BLOCK, L)
    nb = L // blk
    res = [_exact_dot(sel_ref[...], logf[i * blk:(i + 1) * blk]) for i in range(nb)]
    cum = [r[0:blk] for r in res]
    pre = [jnp.zeros_like(cum[0][0:1])]
    for i in range(nb):
        pre.append(pre[i] + cum[i][blk - 1:blk])
    cat = lambda parts: jnp.concatenate(parts, axis=0) if len(parts) > 1 else parts[0]
    b = cat([cum[i] + pre[i] for i in range(nb)])
    levels = []
    nbh = nb // 2
    while nbh >= 1:
        parts = []
        for i in range(nb):
            ref = pre[(i // (2 * nbh)) * 2 * nbh + nbh]
            if (i % (2 * nbh)) >= nbh:
                parts.append(cum[i] + (pre[i] - ref))
            else:
                parts.append((ref - pre[i]) - cum[i])
        levels.append(cat(parts))
        nbh //= 2
    h = blk // 2
    while h >= SUBLANES:
        parts = []
        for i in range(nb):
            for base in range(0, blk, 2 * h):
                ref = cum[i][base + h - 1:base + h]
                parts += [ref - cum[i][base:base + h], cum[i][base + h:base + 2 * h] - ref]
        levels.append(cat(parts))
        h //= 2
    for lev in range(sel_ref.shape[0] // blk - 1):
        levels.append(cat([r[(lev + 1) * blk:(lev + 2) * blk] for r in res]))
    return b, pre[nb], levels


def _retention_consts(L):
    lg = np.log1p(-np.exp2(-5.0 - np.arange(N_HEADS, dtype=np.float32))).astype(np.float32)
    idx = np.arange(L, dtype=np.float32)
    rel = idx[:, None] - idx[None, :]
    dmat = np.where(rel >= 0, np.exp(lg[:, None, None] * np.maximum(rel, 0.0)), 0.0)
    w_in = np.exp(lg[:, None] * (idx + 1.0))
    w_tail = np.exp(lg[:, None] * (L - 1.0 - idx))
    bcast = lambda v: jnp.asarray(np.broadcast_to(v[:, :, None], (N_HEADS, L, HEAD_DK)).astype(np.float32))
    g_chunk = [float(v) for v in np.exp(lg * L).astype(np.float32)]
    return jnp.asarray(dmat.astype(np.float32)), bcast(w_in), bcast(w_tail), g_chunk


def _rope_tables(pos0, T):
    half = HEAD_DK // 2
    inv = ROPE_BASE ** (-jnp.arange(half, dtype=F32) / half)
    ang = (jnp.arange(T) + pos0).astype(F32)[:, None] * inv[None, :]
    cos, sin = jnp.cos(ang), jnp.sin(ang)
    return jnp.concatenate([cos, cos], axis=-1), jnp.concatenate([-sin, sin], axis=-1)


def _hgrn_intra(units, mask_ref, L, fillers=()):
    sub = min(L, HGRN_SUB)
    n_sub = L // sub
    n_lev = mask_ref.shape[0] - 1
    top = len(units[0][3]) - n_lev
    assert n_sub in (1, 2) and top == n_sub - 1
    pieces = [(u, slice(i * sub, (i + 1) * sub)) for u in range(len(units)) for i in range(n_sub)]
    atts = [mask_ref[0] * _dot_nt(units[u][0][r], units[u][1][r]) for u, r in pieces]
    fillers = list(fillers)
    every = max(1, n_lev // (len(fillers) + 1))
    for lev in range(n_lev):
        for p, (u, r) in enumerate(pieces):
            q, k, _, expo = units[u]
            sc = jnp.exp(expo[top + lev][r])
            atts[p] = atts[p] + mask_ref[lev + 1] * _dot_nt(q[r] * sc, k[r] * sc)
        if fillers and (lev + 1) % every == 0:
            fillers.pop(0)()
    while fillers:
        fillers.pop(0)()
    if n_sub == 1:
        return [_dot(att, units[u][2]) for att, (u, _) in zip(atts, pieces)]
    outs = []
    for u, (q, k, v, expo) in enumerate(units):
        cross = _dot_nt(q[sub:] * jnp.exp(expo[0][sub:]), k[:sub] * jnp.exp(expo[0][:sub]))
        outs.append(jnp.concatenate([_dot(atts[2 * u], v[:sub]),
                                     _dot(jnp.concatenate([cross, atts[2 * u + 1]], axis=1), v)], axis=0))
    return outs


def _mixer0_body(*refs, L, layer_slot, g_chunk, fused, n_side, side_every):
    it = iter(refs)
    take = lambda n: [next(it) for _ in range(n)]
    src = take(3 if fused else 1)
    (cos_ref, sin_ref, lbl_ref, ga_ref, gb_ref, sel_ref, mask_ref, dmat_ref, win_ref, wtail_ref,
     sa0_ref, sb0_ref) = take(12)
    side_in = take(n_side)
    mix_ref, sa_ref, sb_ref = take(3)
    side_out = take(n_side)
    sat_scr, sbt_scr = take(2)
    c = pl.program_id(1)
    last = pl.num_programs(1) - 1
    head_w = IN0_WIDTH // N_HEADS
    G = sa_ref.shape[0]

    if n_side:
        @pl.when((pl.program_id(0) * pl.num_programs(1) + c) % side_every == 0)
        def _():
            for i_ref, o_ref in zip(side_in, side_out):
                o_ref[...] = i_ref[...].astype(BF16)

    @pl.when(c == 0)
    def _():
        for gi in range(G):
            for hh in range(N_HEADS):
                sat_scr[gi * N_HEADS + hh] = sa0_ref[gi, hh].T
                sbt_scr[gi * N_HEADS + hh] = sb0_ref[gi, hh].T

    lbl = lbl_ref[...]
    e = jnp.exp(lbl - jnp.max(lbl, axis=0, keepdims=True))
    lb_all = jnp.sum(e[:layer_slot + 1], axis=0, keepdims=True) / jnp.sum(e, axis=0, keepdims=True)

    if fused:
        x_ref, g_ref, w_ref = src
        hb = _rms(x_ref[...], g_ref[...]).astype(BF16)
    cosf = cos_ref[...]
    sinf = sin_ref[...]

    def project(hh):
        hcols = slice(hh * head_w, (hh + 1) * head_w)
        return jnp.dot(hb, w_ref[:, hcols], preferred_element_type=F32) if fused else src[0][:, hcols]

    def run_phase(heads, z_of, fillers):
        hg_units, ret_units = [], []
        for hh in heads:
            hs = slice(hh * HEAD_DK, (hh + 1) * HEAD_DK)
            lb = lb_all[:, hs]
            for gi in range(G):
                rs = slice(gi * L, (gi + 1) * L)
                si = gi * N_HEADS + hh
                zh = z_of[hh][rs]
                part = lambda j, zh=zh: zh[:, j * HEAD_DK:(j + 1) * HEAD_DK]
                f = lb + (1.0 - lb) * _sigmoid(part(1))
                b, b_last, expo = _decay_exponents(sel_ref, jnp.log(f), L)
                hg_units.append((rs, hs, si, part(0), 1.0 - f, part(2), part(3), b, b_last, expo))
                rq = part(4)
                rk = part(5)
                rq = rq * cosf + pltpu.roll(rq, HEAD_DK // 2, 1) * sinf
                rk = (rk * cosf + pltpu.roll(rk, HEAD_DK // 2, 1) * sinf) * (HEAD_DK ** -0.5)
                ret_units.append((rs, hh, si, rq, rk, part(6), part(7)))

        states = [sat_scr[u[2]] for u in hg_units]
        inter = [_dot_nt(u[3] * jnp.exp(u[7]), st) for u, st in zip(hg_units, states)]
        intra = _hgrn_intra([(u[3], u[4], u[5], u[9]) for u in hg_units], mask_ref, L, fillers)
        for (rs, hs, si, q, k, v, og, b, b_last, _), st, o1, o2 in zip(hg_units, states, inter, intra):
            sat_scr[si] = st * jnp.exp(b_last) + _dot_tn(v, k * jnp.exp(b_last - b))
            o = _rms((o1 + o2) * _sigmoid(og), ga_ref[...])
            mix_ref[rs, hs] = o.astype(mix_ref.dtype)

        states = [sbt_scr[si] for _, _, si, _, _, _, _ in ret_units]
        scores = [_dot_nt(rq, rk) * dmat_ref[hh] for _, hh, _, rq, rk, _, _ in ret_units]
        inter = [_dot_nt(u[3], rt) for u, rt in zip(ret_units, states)]
        outs = [_dot(sc, u[5]) + win_ref[u[1]] * it_ for u, sc, it_ in zip(ret_units, scores, inter)]
        for (rs, hh, si, rq, rk, rv, rg), rt, ro in zip(ret_units, states, outs):
            sbt_scr[si] = g_chunk[hh] * rt + _dot_tn(rv, rk * wtail_ref[hh])
            ro = _rms(ro, gb_ref[...]) * _silu(rg)
            mix_ref[rs, KW + hh * HEAD_DK:KW + (hh + 1) * HEAD_DK] = ro.astype(mix_ref.dtype)

    if fused:
        per = PHASE_HEADS
        z_of = {hh: project(hh) for hh in range(per)}
        for h0 in range(0, N_HEADS, per):
            nxt = range(h0 + per, min(h0 + 2 * per, N_HEADS))
            run_phase(range(h0, h0 + per), z_of, [lambda hh=hh: z_of.__setitem__(hh, project(hh)) for hh in nxt])
    else:
        run_phase(range(N_HEADS), {hh: project(hh) for hh in range(N_HEADS)}, [])

    @pl.when(c == last)
    def _():
        for gi in range(G):
            for hh in range(N_HEADS):
                sa_ref[gi, hh] = sat_scr[gi * N_HEADS + hh].T
                sb_ref[gi, hh] = sbt_scr[gi * N_HEADS + hh].T


def _mixer0(src, n_streams, T, pos0, sa0, sb0, lb_logits, ga, gb, layer_slot, side_cast=()):
    L = min(CHUNK0, T)
    nc = T // L
    G = max(1, min(n_streams, STEP_ROWS0 // L)) if nc == 1 else 1
    fused = len(src) == 3
    sel = jnp.asarray(_hgrn_consts(min(SEL_BLOCK, L))[0], BF16)
    masks = jnp.asarray(_hgrn_consts(min(HGRN_SUB, L))[1])
    dmat, w_in, w_tail, g_chunk = _retention_consts(L)
    cosf, sinf = _rope_tables(pos0, T)
    full = lambda a: pl.BlockSpec(a.shape, lambda s, c: (0,) * a.ndim, pipeline_mode=pl.Buffered(1))
    rows = lambda a: pl.BlockSpec((G * L, a.shape[1]), lambda s, c: (s * nc + c, 0))
    state_spec = pl.BlockSpec((G, N_HEADS, HEAD_DK, HEAD_DK), lambda s, c: (s, 0, 0, 0))
    state_shape = jax.ShapeDtypeStruct((n_streams, N_HEADS, HEAD_DK, HEAD_DK), F32)
    ga2, gb2 = ga.reshape(1, HEAD_DK), gb.reshape(1, HEAD_DK)
    if fused:
        src = (src[0], src[1].reshape(1, -1), src[2])
        src_specs = [rows(src[0]), full(src[1]), full(src[2])]
    else:
        src_specs = [rows(src[0])]
    consts = (lb_logits, ga2, gb2, sel, masks, dmat, w_in, w_tail)
    side = [w.reshape(-1, w.shape[-1]) for w in side_cast]
    per_block = (n_streams // G) * nc // SIDE_BLOCKS if side else 1
    side_spec = lambda w: pl.BlockSpec((w.shape[0] // SIDE_BLOCKS, w.shape[1]),
                                       lambda s, c: ((s * nc + c) // per_block, 0))
    outs = pl.pallas_call(
        functools.partial(_mixer0_body, L=L, layer_slot=layer_slot, g_chunk=g_chunk, fused=fused,
                          n_side=len(side), side_every=per_block),
        grid=(n_streams // G, nc),
        in_specs=src_specs
                 + [pl.BlockSpec((L, HEAD_DK), lambda s, c: (c, 0)), pl.BlockSpec((L, HEAD_DK), lambda s, c: (c, 0))]
                 + [full(a) for a in consts] + [state_spec, state_spec] + [side_spec(w) for w in side],
        out_specs=[pl.BlockSpec((G * L, 2 * KW), lambda s, c: (s * nc + c, 0)), state_spec, state_spec]
                  + [side_spec(w) for w in side],
        out_shape=[jax.ShapeDtypeStruct((n_streams * T, 2 * KW), BF16), state_shape, state_shape]
                  + [jax.ShapeDtypeStruct(w.shape, BF16) for w in side],
        scratch_shapes=[pltpu.VMEM((G * N_HEADS, HEAD_DK, HEAD_DK), F32),
                        pltpu.VMEM((G * N_HEADS, HEAD_DK, HEAD_DK), F32)],
        compiler_params=pltpu.CompilerParams(dimension_semantics=("arbitrary", "arbitrary"),
                                             vmem_limit_bytes=VMEM_LIMIT_BYTES),
        name="mixer0_T%d" % T,
    )(*src, cosf, sinf, *consts, sa0, sb0, *side)
    return outs[0], outs[1], outs[2], [o.reshape(w.shape) for o, w in zip(outs[3:], side_cast)]


def _mixer1_body(*refs, L, fused, zero_init, single_chunk):
    it = iter(refs)
    take = lambda n: [next(it) for _ in range(n)]
    src = take(3 if fused else 1)
    (cw_ref, cb_ref, wq_ref, wk_ref, wv_ref, wgate_ref, bgate_ref, tri_ref, gn_ref, skip_ref) = take(10)
    init = None if zero_init else take(4)
    hg_ref, c_ref, n_ref, m_ref, conv_ref = take(5)
    xbuf = take(1)[0]
    c = pl.program_id(1)
    last = pl.num_programs(1) - 1
    n_taps = C_CONV - 1
    G = c_ref.shape[0]
    kv_state = zero_init

    @pl.when(c == 0)
    def _():
        if zero_init:
            c_ref[...] = jnp.zeros_like(c_ref)
            n_ref[...] = jnp.zeros_like(n_ref)
            m_ref[...] = jnp.zeros_like(m_ref)
            xbuf[:, 0:CONV_HDR, :] = jnp.zeros((G, CONV_HDR, C_INNER), F32)
        else:
            c0_ref, n0_ref, m0_ref, conv0_ref = init
            if not single_chunk:
                c_ref[...] = c0_ref[...]
                n_ref[...] = n0_ref[...]
                m_ref[...] = m0_ref[...]
            xbuf[:, CONV_HDR - n_taps:CONV_HDR, :] = conv0_ref[...]

    c_in, n_in, m_in = init[:3] if single_chunk and not zero_init else (c_ref, n_ref, m_ref)

    if fused:
        x_ref, g_ref, w_ref = src
        hb = _rms(x_ref[...], g_ref[...]).astype(BF16)
        xm = jnp.dot(hb, w_ref[:, :C_INNER], preferred_element_type=F32)
        zg = jnp.dot(hb, w_ref[:, C_INNER:], preferred_element_type=F32)
    else:
        xm = src[0][:, :C_INNER]
        zg = src[0][:, C_INNER:]

    xcs, tails = [], []
    for gi in range(G):
        xg = xm[gi * L:(gi + 1) * L]
        xbuf[gi, CONV_HDR:CONV_HDR + L, :] = xg
        acc = cb_ref[...] + xg * cw_ref[n_taps:n_taps + 1, :]
        for w in range(n_taps):
            acc = acc + xbuf[gi, CONV_HDR - n_taps + w:CONV_HDR - n_taps + w + L, :] * cw_ref[w:w + 1, :]
        tails.append(xbuf[gi, CONV_HDR + L - n_taps:CONV_HDR + L, :])
        xbuf[gi, CONV_HDR - n_taps:CONV_HDR, :] = tails[gi]
        xcs.append(acc)
    xc = _silu(jnp.concatenate(xcs, axis=0) if G > 1 else xcs[0])

    xcb = xc.astype(BF16)
    xmb = xm.astype(BF16)
    qs, ks, vs = [], [], []
    for g in range(C_INNER // BD_GROUP):
        gs = slice(g * BD_GROUP, (g + 1) * BD_GROUP)
        qs.append(jnp.dot(xcb[:, gs], wq_ref[g], preferred_element_type=F32))
        ks.append(jnp.dot(xcb[:, gs], wk_ref[g], preferred_element_type=F32))
        vs.append(jnp.dot(xmb[:, gs], wv_ref[g], preferred_element_type=F32))
    q = jnp.concatenate(qs, axis=1).astype(BF16)
    k = (jnp.concatenate(ks, axis=1) * (C_DH ** -0.5)).astype(BF16)
    v = jnp.concatenate(vs, axis=1).astype(BF16)

    gates = (_dot(q, wgate_ref[0:C_INNER, :]) + _dot(k, wgate_ref[C_INNER:2 * C_INNER, :])
             + _dot(v, wgate_ref[2 * C_INNER:3 * C_INNER, :]) + bgate_ref[...])
    bcum = _exact_dot(tri_ref[...], _log_sigmoid(gates))
    lane = lax.broadcasted_iota(jnp.int32, gates.shape, 1)
    rows = jnp.where(lane < N_HEADS, gates, bcum).T
    ti = lax.broadcasted_iota(jnp.int32, (L, L), 0)
    si = lax.broadcasted_iota(jnp.int32, (L, L), 1)
    causal = si <= ti

    units = []
    for gi, hh in [(gi, hh) for gi in range(G) for hh in range(N_HEADS)]:
        hs = slice(hh * C_DH, (hh + 1) * C_DH)
        rs = slice(gi * L, (gi + 1) * L)
        b_col = bcum[rs, N_HEADS + hh:N_HEADS + hh + 1]
        i_col = gates[rs, hh:hh + 1]
        b_row = rows[N_HEADS + hh:N_HEADS + hh + 1, rs]
        i_row = rows[hh:hh + 1, rs]
        m_prev = m_in[gi, :, hh:hh + 1]
        lw = jnp.where(causal, b_col - b_row + i_row, -jnp.inf)
        lp = b_col + m_prev
        mj = jnp.maximum(lp, jnp.max(lw, axis=-1, keepdims=True))
        m_new = mj[L - 1:L, :]
        b_last = b_col[L - 1:L, :]
        units.append(dict(
            gi=gi, hh=hh, hs=hs, rs=rs, mj=mj, m_new=m_new, wgt=jnp.exp(lw - mj), wp=jnp.exp(lp - mj),
            ws=jnp.exp(b_last - b_col + i_col - m_new),
            wpl=jnp.exp(b_last + m_prev - m_new),
            cm=c_in[gi, hh],
            nv=n_in[gi, hh:hh + 1, :],
            q=q[rs, hs], k=k[rs, hs], v=v[rs, hs]))

    scores = [_dot_nt(u['q'], u['k']) * u['wgt'] for u in units]
    inter = [_dot(u['q'], u['cm']) if kv_state else _dot_nt(u['q'], u['cm']) for u in units]
    cells = []
    for u, s, it_ in zip(units, scores, inter):
        num = _dot(s, u['v']) + u['wp'] * it_
        den = jnp.sum(s, axis=-1, keepdims=True) + u['wp'] * jnp.sum(u['q'] * u['nv'], axis=-1, keepdims=True)
        cells.append(num / jnp.maximum(jnp.abs(den), jnp.exp(-u['mj'])))
    for u in units:
        gi, hh, vw = u['gi'], u['hh'], u['v'] * u['ws']
        c_ref[gi, hh] = u['wpl'] * u['cm'] + (_dot_tn(u['k'], vw) if kv_state else _dot_tn(vw, u['k']))
        n_ref[gi, hh:hh + 1, :] = u['wpl'] * u['nv'] + jnp.sum(u['k'] * u['ws'], axis=0, keepdims=True)
        m_ref[gi, :, hh:hh + 1] = u['m_new']
    for u, hcell in zip(units, cells):
        hs, rs = u['hs'], u['rs']
        hc = hcell - jnp.mean(hcell, axis=-1, keepdims=True)
        hn = hc * lax.rsqrt(jnp.mean(hc * hc, axis=-1, keepdims=True) + EPS) * gn_ref[...]
        out = (hn + skip_ref[:, hs] * xc[rs, hs]) * _silu(zg[rs, hs])
        hg_ref[rs, hs] = out.astype(hg_ref.dtype)

    @pl.when(c == last)
    def _():
        for gi in range(G):
            conv_ref[gi] = tails[gi]
            if kv_state:
                for hh in range(N_HEADS):
                    c_ref[gi, hh] = c_ref[gi, hh].T


def _mixer1(src, n_streams, T, states, cw, cb, wq, wk, wv, wgate, bgate, gn, skip):
    L = min(CHUNK1, T)
    nc = T // L
    G = max(1, min(n_streams, STEP_ROWS1 // L)) if nc == 1 else 1
    fused = len(src) == 3
    zero_init = states is None
    tri = jnp.asarray(np.kron(np.eye(G, dtype=np.float32), np.tril(np.ones((L, L), np.float32))), BF16)
    full = lambda a: pl.BlockSpec(a.shape, lambda s, c: (0,) * a.ndim, pipeline_mode=pl.Buffered(1))
    rows = lambda a: pl.BlockSpec((G * L, a.shape[1]), lambda s, c: (s * nc + c, 0))
    per_stream = lambda shape: pl.BlockSpec((G,) + shape[1:], lambda s, c: (s,) + (0,) * (len(shape) - 1))
    gn2, skip2, cb2 = gn.reshape(1, C_DH), skip.reshape(1, C_INNER), cb.reshape(1, C_INNER)
    state_shapes = ((n_streams, N_HEADS, C_DH, C_DH), (n_streams, N_HEADS, C_DH), (n_streams, 1, N_HEADS),
                    (n_streams, C_CONV - 1, C_INNER))
    scratch = [pltpu.VMEM((G, CONV_HDR + L, C_INNER), F32)]
    if fused:
        src = (src[0], src[1].reshape(1, -1), src[2])
        src_specs = [rows(src[0]), full(src[1]), full(src[2])]
    else:
        src_specs = [rows(src[0])]
    consts = (cw, cb2, wq, wk, wv, wgate, bgate, tri, gn2, skip2)
    if not zero_init:
        states = tuple(a.reshape(shape) for a, shape in zip(states, state_shapes))
    hg, c1, n1, m1, conv1 = pl.pallas_call(
        functools.partial(_mixer1_body, L=L, fused=fused, zero_init=zero_init, single_chunk=nc == 1),
        grid=(n_streams // G, nc),
        in_specs=src_specs + [full(a) for a in consts]
                 + ([] if zero_init else [per_stream(shape) for shape in state_shapes]),
        out_specs=[pl.BlockSpec((G * L, C_INNER), lambda s, c: (s * nc + c, 0))]
                  + [per_stream(shape) for shape in state_shapes],
        out_shape=[jax.ShapeDtypeStruct((n_streams * T, C_INNER), BF16)]
                  + [jax.ShapeDtypeStruct(shape, F32) for shape in state_shapes],
        scratch_shapes=scratch,
        compiler_params=pltpu.CompilerParams(dimension_semantics=("arbitrary", "arbitrary"),
                                             vmem_limit_bytes=VMEM_LIMIT_BYTES),
        name="mixer1_T%d" % T,
    )(*src, *consts, *(() if zero_init else states))
    return hg, c1, n1, m1.reshape(n_streams, N_HEADS), conv1


def _dense_blockdiag(w):
    rows = w.reshape(-1, BD_GROUP, C_BLOCK)
    tiled = jnp.tile(rows, (1, 1, BD_GROUP // C_BLOCK))
    idx = np.arange(BD_GROUP) // C_BLOCK
    same_block = jnp.asarray(idx[:, None] == idx[None, :])
    return jnp.where(same_block, tiled, 0.0).astype(BF16)


def kernel(x_prompt, x_sample, state_hgrn, state_ret, state_mlstm_c, state_mlstm_n, state_mlstm_m, state_conv,
           norm_mix, norm_ffn, norm_final, w_in0, lb_logits, hgrn_norm, ret_norm, w_out0,
           w_up1, conv_w, conv_b, w_q1, w_k1, w_v1, w_ig, b_ig, w_fg, b_fg, mlstm_norm, mlstm_skip, w_down1,
           w_ffn_gate, w_ffn_up, w_ffn_down):
    bp, tp, d = x_prompt.shape
    bs, ts, _ = x_sample.shape
    mp, ms = bp * tp, bs * ts
    zeros = lambda *shape: jnp.zeros(shape, F32)
    w_in, = _cast_weights([w_in0[0]])
    later_weights = [w_out0[0], w_up1[0], w_down1[0], w_ffn_gate, w_ffn_up, w_ffn_down]

    xp = x_prompt.reshape(mp, d)
    xs = x_sample.reshape(ms, d)

    mix_p, hg_p, rt_p, (w_out, w_up, w_down, wf_gate, wf_up, wf_down) = _mixer0(
        (xp, norm_mix[0], w_in), bp, tp, 0, zeros(bp, N_HEADS, HEAD_DK, HEAD_DK),
        zeros(bp, N_HEADS, HEAD_DK, HEAD_DK), lb_logits, hgrn_norm[0], ret_norm[0], 0, side_cast=later_weights)
    ffn0 = (w_out, norm_ffn[0], wf_gate, wf_up, wf_down, 0, norm_final, False)
    mix_s, hg_s, rt_s, _ = _mixer0((_norm_matmul(xs, norm_mix[0], w_in),), bs, ts, PAST_LEN, state_hgrn[0],
                                   state_ret[0], lb_logits, hgrn_norm[0], ret_norm[0], 0)
    xp, xs = _proj_ffn(xp, mix_p, xs, mix_s, *ffn0)

    ffn1 = (w_down, norm_ffn[1], wf_gate, wf_up, wf_down, 1, norm_final, True)
    wgate = jnp.pad(jnp.concatenate([w_ig[0], w_fg[0]], axis=1), ((0, 0), (0, GATE_LANES - 2 * N_HEADS))).astype(BF16)
    bgate = jnp.pad(jnp.concatenate([b_ig[0], b_fg[0]]), (0, GATE_LANES - 2 * N_HEADS)).reshape(1, GATE_LANES)
    m1_consts = (conv_w[0], conv_b[0], _dense_blockdiag(w_q1[0]), _dense_blockdiag(w_k1[0]),
                 _dense_blockdiag(w_v1[0]), wgate, bgate, mlstm_norm[0], mlstm_skip[0])
    hg1_p, mc_p, mn_p, mm_p, cv_p = _mixer1((xp, norm_mix[1], w_up), bp, tp, None, *m1_consts)
    hg1_s, mc_s, mn_s, mm_s, cv_s = _mixer1((_norm_matmul(xs, norm_mix[1], w_up),), bs, ts,
                                            (state_mlstm_c[0], state_mlstm_n[0], state_mlstm_m[0], state_conv[0]),
                                            *m1_consts)
    yp, ys = _proj_ffn(xp, hg1_p, xs, hg1_s, *ffn1)

    lead = lambda a: a[None]
    return (yp.reshape(bp, tp, d), ys.reshape(bs, ts, d),
            lead(hg_p), lead(hg_s), lead(rt_p), lead(rt_s),
            lead(mc_p), lead(mc_s), lead(mn_p), lead(mn_s), lead(mm_p), lead(mm_s), lead(cv_p), lead(cv_s))
```

```python
import functools

import numpy as np
import jax
import jax.numpy as jnp
from jax import lax
from jax.experimental import pallas as pl
from jax.experimental.pallas import tpu as pltpu

F32 = jnp.float32
BF16 = jnp.bfloat16

D_MODEL = 1024
CHUNK0 = 256
CHUNK1 = 256
STEP_ROWS0 = 128
STEP_ROWS1 = 32
SEL_BLOCK = 64
HGRN_SUB = 128
PHASE_HEADS = 2
EPS = 1e-6
SUBLANES = 8
N_HEADS = 4
HEAD_DK = 128
KW = N_HEADS * HEAD_DK
IN0_WIDTH = 8 * KW
ROPE_BASE = 10000.0
C_INNER = 2 * D_MODEL
C_DH = C_INNER // N_HEADS
C_CONV = 4
C_BLOCK = 4
BD_GROUP = 256
PAST_LEN = 2048
GATE_LANES = 128
CONV_HDR = 8
VMEM_LIMIT_BYTES = 56 * 1024 * 1024
ROW_TILE = 512
CAST_STEPS = 4
SIDE_BLOCKS = 32


def _dot(a, b):
    return jnp.dot(a.astype(BF16), b.astype(BF16), preferred_element_type=F32)


def _dot_nt(a, b):
    return lax.dot_general(a.astype(BF16), b.astype(BF16), (((1,), (1,)), ((), ())),
                           preferred_element_type=F32)


def _dot_tn(a, b):
    return lax.dot_general(a.astype(BF16), b.astype(BF16), (((0,), (0,)), ((), ())),
                           preferred_element_type=F32)


def _exact_dot(sel, x):
    hi = x.astype(BF16)
    lo = (x - hi.astype(F32)).astype(BF16)
    n = x.shape[1]
    both = jnp.dot(sel, jnp.concatenate([hi, lo], axis=1), preferred_element_type=F32)
    return both[:, :n] + both[:, n:]


def _rms(x, g):
    return x * lax.rsqrt(jnp.mean(x * x, axis=-1, keepdims=True) + EPS) * g


def _sigmoid(x):
    return 0.5 * jnp.tanh(0.5 * x) + 0.5


def _silu(x):
    h = 0.5 * x
    return h * jnp.tanh(h) + h


def _log_sigmoid(x):
    return jnp.minimum(x, 0.0) - jnp.log(1.0 + jnp.exp(-jnp.abs(x)))


def _norm_matmul_body(x_ref, g_ref, w_ref, o_ref):
    h = _rms(x_ref[...], g_ref[...])
    o_ref[...] = jnp.dot(h.astype(BF16), w_ref[...], preferred_element_type=F32)


def _norm_matmul(x, g, w):
    m, d = x.shape
    n = w.shape[1]
    return pl.pallas_call(
        _norm_matmul_body,
        grid=(m // ROW_TILE,),
        in_specs=[pl.BlockSpec((ROW_TILE, d), lambda i: (i, 0)),
                  pl.BlockSpec((1, d), lambda i: (0, 0)),
                  pl.BlockSpec((d, n), lambda i: (0, 0), pipeline_mode=pl.Buffered(1))],
        out_specs=pl.BlockSpec((ROW_TILE, n), lambda i: (i, 0)),
        out_shape=jax.ShapeDtypeStruct((m, n), F32),
        compiler_params=pltpu.CompilerParams(dimension_semantics=("arbitrary",),
                                             vmem_limit_bytes=VMEM_LIMIT_BYTES),
        name="norm_matmul",
    )(x, g.reshape(1, d), w)


def _cast_weights_body(*refs, n):
    ins, outs = refs[:n], refs[n:]
    n_slot = IN0_WIDTH // KW
    for hh in range(N_HEADS):
        for j in range(n_slot):
            dst = slice((hh * n_slot + j) * HEAD_DK, (hh * n_slot + j + 1) * HEAD_DK)
            src = slice((j * N_HEADS + hh) * HEAD_DK, (j * N_HEADS + hh + 1) * HEAD_DK)
            outs[0][:, dst] = ins[0][:, src].astype(BF16)
    for i_ref, o_ref in zip(ins[1:], outs[1:]):
        o_ref[...] = i_ref[...].astype(BF16)


def _cast_weights(ws):
    flat = [w.reshape(-1, w.shape[-1]) for w in ws]
    spec = lambda w: pl.BlockSpec((w.shape[0] // CAST_STEPS, w.shape[1]), lambda i: (i, 0))
    out = pl.pallas_call(
        functools.partial(_cast_weights_body, n=len(flat)),
        grid=(CAST_STEPS,),
        in_specs=[spec(w) for w in flat],
        out_specs=[spec(w) for w in flat],
        out_shape=[jax.ShapeDtypeStruct(w.shape, BF16) for w in flat],
        compiler_params=pltpu.CompilerParams(dimension_semantics=("arbitrary",),
                                             vmem_limit_bytes=VMEM_LIMIT_BYTES),
        name="cast_weights",
    )(*flat)
    return [o.reshape(w.shape) for o, w in zip(out, ws)]


def _proj_ffn_body(x_ref, a_ref, wo_ref, g_ref, wg_ref, wu_ref, wd_ref, gf_ref, o_ref, *, final_norm):
    x1 = x_ref[...] + jnp.dot(a_ref[...], wo_ref[...], preferred_element_type=F32)
    h = _rms(x1, g_ref[...]).astype(BF16)
    gate = jnp.dot(h, wg_ref[...], preferred_element_type=F32)
    up = jnp.dot(h, wu_ref[...], preferred_element_type=F32)
    t = (_silu(gate) * up).astype(BF16)
    x2 = x1 + jnp.dot(t, wd_ref[...], preferred_element_type=F32)
    if final_norm:
        x2 = _rms(x2, gf_ref[...])
    o_ref[...] = x2


def _proj_ffn_pair_body(xa_ref, aa_ref, xb_ref, ab_ref, wo_ref, g_ref, wg_ref, wu_ref, wd_ref, gf_ref,
                        oa_ref, ob_ref, *, final_norm, n_a):
    i = pl.program_id(0)
    shared = (wo_ref, g_ref, wg_ref, wu_ref, wd_ref, gf_ref)

    @pl.when(i < n_a)
    def _():
        _proj_ffn_body(xa_ref, aa_ref, *shared, oa_ref, final_norm=final_norm)

    @pl.when(i >= n_a)
    def _():
        _proj_ffn_body(xb_ref, ab_ref, *shared, ob_ref, final_norm=final_norm)


def _proj_ffn(xa, aa, xb, ab, wo, g, wg, wu, wd, layer, gf, final_norm):
    d = xa.shape[1]
    ka = aa.shape[1]
    ff = wg.shape[2]
    n_a, n_b = xa.shape[0] // ROW_TILE, xb.shape[0] // ROW_TILE
    const = lambda shape: pl.BlockSpec(shape, lambda i: (0, 0), pipeline_mode=pl.Buffered(1))
    of_layer = lambda shape: pl.BlockSpec((None,) + shape, lambda i: (layer, 0, 0), pipeline_mode=pl.Buffered(1))
    rows_a = lambda w: pl.BlockSpec((ROW_TILE, w), lambda i: (jnp.minimum(i, n_a - 1), 0))
    rows_b = lambda w: pl.BlockSpec((ROW_TILE, w), lambda i: (jnp.maximum(i - n_a, 0), 0))
    return pl.pallas_call(
        functools.partial(_proj_ffn_pair_body, final_norm=final_norm, n_a=n_a),
        grid=(n_a + n_b,),
        in_specs=[rows_a(d), rows_a(ka), rows_b(d), rows_b(ka),
                  const((ka, d)), const((1, d)), of_layer((d, ff)), of_layer((d, ff)), of_layer((ff, d)),
                  const((1, d))],
        out_specs=[rows_a(d), rows_b(d)],
        out_shape=[jax.ShapeDtypeStruct(xa.shape, F32), jax.ShapeDtypeStruct(xb.shape, F32)],
        compiler_params=pltpu.CompilerParams(dimension_semantics=("arbitrary",),
                                             vmem_limit_bytes=VMEM_LIMIT_BYTES),
        name="proj_ffn",
    )(xa, aa, xb, ab, wo, g.reshape(1, d), wg, wu, wd, gf.reshape(1, d))


def _hgrn_consts(L):
    r = np.arange(L)[:, None]
    t = np.arange(L)[None, :]
    sels = [t <= r]
    masks = [r == t]
    h = L // 2
    while h >= 1:
        base = (r // (2 * h)) * (2 * h)
        upper = (r % (2 * h)) >= h
        if h < SUBLANES:
            sels.append(np.where(upper, (t >= base + h) & (t <= r), (t > r) & (t <= base + h - 1)))
        masks.append((r // (2 * h) == t // (2 * h)) & upper & ((t % (2 * h)) < h))
        h //= 2
    return np.concatenate(sels, axis=0).astype(np.float32), np.stack(masks).astype(np.float32)


def _decay_exponents(sel_ref, logf, L):
    blk = min(SEL_BLOCK, L)
    nb = L // blk
    res = [_exact_dot(sel_ref[...], logf[i * blk:(i + 1) * blk]) for i in range(nb)]
    cum = [r[0:blk] for r in res]
    pre = [jnp.zeros_like(cum[0][0:1])]
    for i in range(nb):
        pre.append(pre[i] + cum[i][blk - 1:blk])
    cat = lambda parts: jnp.concatenate(parts, axis=0) if len(parts) > 1 else parts[0]
    b = cat([cum[i] + pre[i] for i in range(nb)])
    levels = []
    nbh = nb // 2
    while nbh >= 1:
        parts = []
        for i in range(nb):
            ref = pre[(i // (2 * nbh)) * 2 * nbh + nbh]
            if (i % (2 * nbh)) >= nbh:
                parts.append(cum[i] + (pre[i] - ref))
            else:
                parts.append((ref - pre[i]) - cum[i])
        levels.append(cat(parts))
        nbh //= 2
    h = blk // 2
    while h >= SUBLANES:
        parts = []
        for i in range(nb):
            for base in range(0, blk, 2 * h):
                ref = cum[i][base + h - 1:base + h]
                parts += [ref - cum[i][base:base + h], cum[i][base + h:base + 2 * h] - ref]
        levels.append(cat(parts))
        h //= 2
    for lev in range(sel_ref.shape[0] // blk - 1):
        levels.append(cat([r[(lev + 1) * blk:(lev + 2) * blk] for r in res]))
    return b, pre[nb], levels


def _retention_consts(L):
    lg = np.log1p(-np.exp2(-5.0 - np.arange(N_HEADS, dtype=np.float32))).astype(np.float32)
    idx = np.arange(L, dtype=np.float32)
    rel = idx[:, None] - idx[None, :]
    dmat = np.where(rel >= 0, np.exp(lg[:, None, None] * np.maximum(rel, 0.0)), 0.0)
    w_in = np.exp(lg[:, None] * (idx + 1.0))
    w_tail = np.exp(lg[:, None] * (L - 1.0 - idx))
    bcast = lambda v: jnp.asarray(np.broadcast_to(v[:, :, None], (N_HEADS, L, HEAD_DK)).astype(np.float32))
    g_chunk = [float(v) for v in np.exp(lg * L).astype(np.float32)]
    return jnp.asarray(dmat.astype(np.float32)), bcast(w_in), bcast(w_tail), g_chunk


def _rope_tables(pos0, T):
    half = HEAD_DK // 2
    inv = ROPE_BASE ** (-jnp.arange(half, dtype=F32) / half)
    ang = (jnp.arange(T) + pos0).astype(F32)[:, None] * inv[None, :]
    cos, sin = jnp.cos(ang), jnp.sin(ang)
    return jnp.concatenate([cos, cos], axis=-1), jnp.concatenate([-sin, sin], axis=-1)


def _hgrn_intra(units, mask_ref, L, fillers=()):
    sub = min(L, HGRN_SUB)
    n_sub = L // sub
    n_lev = mask_ref.shape[0] - 1
    top = len(units[0][3]) - n_lev
    assert n_sub in (1, 2) and top == n_sub - 1
    pieces = [(u, slice(i * sub, (i + 1) * sub)) for u in range(len(units)) for i in range(n_sub)]
    atts = [mask_ref[0] * _dot_nt(units[u][0][r], units[u][1][r]) for u, r in pieces]
    fillers = list(fillers)
    every = max(1, n_lev // (len(fillers) + 1))
    for lev in range(n_lev):
        for p, (u, r) in enumerate(pieces):
            q, k, _, expo = units[u]
            sc = jnp.exp(expo[top + lev][r])
            atts[p] = atts[p] + mask_ref[lev + 1] * _dot_nt(q[r] * sc, k[r] * sc)
        if fillers and (lev + 1) % every == 0:
            fillers.pop(0)()
    while fillers:
        fillers.pop(0)()
    if n_sub == 1:
        return [_dot(att, units[u][2]) for att, (u, _) in zip(atts, pieces)]
    outs = []
    for u, (q, k, v, expo) in enumerate(units):
        cross = _dot_nt(q[sub:] * jnp.exp(expo[0][sub:]), k[:sub] * jnp.exp(expo[0][:sub]))
        outs.append(jnp.concatenate([_dot(atts[2 * u], v[:sub]),
                                     _dot(jnp.concatenate([cross, atts[2 * u + 1]], axis=1), v)], axis=0))
    return outs


def _mixer0_body(*refs, L, layer_slot, g_chunk, fused, n_side, side_every):
    it = iter(refs)
    take = lambda n: [next(it) for _ in range(n)]
    src = take(3 if fused else 1)
    (cos_ref, sin_ref, lbl_ref, ga_ref, gb_ref, sel_ref, mask_ref, dmat_ref, win_ref, wtail_ref,
     sa0_ref, sb0_ref) = take(12)
    side_in = take(n_side)
    mix_ref, sa_ref, sb_ref = take(3)
    side_out = take(n_side)
    sat_scr, sbt_scr = take(2)
    c = pl.program_id(1)
    last = pl.num_programs(1) - 1
    head_w = IN0_WIDTH // N_HEADS
    G = sa_ref.shape[0]

    if n_side:
        @pl.when((pl.program_id(0) * pl.num_programs(1) + c) % side_every == 0)
        def _():
            for i_ref, o_ref in zip(side_in, side_out):
                o_ref[...] = i_ref[...].astype(BF16)

    @pl.when(c == 0)
    def _():
        for gi in range(G):
            for hh in range(N_HEADS):
                sat_scr[gi * N_HEADS + hh] = sa0_ref[gi, hh].T
                sbt_scr[gi * N_HEADS + hh] = sb0_ref[gi, hh].T

    lbl = lbl_ref[...]
    e = jnp.exp(lbl - jnp.max(lbl, axis=0, keepdims=True))
    lb_all = jnp.sum(e[:layer_slot + 1], axis=0, keepdims=True) / jnp.sum(e, axis=0, keepdims=True)

    if fused:
        x_ref, g_ref, w_ref = src
        hb = _rms(x_ref[...], g_ref[...]).astype(BF16)
    cosf = cos_ref[...]
    sinf = sin_ref[...]

    def project(hh):
        hcols = slice(hh * head_w, (hh + 1) * head_w)
        return jnp.dot(hb, w_ref[:, hcols], preferred_element_type=F32) if fused else src[0][:, hcols]

    def run_phase(heads, z_of, fillers):
        hg_units, ret_units = [], []
        for hh in heads:
            hs = slice(hh * HEAD_DK, (hh + 1) * HEAD_DK)
            lb = lb_all[:, hs]
            for gi in range(G):
                rs = slice(gi * L, (gi + 1) * L)
                si = gi * N_HEADS + hh
                zh = z_of[hh][rs]
                part = lambda j, zh=zh: zh[:, j * HEAD_DK:(j + 1) * HEAD_DK]
                f = lb + (1.0 - lb) * _sigmoid(part(1))
                b, b_last, expo = _decay_exponents(sel_ref, jnp.log(f), L)
                hg_units.append((rs, hs, si, part(0), 1.0 - f, part(2), part(3), b, b_last, expo))
                rq = part(4)
                rk = part(5)
                rq = rq * cosf + pltpu.roll(rq, HEAD_DK // 2, 1) * sinf
                rk = (rk * cosf + pltpu.roll(rk, HEAD_DK // 2, 1) * sinf) * (HEAD_DK ** -0.5)
                ret_units.append((rs, hh, si, rq, rk, part(6), part(7)))

        states = [sat_scr[u[2]] for u in hg_units]
        inter = [_dot_nt(u[3] * jnp.exp(u[7]), st) for u, st in zip(hg_units, states)]
        intra = _hgrn_intra([(u[3], u[4], u[5], u[9]) for u in hg_units], mask_ref, L, fillers)
        for (rs, hs, si, q, k, v, og, b, b_last, _), st, o1, o2 in zip(hg_units, states, inter, intra):
            sat_scr[si] = st * jnp.exp(b_last) + _dot_tn(v, k * jnp.exp(b_last - b))
            o = _rms((o1 + o2) * _sigmoid(og), ga_ref[...])
            mix_ref[rs, hs] = o.astype(mix_ref.dtype)

        states = [sbt_scr[si] for _, _, si, _, _, _, _ in ret_units]
        scores = [_dot_nt(rq, rk) * dmat_ref[hh] for _, hh, _, rq, rk, _, _ in ret_units]
        inter = [_dot_nt(u[3], rt) for u, rt in zip(ret_units, states)]
        outs = [_dot(sc, u[5]) + win_ref[u[1]] * it_ for u, sc, it_ in zip(ret_units, scores, inter)]
        for (rs, hh, si, rq, rk, rv, rg), rt, ro in zip(ret_units, states, outs):
            sbt_scr[si] = g_chunk[hh] * rt + _dot_tn(rv, rk * wtail_ref[hh])
            ro = _rms(ro, gb_ref[...]) * _silu(rg)
            mix_ref[rs, KW + hh * HEAD_DK:KW + (hh + 1) * HEAD_DK] = ro.astype(mix_ref.dtype)

    if fused:
        per = PHASE_HEADS
        z_of = {hh: project(hh) for hh in range(per)}
        for h0 in range(0, N_HEADS, per):
            nxt = range(h0 + per, min(h0 + 2 * per, N_HEADS))
            run_phase(range(h0, h0 + per), z_of, [lambda hh=hh: z_of.__setitem__(hh, project(hh)) for hh in nxt])
    else:
        run_phase(range(N_HEADS), {hh: project(hh) for hh in range(N_HEADS)}, [])

    @pl.when(c == last)
    def _():
        for gi in range(G):
            for hh in range(N_HEADS):
                sa_ref[gi, hh] = sat_scr[gi * N_HEADS + hh].T
                sb_ref[gi, hh] = sbt_scr[gi * N_HEADS + hh].T


def _mixer0(src, n_streams, T, pos0, sa0, sb0, lb_logits, ga, gb, layer_slot, side_cast=()):
    L = min(CHUNK0, T)
    nc = T // L
    G = max(1, min(n_streams, STEP_ROWS0 // L)) if nc == 1 else 1
    fused = len(src) == 3
    sel = jnp.asarray(_hgrn_consts(min(SEL_BLOCK, L))[0], BF16)
    masks = jnp.asarray(_hgrn_consts(min(HGRN_SUB, L))[1])
    dmat, w_in, w_tail, g_chunk = _retention_consts(L)
    cosf, sinf = _rope_tables(pos0, T)
    full = lambda a: pl.BlockSpec(a.shape, lambda s, c: (0,) * a.ndim, pipeline_mode=pl.Buffered(1))
    rows = lambda a: pl.BlockSpec((G * L, a.shape[1]), lambda s, c: (s * nc + c, 0))
    state_spec = pl.BlockSpec((G, N_HEADS, HEAD_DK, HEAD_DK), lambda s, c: (s, 0, 0, 0))
    state_shape = jax.ShapeDtypeStruct((n_streams, N_HEADS, HEAD_DK, HEAD_DK), F32)
    ga2, gb2 = ga.reshape(1, HEAD_DK), gb.reshape(1, HEAD_DK)
    if fused:
        src = (src[0], src[1].reshape(1, -1), src[2])
        src_specs = [rows(src[0]), full(src[1]), full(src[2])]
    else:
        src_specs = [rows(src[0])]
    consts = (lb_logits, ga2, gb2, sel, masks, dmat, w_in, w_tail)
    side = [w.reshape(-1, w.shape[-1]) for w in side_cast]
    per_block = (n_streams // G) * nc // SIDE_BLOCKS if side else 1
    side_spec = lambda w: pl.BlockSpec((w.shape[0] // SIDE_BLOCKS, w.shape[1]),
                                       lambda s, c: ((s * nc + c) // per_block, 0))
    outs = pl.pallas_call(
        functools.partial(_mixer0_body, L=L, layer_slot=layer_slot, g_chunk=g_chunk, fused=fused,
                          n_side=len(side), side_every=per_block),
        grid=(n_streams // G, nc),
        in_specs=src_specs
                 + [pl.BlockSpec((L, HEAD_DK), lambda s, c: (c, 0)), pl.BlockSpec((L, HEAD_DK), lambda s, c: (c, 0))]
                 + [full(a) for a in consts] + [state_spec, state_spec] + [side_spec(w) for w in side],
        out_specs=[pl.BlockSpec((G * L, 2 * KW), lambda s, c: (s * nc + c, 0)), state_spec, state_spec]
                  + [side_spec(w) for w in side],
        out_shape=[jax.ShapeDtypeStruct((n_streams * T, 2 * KW), BF16), state_shape, state_shape]
                  + [jax.ShapeDtypeStruct(w.shape, BF16) for w in side],
        scratch_shapes=[pltpu.VMEM((G * N_HEADS, HEAD_DK, HEAD_DK), F32),
                        pltpu.VMEM((G * N_HEADS, HEAD_DK, HEAD_DK), F32)],
        compiler_params=pltpu.CompilerParams(dimension_semantics=("arbitrary", "arbitrary"),
                                             vmem_limit_bytes=VMEM_LIMIT_BYTES),
        name="mixer0_T%d" % T,
    )(*src, cosf, sinf, *consts, sa0, sb0, *side)
    return outs[0], outs[1], outs[2], [o.reshape(w.shape) for o, w in zip(outs[3:], side_cast)]


def _mixer1_body(*refs, L, fused, zero_init, single_chunk):
    it = iter(refs)
    take = lambda n: [next(it) for _ in range(n)]
    src = take(3 if fused else 1)
    (cw_ref, cb_ref, wq_ref, wk_ref, wv_ref, wgate_ref, bgate_ref, tri_ref, gn_ref, skip_ref) = take(10)
    init = None if zero_init else take(4)
    hg_ref, c_ref, n_ref, m_ref, conv_ref = take(5)
    xbuf = take(1)[0]
    c = pl.program_id(1)
    last = pl.num_programs(1) - 1
    n_taps = C_CONV - 1
    G = c_ref.shape[0]
    kv_state = zero_init

    @pl.when(c == 0)
    def _():
        if zero_init:
            c_ref[...] = jnp.zeros_like(c_ref)
            n_ref[...] = jnp.zeros_like(n_ref)
            m_ref[...] = jnp.zeros_like(m_ref)
            xbuf[:, 0:CONV_HDR, :] = jnp.zeros((G, CONV_HDR, C_INNER), F32)
        else:
            c0_ref, n0_ref, m0_ref, conv0_ref = init
            if not single_chunk:
                c_ref[...] = c0_ref[...]
                n_ref[...] = n0_ref[...]
                m_ref[...] = m0_ref[...]
            xbuf[:, CONV_HDR - n_taps:CONV_HDR, :] = conv0_ref[...]

    c_in, n_in, m_in = init[:3] if single_chunk and not zero_init else (c_ref, n_ref, m_ref)

    if fused:
        x_ref, g_ref, w_ref = src
        hb = _rms(x_ref[...], g_ref[...]).astype(BF16)
        xm = jnp.dot(hb, w_ref[:, :C_INNER], preferred_element_type=F32)
        zg = jnp.dot(hb, w_ref[:, C_INNER:], preferred_element_type=F32)
    else:
        xm = src[0][:, :C_INNER]
        zg = src[0][:, C_INNER:]

    xcs, tails = [], []
    for gi in range(G):
        xg = xm[gi * L:(gi + 1) * L]
        xbuf[gi, CONV_HDR:CONV_HDR + L, :] = xg
        acc = cb_ref[...] + xg * cw_ref[n_taps:n_taps + 1, :]
        for w in range(n_taps):
            acc = acc + xbuf[gi, CONV_HDR - n_taps + w:CONV_HDR - n_taps + w + L, :] * cw_ref[w:w + 1, :]
        tails.append(xbuf[gi, CONV_HDR + L - n_taps:CONV_HDR + L, :])
        xbuf[gi, CONV_HDR - n_taps:CONV_HDR, :] = tails[gi]
        xcs.append(acc)
    xc = _silu(jnp.concatenate(xcs, axis=0) if G > 1 else xcs[0])

    xcb = xc.astype(BF16)
    xmb = xm.astype(BF16)
    qs, ks, vs = [], [], []
    for g in range(C_INNER // BD_GROUP):
        gs = slice(g * BD_GROUP, (g + 1) * BD_GROUP)
        qs.append(jnp.dot(xcb[:, gs], wq_ref[g], preferred_element_type=F32))
        ks.append(jnp.dot(xcb[:, gs], wk_ref[g], preferred_element_type=F32))
        vs.append(jnp.dot(xmb[:, gs], wv_ref[g], preferred_element_type=F32))
    q = jnp.concatenate(qs, axis=1).astype(BF16)
    k = (jnp.concatenate(ks, axis=1) * (C_DH ** -0.5)).astype(BF16)
    v = jnp.concatenate(vs, axis=1).astype(BF16)

    gates = (_dot(q, wgate_ref[0:C_INNER, :]) + _dot(k, wgate_ref[C_INNER:2 * C_INNER, :])
             + _dot(v, wgate_ref[2 * C_INNER:3 * C_INNER, :]) + bgate_ref[...])
    bcum = _exact_dot(tri_ref[...], _log_sigmoid(gates))
    lane = lax.broadcasted_iota(jnp.int32, gates.shape, 1)
    rows = jnp.where(lane < N_HEADS, gates, bcum).T
    ti = lax.broadcasted_iota(jnp.int32, (L, L), 0)
    si = lax.broadcasted_iota(jnp.int32, (L, L), 1)
    causal = si <= ti

    units = []
    for gi, hh in [(gi, hh) for gi in range(G) for hh in range(N_HEADS)]:
        hs = slice(hh * C_DH, (hh + 1) * C_DH)
        rs = slice(gi * L, (gi + 1) * L)
        b_col = bcum[rs, N_HEADS + hh:N_HEADS + hh + 1]
        i_col = gates[rs, hh:hh + 1]
        b_row = rows[N_HEADS + hh:N_HEADS + hh + 1, rs]
        i_row = rows[hh:hh + 1, rs]
        m_prev = m_in[gi, :, hh:hh + 1]
        lw = jnp.where(causal, b_col - b_row + i_row, -jnp.inf)
        lp = b_col + m_prev
        mj = jnp.maximum(lp, jnp.max(lw, axis=-1, keepdims=True))
        m_new = mj[L - 1:L, :]
        b_last = b_col[L - 1:L, :]
        units.append(dict(
            gi=gi, hh=hh, hs=hs, rs=rs, mj=mj, m_new=m_new, wgt=jnp.exp(lw - mj), wp=jnp.exp(lp - mj),
            ws=jnp.exp(b_last - b_col + i_col - m_new),
            wpl=jnp.exp(b_last + m_prev - m_new),
            cm=c_in[gi, hh],
            nv=n_in[gi, hh:hh + 1, :],
            q=q[rs, hs], k=k[rs, hs], v=v[rs, hs]))

    scores = [_dot_nt(u['q'], u['k']) * u['wgt'] for u in units]
    inter = [_dot(u['q'], u['cm']) if kv_state else _dot_nt(u['q'], u['cm']) for u in units]
    cells = []
    for u, s, it_ in zip(units, scores, inter):
        num = _dot(s, u['v']) + u['wp'] * it_
        den = jnp.sum(s, axis=-1, keepdims=True) + u['wp'] * jnp.sum(u['q'] * u['nv'], axis=-1, keepdims=True)
        cells.append(num / jnp.maximum(jnp.abs(den), jnp.exp(-u['mj'])))
    for u in units:
        gi, hh, vw = u['gi'], u['hh'], u['v'] * u['ws']
        c_ref[gi, hh] = u['wpl'] * u['cm'] + (_dot_tn(u['k'], vw) if kv_state else _dot_tn(vw, u['k']))
        n_ref[gi, hh:hh + 1, :] = u['wpl'] * u['nv'] + jnp.sum(u['k'] * u['ws'], axis=0, keepdims=True)
        m_ref[gi, :, hh:hh + 1] = u['m_new']
    for u, hcell in zip(units, cells):
        hs, rs = u['hs'], u['rs']
        hc = hcell - jnp.mean(hcell, axis=-1, keepdims=True)
        hn = hc * lax.rsqrt(jnp.mean(hc * hc, axis=-1, keepdims=True) + EPS) * gn_ref[...]
        out = (hn + skip_ref[:, hs] * xc[rs, hs]) * _silu(zg[rs, hs])
        hg_ref[rs, hs] = out.astype(hg_ref.dtype)

    @pl.when(c == last)
    def _():
        for gi in range(G):
            conv_ref[gi] = tails[gi]
            if kv_state:
                for hh in range(N_HEADS):
                    c_ref[gi, hh] = c_ref[gi, hh].T


def _mixer1(src, n_streams, T, states, cw, cb, wq, wk, wv, wgate, bgate, gn, skip):
    L = min(CHUNK1, T)
    nc = T // L
    G = max(1, min(n_streams, STEP_ROWS1 // L)) if nc == 1 else 1
    fused = len(src) == 3
    zero_init = states is None
    tri = jnp.asarray(np.kron(np.eye(G, dtype=np.float32), np.tril(np.ones((L, L), np.float32))), BF16)
    full = lambda a: pl.BlockSpec(a.shape, lambda s, c: (0,) * a.ndim, pipeline_mode=pl.Buffered(1))
    rows = lambda a: pl.BlockSpec((G * L, a.shape[1]), lambda s, c: (s * nc + c, 0))
    per_stream = lambda shape: pl.BlockSpec((G,) + shape[1:], lambda s, c: (s,) + (0,) * (len(shape) - 1))
    gn2, skip2, cb2 = gn.reshape(1, C_DH), skip.reshape(1, C_INNER), cb.reshape(1, C_INNER)
    state_shapes = ((n_streams, N_HEADS, C_DH, C_DH), (n_streams, N_HEADS, C_DH), (n_streams, 1, N_HEADS),
                    (n_streams, C_CONV - 1, C_INNER))
    scratch = [pltpu.VMEM((G, CONV_HDR + L, C_INNER), F32)]
    if fused:
        src = (src[0], src[1].reshape(1, -1), src[2])
        src_specs = [rows(src[0]), full(src[1]), full(src[2])]
    else:
        src_specs = [rows(src[0])]
    consts = (cw, cb2, wq, wk, wv, wgate, bgate, tri, gn2, skip2)
    if not zero_init:
        states = tuple(a.reshape(shape) for a, shape in zip(states, state_shapes))
    hg, c1, n1, m1, conv1 = pl.pallas_call(
        functools.partial(_mixer1_body, L=L, fused=fused, zero_init=zero_init, single_chunk=nc == 1),
        grid=(n_streams // G, nc),
        in_specs=src_specs + [full(a) for a in consts]
                 + ([] if zero_init else [per_stream(shape) for shape in state_shapes]),
        out_specs=[pl.BlockSpec((G * L, C_INNER), lambda s, c: (s * nc + c, 0))]
                  + [per_stream(shape) for shape in state_shapes],
        out_shape=[jax.ShapeDtypeStruct((n_streams * T, C_INNER), BF16)]
                  + [jax.ShapeDtypeStruct(shape, F32) for shape in state_shapes],
        scratch_shapes=scratch,
        compiler_params=pltpu.CompilerParams(dimension_semantics=("arbitrary", "arbitrary"),
                                             vmem_limit_bytes=VMEM_LIMIT_BYTES),
        name="mixer1_T%d" % T,
    )(*src, *consts, *(() if zero_init else states))
    return hg, c1, n1, m1.reshape(n_streams, N_HEADS), conv1


def _dense_blockdiag(w):
    rows = w.reshape(-1, BD_GROUP, C_BLOCK)
    tiled = jnp.tile(rows, (1, 1, BD_GROUP // C_BLOCK))
    idx = np.arange(BD_GROUP) // C_BLOCK
    same_block = jnp.asarray(idx[:, None] == idx[None, :])
    return jnp.where(same_block, tiled, 0.0).astype(BF16)


def kernel(x_prompt, x_sample, state_hgrn, state_ret, state_mlstm_c, state_mlstm_n, state_mlstm_m, state_conv,
           norm_mix, norm_ffn, norm_final, w_in0, lb_logits, hgrn_norm, ret_norm, w_out0,
           w_up1, conv_w, conv_b, w_q1, w_k1, w_v1, w_ig, b_ig, w_fg, b_fg, mlstm_norm, mlstm_skip, w_down1,
           w_ffn_gate, w_ffn_up, w_ffn_down):
    bp, tp, d = x_prompt.shape
    bs, ts, _ = x_sample.shape
    mp, ms = bp * tp, bs * ts
    zeros = lambda *shape: jnp.zeros(shape, F32)
    w_in, = _cast_weights([w_in0[0]])
    later_weights = [w_out0[0], w_up1[0], w_down1[0], w_ffn_gate, w_ffn_up, w_ffn_down]

    xp = x_prompt.reshape(mp, d)
    xs = x_sample.reshape(ms, d)

    mix_p, hg_p, rt_p, (w_out, w_up, w_down, wf_gate, wf_up, wf_down) = _mixer0(
        (xp, norm_mix[0], w_in), bp, tp, 0, zeros(bp, N_HEADS, HEAD_DK, HEAD_DK),
        zeros(bp, N_HEADS, HEAD_DK, HEAD_DK), lb_logits, hgrn_norm[0], ret_norm[0], 0, side_cast=later_weights)
    ffn0 = (w_out, norm_ffn[0], wf_gate, wf_up, wf_down, 0, norm_final, False)
    mix_s, hg_s, rt_s, _ = _mixer0((_norm_matmul(xs, norm_mix[0], w_in),), bs, ts, PAST_LEN, state_hgrn[0],
                                   state_ret[0], lb_logits, hgrn_norm[0], ret_norm[0], 0)
    xp, xs = _proj_ffn(xp, mix_p, xs, mix_s, *ffn0)

    ffn1 = (w_down, norm_ffn[1], wf_gate, wf_up, wf_down, 1, norm_final, True)
    wgate = jnp.pad(jnp.concatenate([w_ig[0], w_fg[0]], axis=1), ((0, 0), (0, GATE_LANES - 2 * N_HEADS))).astype(BF16)
    bgate = jnp.pad(jnp.concatenate([b_ig[0], b_fg[0]]), (0, GATE_LANES - 2 * N_HEADS)).reshape(1, GATE_LANES)
    m1_consts = (conv_w[0], conv_b[0], _dense_blockdiag(w_q1[0]), _dense_blockdiag(w_k1[0]),
                 _dense_blockdiag(w_v1[0]), wgate, bgate, mlstm_norm[0], mlstm_skip[0])
    hg1_p, mc_p, mn_p, mm_p, cv_p = _mixer1((xp, norm_mix[1], w_up), bp, tp, None, *m1_consts)
    hg1_s, mc_s, mn_s, mm_s, cv_s = _mixer1((_norm_matmul(xs, norm_mix[1], w_up),), bs, ts,
                                            (state_mlstm_c[0], state_mlstm_n[0], state_mlstm_m[0], state_conv[0]),
                                            *m1_consts)
    yp, ys = _proj_ffn(xp, hg1_p, xs, hg1_s, *ffn1)

    lead = lambda a: a[None]
    return (yp.reshape(bp, tp, d), ys.reshape(bs, ts, d),
            lead(hg_p), lead(hg_s), lead(rt_p), lead(rt_s),
            lead(mc_p), lead(mc_s), lead(mn_p), lead(mn_s), lead(mm_p), lead(mm_s), lead(cv_p), lead(cv_s))
```
